```python
import math
import jax
import jax.numpy as jnp
from jax import lax
import numpy as np

D_MODEL = 2048
BATCH = 2
SEQ = 4096
DEPTH = 4

GRID_W = 64
CTX_LEN = 256

MIX_WIDTH = D_MODEL
SSD_INNER = MIX_WIDTH // 2
SSD_HEAD_DIM = 64
SSD_HEADS = SSD_INNER // SSD_HEAD_DIM
SSD_GROUPS = 4
SSD_STATE = 128
SSD_CONV = 5
SSD_CHUNK = 128
SSD_CONV_DIM = SSD_INNER + 2 * SSD_GROUPS * SSD_STATE
CONF_DIM = MIX_WIDTH // 4
CONF_KERNEL = 31
S5_DIM = MIX_WIDTH - SSD_INNER - CONF_DIM
S5_GROUP = 16
S5_GROUPS = S5_DIM // S5_GROUP
S5_STATE = 64
IN_SPLITS = (SSD_INNER,
             SSD_INNER + SSD_CONV_DIM,
             SSD_INNER + SSD_CONV_DIM + SSD_HEADS,
             SSD_INNER + SSD_CONV_DIM + SSD_HEADS + 2 * CONF_DIM)
IN_PROJ_DIM = IN_SPLITS[-1] + S5_DIM
N_EXPERTS = 32
TOP_K = 4
D_EXPERT = 768
SWIGLU_LIMIT = 7.0
SWIGLU_ALPHA = 1.702
MOE_BLOCK = 256
EPS = 1e-6

kernel_name = 'hymba_style_ssd_conformer_s5_moe_dit'


def rms_norm(x, g):
    xf = x.astype(jnp.float32)
    y = xf * lax.rsqrt(jnp.mean(xf * xf, axis=-1, keepdims=True) + EPS)
    return (y * g.astype(jnp.float32)).astype(x.dtype)


def layer_norm(x, g, b):
    xf = x.astype(jnp.float32)
    mu = jnp.mean(xf, axis=-1, keepdims=True)
    var = jnp.mean(jnp.square(xf - mu), axis=-1, keepdims=True)
    y = (xf - mu) * lax.rsqrt(var + EPS)
    return (y * g.astype(jnp.float32) + b.astype(jnp.float32)).astype(x.dtype)


def modulate(h, shift, scale):
    return h * (1.0 + scale) + shift


def dw_conv(u, w, b):
    pad = (w.shape[0] - 1) // 2
    y = lax.conv_general_dilated(u, w[:, None, :], window_strides=(1,), padding=[(pad, pad)],
                                 dimension_numbers=('NWC', 'WIO', 'NWC'),
                                 feature_group_count=u.shape[-1])
    return y + b


def flip(t, rev):
    return t[:, ::-1] if rev else t


def to_col_major(u):
    b, n, ch = u.shape
    rows = n // GRID_W
    return u.reshape(b, rows, GRID_W, ch).transpose(0, 2, 1, 3).reshape(b, n, ch)


def from_col_major(u):
    b, n, ch = u.shape
    rows = n // GRID_W
    return u.reshape(b, GRID_W, rows, ch).transpose(0, 2, 1, 3).reshape(b, n, ch)


def ssd_inputs(xbc, conv_w, conv_b):
    b, n, _ = xbc.shape
    xbc = jax.nn.silu(dw_conv(xbc, conv_w, conv_b))
    xs, bm, cm = jnp.split(xbc, [SSD_INNER, SSD_INNER + SSD_GROUPS * SSD_STATE], axis=-1)
    return (xs.reshape(b, n, SSD_HEADS, SSD_HEAD_DIM),
            bm.reshape(b, n, SSD_GROUPS, SSD_STATE),
            cm.reshape(b, n, SSD_GROUPS, SSD_STATE))


def ssd_chunk_scan(x, dt, a_head, bm, cm, h0):
    b, n, nh, p = x.shape
    g, ns = bm.shape[2], bm.shape[3]
    r = nh // g
    q = SSD_CHUNK
    nc = n // q
    f32 = jnp.float32
    xf = (x.astype(f32) * dt[..., None]).reshape(b, nc, q, g, r, p)
    a = (dt * a_head).reshape(b, nc, q, g, r)
    bf = bm.astype(f32).reshape(b, nc, q, g, ns)
    cf = cm.astype(f32).reshape(b, nc, q, g, ns)
    a_cs = jnp.cumsum(a, axis=2)
    causal = jnp.tril(jnp.ones((q, q), bool))
    seg = a_cs[:, :, :, None] - a_cs[:, :, None, :]
    decay_in = jnp.exp(jnp.where(causal[:, :, None, None], seg, -jnp.inf))
    cb = jnp.einsum('bclgn,bcsgn->bclsg', cf, bf)
    y_diag = jnp.einsum('bclsgr,bcsgrp->bclgrp', cb[..., None] * decay_in, xf)
    decay_to_end = jnp.exp(a_cs[:, :, -1:] - a_cs)
    chunk_states = jnp.einsum('bcsgn,bcsgrp->bcgrpn', bf, xf * decay_to_end[..., None])
    chunk_decay = jnp.pad(a_cs[:, :, -1], ((0, 0), (1, 0), (0, 0), (0, 0)))
    cd_cs = jnp.cumsum(chunk_decay, axis=1)
    seg_c = cd_cs[:, :, None] - cd_cs[:, None, :]
    causal_c = jnp.tril(jnp.ones((nc + 1, nc + 1), bool))
    decay_c = jnp.exp(jnp.where(causal_c[:, :, None, None], seg_c, -jnp.inf))
    states = jnp.concatenate([h0[:, None], chunk_states], axis=1)
    states = jnp.einsum('bzcgr,bcgrpn->bzgrpn', decay_c, states)
    y_off = jnp.einsum('bclgn,bcgrpn->bclgrp', cf, states[:, :-1]) * jnp.exp(a_cs)[..., None]
    y = (y_diag + y_off).reshape(b, n, nh, p)
    return y, states[:, -1]


def conformer_conv(p, dw_w, dw_b, ln_g, ln_b, pw_w, pw_b, out_g):
    v, gt = jnp.split(p, 2, axis=-1)
    u = v * jax.nn.sigmoid(gt)
    u = dw_conv(u, dw_w, dw_b)
    u = jax.nn.silu(layer_norm(u, ln_g, ln_b))
    return rms_norm(u @ pw_w + pw_b, out_g)


def s5_discretize(lam_re, lam_im, log_step, b_re, b_im, c_re, c_im):
    f32 = jnp.float32
    lam = lax.complex(jnp.minimum(lam_re.astype(f32), -1e-4), lam_im.astype(f32))
    step = jnp.exp(log_step.astype(f32))[:, None]
    lam_bar = jnp.exp(lam * step)
    b_bar = ((lam_bar - 1.0) / lam)[..., None] * lax.complex(b_re.astype(f32), b_im.astype(f32))
    c_mat = lax.complex(c_re.astype(f32), c_im.astype(f32))
    return lam_bar, b_bar, c_mat


def _linear_recurrence(e1, e2):
    a1, b1 = e1
    a2, b2 = e2
    return a1 * a2, a2 * b1 + b2


def s5_scan(u, lam_bar, b_bar, c_mat, h0):
    bu = lax.complex(jnp.einsum('gpk,blgk->blgp', jnp.real(b_bar), u),
                     jnp.einsum('gpk,blgk->blgp', jnp.imag(b_bar), u))
    a = jnp.broadcast_to(lam_bar, bu.shape)
    a_cum, h = lax.associative_scan(_linear_recurrence, (a, bu), axis=1)
    h = h + a_cum * h0[:, None]
    y = jnp.real(jnp.einsum('gkp,blgp->blgk', c_mat, h))
    return y, h[:, -1]


def s5_glu(y, w, b, g):
    y = jax.nn.gelu(y)
    return rms_norm(y * jax.nn.sigmoid(y @ w + b), g)


def hybrid_mixer(hl, hc, w_in, ssd_conv_w, ssd_conv_b, ssd_a_log, ssd_dt_bias, ssd_d, ssd_norm_g,
                 conf_dw_w, conf_dw_b, conf_ln_g, conf_ln_b, conf_pw_w, conf_pw_b, conf_out_g,
                 s5_lam_re, s5_lam_im, s5_log_step, s5_b_re, s5_b_im, s5_c_re, s5_c_im,
                 s5_d, s5_glu_w, s5_glu_b, s5_out_g, w_out, with_ctx_out):
    f32 = jnp.float32
    dtype = hl.dtype
    bsz, n_lat, _ = hl.shape
    n_ctx = hc.shape[1]
    z_l, xbc_l, dt_l, conf_l, s5_l = jnp.split(hl @ w_in, IN_SPLITS, axis=-1)
    z_c, xbc_c, dt_c, conf_c, s5_c = jnp.split(hc @ w_in, IN_SPLITS, axis=-1)

    xs_l, bm_l, cm_l = ssd_inputs(xbc_l, ssd_conv_w, ssd_conv_b)
    xs_c, bm_c, cm_c = ssd_inputs(xbc_c, ssd_conv_w, ssd_conv_b)
    d_skip = ssd_d.astype(f32)[:, None]
    y_l = d_skip * xs_l.astype(f32)
    y_c = d_skip * xs_c.astype(f32)
    h0 = jnp.zeros((bsz, SSD_GROUPS, SSD_HEADS // SSD_GROUPS, SSD_HEAD_DIM, SSD_STATE), f32)
    for d in range(2):
        rev = d == 1
        a_head = -jnp.exp(ssd_a_log[d].astype(f32))
        dt_bias = ssd_dt_bias[d].astype(f32)
        dtd_l = jax.nn.softplus(dt_l.astype(f32) + dt_bias)
        dtd_c = jax.nn.softplus(dt_c.astype(f32) + dt_bias)
        yd_c, h_ctx = ssd_chunk_scan(flip(xs_c, rev), flip(dtd_c, rev), a_head,
                                     flip(bm_c, rev), flip(cm_c, rev), h0)
        yd_l, _ = ssd_chunk_scan(flip(xs_l, rev), flip(dtd_l, rev), a_head,
                                 flip(bm_l, rev), flip(cm_l, rev), h_ctx)
        y_l = y_l + flip(yd_l, rev)
        y_c = y_c + flip(yd_c, rev)
    ssd_out_l = rms_norm(y_l.reshape(bsz, n_lat, SSD_INNER) * jax.nn.silu(z_l.astype(f32)),
                         ssd_norm_g).astype(dtype)

    u_l = to_col_major(s5_l).astype(f32)
    u_c = s5_c.astype(f32)
    ug_l = u_l.reshape(bsz, n_lat, S5_GROUPS, S5_GROUP)
    ug_c = u_c.reshape(bsz, n_ctx, S5_GROUPS, S5_GROUP)
    s5_skip = s5_d.astype(f32)
    ys_l = s5_skip * u_l
    ys_c = s5_skip * u_c
    hs0 = jnp.zeros((bsz, S5_GROUPS, S5_STATE), jnp.complex64)
    for d in range(2):
        rev = d == 1
        lam_bar, b_bar, c_mat = s5_discretize(s5_lam_re[d], s5_lam_im[d], s5_log_step[d],
                                              s5_b_re[d], s5_b_im[d], s5_c_re[d], s5_c_im[d])
        yd_c, hs_ctx = s5_scan(flip(ug_c, rev), lam_bar, b_bar, c_mat, hs0)
        yd_l, _ = s5_scan(flip(ug_l, rev), lam_bar, b_bar, c_mat, hs_ctx)
        ys_l = ys_l + flip(yd_l, rev).reshape(bsz, n_lat, S5_DIM)
        ys_c = ys_c + flip(yd_c, rev).reshape(bsz, n_ctx, S5_DIM)
    s5_out_l = s5_glu(from_col_major(ys_l).astype(dtype), s5_glu_w, s5_glu_b, s5_out_g)

    conf_out_l = conformer_conv(conf_l, conf_dw_w, conf_dw_b, conf_ln_g, conf_ln_b,
                                conf_pw_w, conf_pw_b, conf_out_g)

    out_l = jnp.concatenate([ssd_out_l, conf_out_l, s5_out_l], axis=-1) @ w_out
    if not with_ctx_out:
        return out_l, None
    ssd_out_c = rms_norm(y_c.reshape(bsz, n_ctx, SSD_INNER) * jax.nn.silu(z_c.astype(f32)),
                         ssd_norm_g).astype(dtype)
    s5_out_c = s5_glu(ys_c.astype(dtype), s5_glu_w, s5_glu_b, s5_out_g)
    conf_out_c = conformer_conv(conf_c, conf_dw_w, conf_dw_b, conf_ln_g, conf_ln_b,
                                conf_pw_w, conf_pw_b, conf_out_g)
    out_c = jnp.concatenate([ssd_out_c, conf_out_c, s5_out_c], axis=-1) @ w_out
    return out_l, out_c


def moe_ffn(h, router_w, router_b, w_gu, b_gu, w_down, b_down):
    t, dm = h.shape
    logits = (h @ router_w + router_b).astype(jnp.float32)
    top_logit, top_idx = lax.top_k(logits, TOP_K)
    gates = jax.nn.softmax(top_logit, axis=-1)
    n_assign = t * TOP_K
    flat_e = top_idx.reshape(-1)
    order = jnp.argsort(flat_e)
    sorted_e = flat_e[order]
    sorted_tok = (order // TOP_K).astype(jnp.int32)
    sorted_gate = gates.reshape(-1)[order]
    counts = jnp.bincount(flat_e, length=N_EXPERTS)
    padded = (counts + MOE_BLOCK - 1) // MOE_BLOCK * MOE_BLOCK
    pad_end = jnp.cumsum(padded)
    pad_start = pad_end - padded
    start = jnp.cumsum(counts) - counts
    dest = pad_start[sorted_e] + jnp.arange(n_assign) - start[sorted_e]
    n_blocks = n_assign // MOE_BLOCK + N_EXPERTS
    n_rows = n_blocks * MOE_BLOCK
    row_tok = jnp.zeros((n_rows,), jnp.int32).at[dest].set(sorted_tok)
    row_gate = jnp.zeros((n_rows,), jnp.float32).at[dest].set(sorted_gate)
    block_e = jnp.minimum(jnp.searchsorted(pad_end, jnp.arange(n_blocks) * MOE_BLOCK, side='right'),
                          N_EXPERTS - 1)
    xb = h[row_tok].reshape(n_blocks, MOE_BLOCK, dm)

    def expert_block(args):
        xe, e = args
        gu = xe @ w_gu[e] + b_gu[e]
        gate, lin = gu[:, :D_EXPERT], gu[:, D_EXPERT:]
        gate = jnp.minimum(gate, SWIGLU_LIMIT)
        lin = jnp.clip(lin, -SWIGLU_LIMIT, SWIGLU_LIMIT)
        act = gate * jax.nn.sigmoid(SWIGLU_ALPHA * gate) * (lin + 1.0)
        return act @ w_down[e] + b_down[e]

    yb = lax.map(expert_block, (xb, block_e))
    y = jax.ops.segment_sum(yb.reshape(n_rows, dm).astype(jnp.float32) * row_gate[:, None],
                            row_tok, num_segments=t)
    return y.astype(h.dtype)


def setup_inputs(seed: int = 0) -> dict:
    key = jax.random.key(seed)
    split = jax.random.split(key, 64)
    key_iter = iter([split[i] for i in range(64)])
    f32 = jnp.float32

    def normal(shape, scale):
        return jax.random.normal(next(key_iter), shape, f32) * scale

    def uniform(shape, lo, hi):
        return jax.random.uniform(next(key_iter), shape, f32, lo, hi)

    def gain(shape):
        return 1.0 + normal(shape, 0.01)

    D = D_MODEL
    dt0 = jnp.exp(uniform((DEPTH, 2, SSD_HEADS), math.log(1e-3), math.log(1e-1)))
    s5_shape = (DEPTH, 2, S5_GROUPS, S5_STATE)
    return {
        'x': normal((BATCH, SEQ, D), 1.0),
        'c': normal((BATCH, D), 1.0),
        'ctx': normal((BATCH, CTX_LEN, D), 1.0),
        'c_ctx': normal((D,), 1.0),
        'ada_w': normal((DEPTH, D, 6 * D), 0.5 * D ** -0.5),
        'ada_b': normal((DEPTH, 6 * D), 0.01),
        'norm1_g': gain((DEPTH, D)),
        'w_in': normal((DEPTH, D, IN_PROJ_DIM), D ** -0.5),
        'ssd_conv_w': normal((DEPTH, SSD_CONV, SSD_CONV_DIM), SSD_CONV ** -0.5),
        'ssd_conv_b': normal((DEPTH, SSD_CONV_DIM), 0.01),
        'ssd_a_log': jnp.log(uniform((DEPTH, 2, SSD_HEADS), 1.0, 16.0)),
        'ssd_dt_bias': dt0 + jnp.log(-jnp.expm1(-dt0)),
        'ssd_d': gain((DEPTH, SSD_HEADS)),
        'ssd_norm_g': gain((DEPTH, SSD_INNER)),
        'conf_dw_w': normal((DEPTH, CONF_KERNEL, CONF_DIM), CONF_KERNEL ** -0.5),
        'conf_dw_b': normal((DEPTH, CONF_DIM), 0.01),
        'conf_ln_g': gain((DEPTH, CONF_DIM)),
        'conf_ln_b': normal((DEPTH, CONF_DIM), 0.01),
        'conf_pw_w': normal((DEPTH, CONF_DIM, CONF_DIM), CONF_DIM ** -0.5),
        'conf_pw_b': normal((DEPTH, CONF_DIM), 0.01),
        'conf_out_g': gain((DEPTH, CONF_DIM)),
        's5_lam_re': -0.5 + normal(s5_shape, 0.01),
        's5_lam_im': math.pi * jnp.arange(S5_STATE, dtype=f32) + normal(s5_shape, 0.01),
        's5_log_step': uniform((DEPTH, 2, S5_GROUPS), math.log(1e-3), math.log(1e-1)),
        's5_b_re': normal((DEPTH, 2, S5_GROUPS, S5_STATE, S5_GROUP), (2 * S5_GROUP) ** -0.5),
        's5_b_im': normal((DEPTH, 2, S5_GROUPS, S5_STATE, S5_GROUP), (2 * S5_GROUP) ** -0.5),
        's5_c_re': normal((DEPTH, 2, S5_GROUPS, S5_GROUP, S5_STATE), (2 * S5_STATE) ** -0.5),
        's5_c_im': normal((DEPTH, 2, S5_GROUPS, S5_GROUP, S5_STATE), (2 * S5_STATE) ** -0.5),
        's5_d': normal((DEPTH, S5_DIM), 1.0),
        's5_glu_w': normal((DEPTH, S5_DIM, S5_DIM), S5_DIM ** -0.5),
        's5_glu_b': normal((DEPTH, S5_DIM), 0.01),
        's5_out_g': gain((DEPTH, S5_DIM)),
        'w_out': normal((DEPTH, MIX_WIDTH, D), MIX_WIDTH ** -0.5),
        'norm2_g': gain((DEPTH, D)),
        'router_w': normal((DEPTH, D, N_EXPERTS), D ** -0.5),
        'router_b': normal((DEPTH, N_EXPERTS), 0.01),
        'w_gate_up': normal((DEPTH, N_EXPERTS, D, 2 * D_EXPERT), D ** -0.5),
        'b_gate_up': normal((DEPTH, N_EXPERTS, 2 * D_EXPERT), 0.01),
        'w_down': normal((DEPTH, N_EXPERTS, D_EXPERT, D), D_EXPERT ** -0.5),
        'b_down': normal((DEPTH, N_EXPERTS, D), 0.01),
        'final_norm_g': gain((D,)),
    }


def reference(x, c, ctx, c_ctx, ada_w, ada_b, norm1_g, w_in, ssd_conv_w, ssd_conv_b, ssd_a_log,
              ssd_dt_bias, ssd_d, ssd_norm_g, conf_dw_w, conf_dw_b, conf_ln_g, conf_ln_b, conf_pw_w,
              conf_pw_b, conf_out_g, s5_lam_re, s5_lam_im, s5_log_step, s5_b_re, s5_b_im, s5_c_re,
              s5_c_im, s5_d, s5_glu_w, s5_glu_b, s5_out_g, w_out, norm2_g, router_w, router_b,
              w_gate_up, b_gate_up, w_down, b_down, final_norm_g):
    bsz, n_lat, dm = x.shape
    n_tok = bsz * n_lat
    for l in range(DEPTH):
        last = l == DEPTH - 1
        mod_l = jax.nn.silu(c) @ ada_w[l] + ada_b[l]
        mod_c = jax.nn.silu(c_ctx) @ ada_w[l] + ada_b[l]
        sh1, sc1, g1, sh2, sc2, g2 = jnp.split(mod_l[:, None, :], 6, axis=-1)
        csh1, csc1, cg1, csh2, csc2, cg2 = jnp.split(mod_c, 6, axis=-1)

        hl = modulate(rms_norm(x, norm1_g[l]), sh1, sc1)
        hc = modulate(rms_norm(ctx, norm1_g[l]), csh1, csc1)
        mix_l, mix_c = hybrid_mixer(hl, hc, w_in[l], ssd_conv_w[l], ssd_conv_b[l], ssd_a_log[l],
                                    ssd_dt_bias[l], ssd_d[l], ssd_norm_g[l], conf_dw_w[l], conf_dw_b[l],
                                    conf_ln_g[l], conf_ln_b[l], conf_pw_w[l], conf_pw_b[l], conf_out_g[l],
                                    s5_lam_re[l], s5_lam_im[l], s5_log_step[l], s5_b_re[l], s5_b_im[l],
                                    s5_c_re[l], s5_c_im[l], s5_d[l], s5_glu_w[l], s5_glu_b[l],
                                    s5_out_g[l], w_out[l], not last)
        x = x + g1 * mix_l
        hl = modulate(rms_norm(x, norm2_g[l]), sh2, sc2).reshape(n_tok, dm)
        if last:
            ffn = moe_ffn(hl, router_w[l], router_b[l], w_gate_up[l], b_gate_up[l], w_down[l], b_down[l])
            x = x + g2 * ffn.reshape(bsz, n_lat, dm)
        else:
            ctx = ctx + cg1 * mix_c
            hc = modulate(rms_norm(ctx, norm2_g[l]), csh2, csc2).reshape(-1, dm)
            ffn = moe_ffn(jnp.concatenate([hl, hc], axis=0), router_w[l], router_b[l],
                          w_gate_up[l], b_gate_up[l], w_down[l], b_down[l])
            x = x + g2 * ffn[:n_tok].reshape(bsz, n_lat, dm)
            ctx = ctx + cg2 * ffn[n_tok:].reshape(ctx.shape)
    return rms_norm(x, final_norm_g)
```

```python
import functools
import math

import jax
import jax.numpy as jnp
from jax import lax
from jax.experimental import pallas as pl
from jax.experimental.pallas import tpu as pltpu

F32 = jnp.float32
BF16 = jnp.bfloat16
I32 = jnp.int32
U32 = jnp.uint32

D_MODEL = 2048
DEPTH = 4
GRID_W = 64
SSD_INNER = 1024
SSD_HEAD_DIM = 64
SSD_HEADS = 16
SSD_GROUPS = 4
SSD_STATE = 128
SSD_CONV = 5
SSD_CHUNK = 128
SSD_CONV_DIM = SSD_INNER + 2 * SSD_GROUPS * SSD_STATE
CONF_DIM = 512
CONF_KERNEL = 31
S5_DIM = 512
S5_GROUP = 16
S5_GROUPS = 32
S5_STATE = 64
S5_NSTATE = S5_GROUPS * S5_STATE
N_EXPERTS = 32
TOP_K = 4
D_EXPERT = 768
SWIGLU_LIMIT = 7.0
SWIGLU_ALPHA = 1.702
MOE_BLOCK = 256
EPS = 1e-6

LANES = 128
SUBLANES = 8
ROW_TILE = 256
MM_TILE_M = 512
MM_TILE_N = 512
MAIN_COLS = 4096
COL_Z, COL_X, COL_CONF = 0, 1024, 3072
CONV_HALO = 8
CONF_HALO = 16
S5_SEG = 32
S5_WIN = SUBLANES * S5_SEG
VMEM_LIMIT = 56 * 1024 * 1024


def _cparams(sem, vmem=None):
    return pltpu.CompilerParams(dimension_semantics=sem, vmem_limit_bytes=vmem)


def _silu(v):
    return v * jax.nn.sigmoid(v)


def _split3(v):
    hi = v.astype(BF16)
    r1 = v - hi.astype(F32)
    mid = r1.astype(BF16)
    lo = (r1 - mid.astype(F32)).astype(BF16)
    return hi, mid, lo


def _dot(a, b):
    return jnp.dot(a, b, preferred_element_type=F32)


def _mod_kernel(c_ref, w_ref, b_ref, o_ref):
    s = _silu(c_ref[...])
    w = w_ref[...]
    s_hi = s.astype(BF16)
    s_lo = (s - s_hi.astype(F32)).astype(BF16)
    w_hi = w.astype(BF16)
    w_lo = (w - w_hi.astype(F32)).astype(BF16)
    acc = _dot(s_hi, w_hi) + _dot(s_lo, w_hi) + _dot(s_hi, w_lo)
    o_ref[...] = acc + b_ref[...]


def _modulation(cond, ada_w, ada_b):
    depth, d, n = ada_w.shape
    tn = 1024
    return pl.pallas_call(
        _mod_kernel,
        out_shape=jax.ShapeDtypeStruct((depth, SUBLANES, n), F32),
        grid=(depth, n // tn),
        in_specs=[pl.BlockSpec((SUBLANES, d), lambda l, j: (0, 0)),
                  pl.BlockSpec((None, d, tn), lambda l, j: (l, 0, j)),
                  pl.BlockSpec((None, 1, tn), lambda l, j: (l, 0, j))],
        out_specs=pl.BlockSpec((None, SUBLANES, tn), lambda l, j: (l, 0, j)),
        compiler_params=_cparams(("arbitrary", "arbitrary"), VMEM_LIMIT),
        name="adaln_mod",
    )(cond, ada_w, ada_b.reshape(depth, 1, n))


def _normmm_kernel(x_ref, g_ref, sh_ref, sc_ref, w_ref, *rest):
    o_refs, hn_ref = rest[:-1], rest[-1]

    @pl.when(pl.program_id(1) == 0)
    def _():
        x = x_ref[...]
        ms = jnp.mean(x * x, axis=-1, keepdims=True)
        y = x * lax.rsqrt(ms + EPS) * g_ref[...]
        hn_ref[...] = (y * (1.0 + sc_ref[...]) + sh_ref[...]).astype(BF16)

    res = _dot(hn_ref[...], w_ref[...])
    col = 0
    for o_ref in o_refs:
        o_ref[...] = res[:, col:col + o_ref.shape[1]].astype(o_ref.dtype)
        col += o_ref.shape[1]


def _norm_matmul(r, gain, mod3, shift_idx, scale_idx, w, lat_len, n_batch, name, splits=None):
    nt, d = r.shape
    n = w.shape[1]
    tm = MM_TILE_M if nt % MM_TILE_M == 0 else ROW_TILE
    tn = n if splits else MM_TILE_N
    assert n % tn == 0
    widths = splits or (tn,)

    def grp(i):
        return jnp.minimum((i * tm) // lat_len, n_batch)

    outs = pl.pallas_call(
        _normmm_kernel,
        out_shape=tuple(jax.ShapeDtypeStruct((nt, n if not splits else wd), F32) for wd in widths),
        grid=(nt // tm, n // tn),
        in_specs=[pl.BlockSpec((tm, d), lambda i, j: (i, 0)),
                  pl.BlockSpec((1, d), lambda i, j: (0, 0)),
                  pl.BlockSpec((None, 1, d), lambda i, j: (grp(i), 0, shift_idx)),
                  pl.BlockSpec((None, 1, d), lambda i, j: (grp(i), 0, scale_idx)),
                  pl.BlockSpec((d, tn), lambda i, j: (0, j))],
        out_specs=tuple(pl.BlockSpec((tm, wd), lambda i, j: (i, j)) for wd in widths),
        scratch_shapes=[pltpu.VMEM((tm, d), BF16)],
        compiler_params=_cparams(("arbitrary", "arbitrary"), VMEM_LIMIT),
        name=name,
    )(r, gain.reshape(1, d), mod3, mod3, w)
    return outs if splits else outs[0]


def _seq_edges(i, tiles_per_lat_seq, n_lat_tiles):
    is_lat = i < n_lat_tiles
    first = jnp.logical_or(jnp.logical_not(is_lat), (i % tiles_per_lat_seq) == 0)
    last = jnp.logical_or(jnp.logical_not(is_lat), (i % tiles_per_lat_seq) == tiles_per_lat_seq - 1)
    return first, last


def _conv5_kernel(cur_ref, prev_ref, next_ref, w_ref, b_ref, o_ref, ext_ref, *, tiles_per_lat_seq,
                  n_lat_tiles):
    i = pl.program_id(0)
    first, last = _seq_edges(i, tiles_per_lat_seq, n_lat_tiles)
    h, tm = CONV_HALO, ROW_TILE
    ext_ref[0:h, :] = jnp.where(first, 0.0, prev_ref[...])
    ext_ref[h:h + tm, :] = cur_ref[...]
    ext_ref[h + tm:h + tm + h, :] = jnp.where(last, 0.0, next_ref[...])
    pad = (SSD_CONV - 1) // 2
    cw = 512
    for c in range(0, ext_ref.shape[1], cw):
        acc = jnp.broadcast_to(b_ref[:, c:c + cw], (tm, cw))
        for j in range(SSD_CONV):
            acc = acc + w_ref[j:j + 1, c:c + cw] * ext_ref[h - pad + j:h - pad + j + tm, c:c + cw]
        o_ref[:, c:c + cw] = _silu(acc).astype(BF16)


def _ssd_conv(p_main, conv_w, conv_b, lat_len, ctx_len, n_batch):
    nt = p_main.shape[0]
    tm, h = ROW_TILE, CONV_HALO
    assert ctx_len == tm and lat_len % tm == 0
    n_lat_tiles = n_batch * lat_len // tm
    cw = SSD_CONV_DIM // 2
    xblk = COL_X // cw
    nhb = nt // h
    kern = functools.partial(_conv5_kernel, tiles_per_lat_seq=lat_len // tm, n_lat_tiles=n_lat_tiles)
    return pl.pallas_call(
        kern,
        out_shape=jax.ShapeDtypeStruct((nt, SSD_CONV_DIM), BF16),
        grid=(nt // tm, 2),
        in_specs=[pl.BlockSpec((tm, cw), lambda i, j: (i, xblk + j)),
                  pl.BlockSpec((h, cw), lambda i, j: (jnp.maximum(i * (tm // h) - 1, 0), xblk + j)),
                  pl.BlockSpec((h, cw), lambda i, j: (jnp.minimum((i + 1) * (tm // h), nhb - 1), xblk + j)),
                  pl.BlockSpec((SSD_CONV, cw), lambda i, j: (0, j)),
                  pl.BlockSpec((1, cw), lambda i, j: (0, j))],
        out_specs=pl.BlockSpec((tm, cw), lambda i, j: (i, j)),
        scratch_shapes=[pltpu.VMEM((tm + 2 * h, cw), F32)],
        compiler_params=_cparams(("arbitrary", "arbitrary"), VMEM_LIMIT),
        name="ssd_conv",
    )(p_main, p_main, p_main, conv_w, conv_b.reshape(1, -1))


def _head_cols(vals, width):
    lane = lax.broadcasted_iota(I32, (1, LANES), 1)
    halves = []
    per_half = LANES // width
    for hh in range(len(vals) // per_half):
        sel = vals[hh * per_half + per_half - 1]
        for k in range(per_half - 2, -1, -1):
            sel = jnp.where(lane < (k + 1) * width, vals[hh * per_half + k], sel)
        halves.append(jnp.broadcast_to(sel, (sel.shape[0], LANES)) if sel.shape[1] == 1 else sel)
    return jnp.concatenate(halves, axis=1)


def _ssd_kernel(*refs, reverse, final):
    if final:
        (x_ref, b_ref, c_ref, dt_ref, dtb_ref, alog_ref, yprev_ref, z_ref, dskip_ref, ng_ref,
         o_ref, state_ref) = refs
    else:
        x_ref, b_ref, c_ref, dt_ref, dtb_ref, alog_ref, o_ref, state_ref = refs
    q = SSD_CHUNK
    r = SSD_HEADS // SSD_GROUPS
    gw = r * SSD_HEAD_DIM

    @pl.when(pl.program_id(1) == 0)
    def _():
        state_ref[...] = jnp.zeros_like(state_ref)

    lane = lax.broadcasted_iota(I32, (1, LANES), 1)
    dtp = jax.nn.softplus(dt_ref[...] + dtb_ref[...])
    a_head = -jnp.exp(alog_ref[...])
    a = jnp.where(lane < SSD_HEADS, dtp * a_head, 0.0)
    ri = lax.broadcasted_iota(I32, (q, q), 0)
    ci = lax.broadcasted_iota(I32, (q, q), 1)
    tri = (ci >= ri) if reverse else (ci <= ri)
    tri_b = jnp.where(tri, 1.0, 0.0).astype(BF16)
    a_hi, a_mid, a_lo = _split3(a)
    a_cs = _dot(tri_b, a_hi) + _dot(tri_b, a_mid) + _dot(tri_b, a_lo)
    a_cs_t = a_cs.T
    dtp_t = dtp.T
    a_end = a_cs[0:1, :] if reverse else a_cs[q - 1:q, :]
    lane_g = lax.broadcasted_iota(I32, (1, gw), 1)

    x = x_ref[...]
    for g in range(SSD_GROUPS):
        cg = c_ref[:, g * SSD_STATE:(g + 1) * SSD_STATE]
        bg = b_ref[:, g * SSD_STATE:(g + 1) * SSD_STATE]
        cb = lax.dot_general(cg, bg, (((1,), (1,)), ((), ())), preferred_element_type=F32)
        xg = x[:, g * gw:(g + 1) * gw]
        yg = jnp.zeros((q, gw), F32)
        ecols, wcols, dec = [], [], []
        for hl in range(r):
            h = g * r + hl
            col = a_cs[:, h:h + 1]
            row = a_cs_t[h:h + 1, :]
            lm = jnp.where(tri, jnp.exp(col - row), 0.0)
            m = (cb * lm * dtp_t[h:h + 1, :]).astype(BF16)
            in_head = jnp.logical_and(lane_g >= hl * SSD_HEAD_DIM, lane_g < (hl + 1) * SSD_HEAD_DIM)
            xm = jnp.where(in_head, xg, jnp.zeros_like(xg))
            yg = yg + _dot(m, xm)
            e_h = a_end[:, h:h + 1]
            ecols.append(jnp.exp(col))
            wcols.append(dtp[:, h:h + 1] * jnp.exp(e_h - col))
            dec.append(jnp.exp(e_h))
        s_old = state_ref[g]
        yg = yg + _head_cols(ecols, SSD_HEAD_DIM) * _dot(cg, s_old.astype(BF16))
        xw = (xg.astype(F32) * _head_cols(wcols, SSD_HEAD_DIM)).astype(BF16)
        upd = lax.dot_general(bg, xw, (((0,), (0,)), ((), ())), preferred_element_type=F32)
        state_ref[g] = _head_cols(dec, SSD_HEAD_DIM) * s_old + upd
        if final:
            sl = slice(g * gw, (g + 1) * gw)
            ytot = yprev_ref[:, sl] + yg + dskip_ref[:, sl] * xg.astype(F32)
            o_ref[:, sl] = ytot * _silu(z_ref[:, sl])
        else:
            o_ref[:, g * gw:(g + 1) * gw] = yg
    if final:
        gated = o_ref[...]
        ms = jnp.mean(gated * gated, axis=-1, keepdims=True)
        o_ref[...] = gated * lax.rsqrt(ms + EPS) * ng_ref[...]


def _ssd_scan(xbc, p_main, p_dt, dt_bias, a_log, lat_len, ctx_len, n_batch, reverse, final_args=None):
    nt = xbc.shape[0]
    q = SSD_CHUNK
    ncl, ncc = lat_len // q, ctx_len // q
    ctx0 = n_batch * ncl

    def blk(b, j):
        if reverse:
            return jnp.where(j < ncc, ctx0 + b * ncc + (ncc - 1 - j), b * ncl + (ncl - 1 - (j - ncc)))
        return jnp.where(j < ncc, ctx0 + b * ncc + j, b * ncl + (j - ncc))

    pad = LANES - SSD_HEADS
    dtb = jnp.pad(dt_bias.astype(F32), (0, pad)).reshape(1, LANES)
    alog = jnp.pad(a_log.astype(F32), (0, pad)).reshape(1, LANES)
    final = final_args is not None
    in_specs = [pl.BlockSpec((q, SSD_INNER), lambda b, j: (blk(b, j), 0)),
                pl.BlockSpec((q, SSD_GROUPS * SSD_STATE), lambda b, j: (blk(b, j), 2)),
                pl.BlockSpec((q, SSD_GROUPS * SSD_STATE), lambda b, j: (blk(b, j), 3)),
                pl.BlockSpec((q, LANES), lambda b, j: (blk(b, j), 0)),
                pl.BlockSpec((1, LANES), lambda b, j: (0, 0)),
                pl.BlockSpec((1, LANES), lambda b, j: (0, 0))]
    args = [xbc, xbc, xbc, p_dt, dtb, alog]
    if final:
        y_prev, d_skip, norm_g = final_args
        in_specs += [pl.BlockSpec((q, SSD_INNER), lambda b, j: (blk(b, j), 0)),
                     pl.BlockSpec((q, SSD_INNER), lambda b, j: (blk(b, j), COL_Z // SSD_INNER)),
                     pl.BlockSpec((1, SSD_INNER), lambda b, j: (0, 0)),
                     pl.BlockSpec((1, SSD_INNER), lambda b, j: (0, 0))]
        args += [y_prev, p_main, d_skip.reshape(1, -1), norm_g.reshape(1, -1)]
    return pl.pallas_call(
        functools.partial(_ssd_kernel, reverse=reverse, final=final),
        out_shape=jax.ShapeDtypeStruct((nt, SSD_INNER), F32),
        grid=(n_batch, ncc + ncl),
        in_specs=in_specs,
        out_specs=pl.BlockSpec((q, SSD_INNER), lambda b, j: (blk(b, j), 0)),
        scratch_shapes=[pltpu.VMEM((SSD_GROUPS, SSD_STATE, (SSD_HEADS // SSD_GROUPS) * SSD_HEAD_DIM), F32)],
        compiler_params=_cparams(("arbitrary", "arbitrary"), VMEM_LIMIT),
        name="ssd_scan_rev" if reverse else "ssd_scan_fwd",
    )(*args)


def _glu(ref):
    v = ref[:, 0:CONF_DIM]
    gt = ref[:, CONF_DIM:2 * CONF_DIM]
    return v * jax.nn.sigmoid(gt)


def _conf_kernel(cur_ref, prev_ref, next_ref, dww_ref, dwb_ref, lng_ref, lnb_ref, pww_ref, pwb_ref,
                 og_ref, o_ref, ext_ref, *, tiles_per_lat_seq, n_lat_tiles):
    i = pl.program_id(0)
    first, last = _seq_edges(i, tiles_per_lat_seq, n_lat_tiles)
    h, tm = CONF_HALO, ROW_TILE
    ext_ref[0:h, :] = jnp.where(first, 0.0, _glu(prev_ref))
    ext_ref[h:h + tm, :] = _glu(cur_ref)
    ext_ref[h + tm:h + tm + h, :] = jnp.where(last, 0.0, _glu(next_ref))
    pad = (CONF_KERNEL - 1) // 2
    cw = 256
    for c in range(0, CONF_DIM, cw):
        acc = jnp.broadcast_to(dwb_ref[:, c:c + cw], (tm, cw))
        for j in range(CONF_KERNEL):
            acc = acc + dww_ref[j:j + 1, c:c + cw] * ext_ref[h - pad + j:h - pad + j + tm, c:c + cw]
        o_ref[:, c:c + cw] = acc
    u = o_ref[...]
    mu = jnp.mean(u, axis=-1, keepdims=True)
    var = jnp.mean(jnp.square(u - mu), axis=-1, keepdims=True)
    y = (u - mu) * lax.rsqrt(var + EPS) * lng_ref[...] + lnb_ref[...]
    y = _silu(y)
    v = _dot(y.astype(BF16), pww_ref[...]) + pwb_ref[...]
    ms = jnp.mean(v * v, axis=-1, keepdims=True)
    o_ref[...] = v * lax.rsqrt(ms + EPS) * og_ref[...]


def _conformer(p_main, dw_w, dw_b, ln_g, ln_b, pw_w, pw_b, out_g, lat_len, ctx_len, n_batch):
    nt = p_main.shape[0]
    tm, h = ROW_TILE, CONF_HALO
    n_lat_tiles = n_batch * lat_len // tm
    cblk = COL_CONF // (2 * CONF_DIM)
    nhb = nt // h
    vec = lambda n: pl.BlockSpec((1, n), lambda i: (0, 0))
    kern = functools.partial(_conf_kernel, tiles_per_lat_seq=lat_len // tm, n_lat_tiles=n_lat_tiles)
    return pl.pallas_call(
        kern,
        out_shape=jax.ShapeDtypeStruct((nt, CONF_DIM), F32),
        grid=(nt // tm,),
        in_specs=[pl.BlockSpec((tm, 2 * CONF_DIM), lambda i: (i, cblk)),
                  pl.BlockSpec((h, 2 * CONF_DIM), lambda i: (jnp.maximum(i * (tm // h) - 1, 0), cblk)),
                  pl.BlockSpec((h, 2 * CONF_DIM), lambda i: (jnp.minimum((i + 1) * (tm // h), nhb - 1), cblk)),
                  pl.BlockSpec((CONF_KERNEL, CONF_DIM), lambda i: (0, 0)),
                  vec(CONF_DIM), vec(CONF_DIM), vec(CONF_DIM),
                  pl.BlockSpec((CONF_DIM, CONF_DIM), lambda i: (0, 0)),
                  vec(CONF_DIM), vec(CONF_DIM)],
        out_specs=pl.BlockSpec((tm, CONF_DIM), lambda i: (i, 0)),
        scratch_shapes=[pltpu.VMEM((tm + 2 * h, CONF_DIM), F32)],
        compiler_params=_cparams(("arbitrary",), VMEM_LIMIT),
        name="conformer",
    )(p_main, p_main, p_main, dw_w, dw_b.reshape(1, -1), ln_g.reshape(1, -1), ln_b.reshape(1, -1),
      pw_w.astype(BF16), pw_b.reshape(1, -1), out_g.reshape(1, -1))


def _gelu_tanh(v):
    return 0.5 * v * (1.0 + jnp.tanh(math.sqrt(2.0 / math.pi) * (v + 0.044715 * (v * v * v))))


def _s5_kernel(*refs, reverse, final, colmajor):
    if final:
        (u_ref, yprev_ref, bblk_ref, cblk_ref, lam_ref, sin_ref, dskip_ref, gw_ref, gb_ref, og_ref,
         o_ref, sout_ref, lhs_ref, h_ref, fin_ref, init_ref, carry_ref, y_ref, tot_ref) = refs
    else:
        (u_ref, bblk_ref, cblk_ref, lam_ref, sin_ref,
         o_ref, sout_ref, lhs_ref, h_ref, fin_ref, init_ref, carry_ref, y_ref) = refs
    ns = S5_NSTATE
    seg, nsub = S5_SEG, SUBLANES
    jw = pl.program_id(1)

    @pl.when(jw == 0)
    def _():
        carry_ref[...] = sin_ref[...]

    def sub_block(s):
        if colmajor:
            half = GRID_W // seg
            return (slice((s % half) * seg, (s % half + 1) * seg),
                    slice((s // half) * S5_DIM, (s // half + 1) * S5_DIM))
        return slice(s * seg, (s + 1) * seg), slice(0, S5_DIM)

    n_lb = S5_DIM // LANES

    def lane_block(cs, k):
        return slice(cs.start + k * LANES, cs.start + (k + 1) * LANES)

    def gathered(ref):
        return jnp.concatenate([ref[k] for k in range(n_lb)], axis=1)

    for s in range(nsub):
        rs, cs = sub_block(s)
        for k in range(n_lb):
            lhs_ref[k, pl.ds(s, seg, stride=nsub), :] = u_ref[rs, lane_block(cs, k)]
            if final:
                tot_ref[k, pl.ds(s, seg, stride=nsub), :] = yprev_ref[rs, lane_block(cs, k)]

    u_win = gathered(lhs_ref)
    h_ref[...] = _dot(u_win.astype(BF16), bblk_ref[...])

    cw = 512
    n_chunks = ns // cw

    def lam_chunk(row, c):
        return (jnp.broadcast_to(lam_ref[row:row + 1, c * cw:(c + 1) * cw], (nsub, cw)),
                jnp.broadcast_to(lam_ref[row + 1:row + 2, c * cw:(c + 1) * cw], (nsub, cw)))

    def row0(i):
        step = (seg - 1 - i) if reverse else i
        return pl.multiple_of(step * nsub, nsub)

    for c in range(n_chunks):
        lre, lim = lam_chunk(0, c)
        cre = slice(c * cw, (c + 1) * cw)
        cim = slice(ns + c * cw, ns + (c + 1) * cw)

        def step1(i, hc, cre=cre, cim=cim, lre=lre, lim=lim):
            hre, him = hc
            r0 = row0(i)
            nre = lre * hre - lim * him + h_ref[pl.ds(r0, nsub), cre]
            nim = lre * him + lim * hre + h_ref[pl.ds(r0, nsub), cim]
            h_ref[pl.ds(r0, nsub), cre] = nre
            h_ref[pl.ds(r0, nsub), cim] = nim
            return nre, nim

        z0 = jnp.zeros((nsub, cw), F32)
        fre, fim = lax.fori_loop(0, seg, step1, (z0, z0), unroll=4)
        fin_ref[:, cre] = fre
        fin_ref[:, cim] = fim

    gre, gim = lam_ref[2:3, :], lam_ref[3:4, :]
    cur_re, cur_im = carry_ref[:, 0:ns], carry_ref[:, ns:2 * ns]
    order = range(nsub - 1, -1, -1) if reverse else range(nsub)
    for s in order:
        init_ref[s:s + 1, 0:ns] = cur_re
        init_ref[s:s + 1, ns:2 * ns] = cur_im
        f_re, f_im = fin_ref[s:s + 1, 0:ns], fin_ref[s:s + 1, ns:2 * ns]
        cur_re, cur_im = gre * cur_re - gim * cur_im + f_re, gre * cur_im + gim * cur_re + f_im
    carry_ref[:, 0:ns] = cur_re
    carry_ref[:, ns:2 * ns] = cur_im

    for c in range(n_chunks):
        lre, lim = lam_chunk(0, c)
        cre = slice(c * cw, (c + 1) * cw)
        cim = slice(ns + c * cw, ns + (c + 1) * cw)

        def step2(i, gc, cre=cre, cim=cim, lre=lre, lim=lim):
            g_re, g_im = gc
            n_re = lre * g_re - lim * g_im
            n_im = lre * g_im + lim * g_re
            r0 = row0(i)
            h_ref[pl.ds(r0, nsub), cre] = h_ref[pl.ds(r0, nsub), cre] + n_re
            h_ref[pl.ds(r0, nsub), cim] = h_ref[pl.ds(r0, nsub), cim] + n_im
            return n_re, n_im

        lax.fori_loop(0, seg, step2, (init_ref[:, cre], init_ref[:, cim]), unroll=4)

    y_win = _dot(h_ref[...].astype(BF16), cblk_ref[...])

    if final:
        tot = gathered(tot_ref) + y_win + dskip_ref[...] * gathered(lhs_ref)
        gl = _gelu_tanh(tot)
        gate = jax.nn.sigmoid(_dot(gl.astype(BF16), gw_ref[...]) + gb_ref[...])
        v = gl * gate
        ms = jnp.mean(v * v, axis=-1, keepdims=True)
        y_win = v * lax.rsqrt(ms + EPS) * og_ref[...]

    for k in range(n_lb):
        y_ref[k] = y_win[:, k * LANES:(k + 1) * LANES]
    for s in range(nsub):
        rs, cs = sub_block(s)
        for k in range(n_lb):
            o_ref[rs, lane_block(cs, k)] = y_ref[k, pl.ds(s, seg, stride=nsub), :].astype(o_ref.dtype)

    @pl.when(jw == pl.num_programs(1) - 1)
    def _():
        sout_ref[...] = carry_ref[...]


def _s5_scan(u, bblk, cblk, lam, state_in, reverse, colmajor, final_args=None):
    n_batch = u.shape[0]
    ns2 = 2 * S5_NSTATE
    if colmajor:
        tile = (GRID_W, (S5_WIN // GRID_W) * S5_DIM)
        n_win = u.shape[2] // tile[1]
        imap = (lambda b, j: (b, 0, n_win - 1 - j)) if reverse else (lambda b, j: (b, 0, j))
    else:
        tile = (S5_WIN, S5_DIM)
        n_win = u.shape[1] // S5_WIN
        imap = (lambda b, j: (b, n_win - 1 - j, 0)) if reverse else (lambda b, j: (b, j, 0))
    tok = pl.BlockSpec((None,) + tile, imap)
    const2 = lambda shape: pl.BlockSpec(shape, lambda b, j: (0, 0))
    st = pl.BlockSpec((None, 1, ns2), lambda b, j: (b, 0, 0))
    final = final_args is not None
    in_specs = [tok]
    args = [u]
    if final:
        in_specs.append(tok)
        args.append(final_args[0])
    in_specs += [const2((S5_DIM, ns2)), const2((ns2, S5_DIM)), const2((4, S5_NSTATE)), st]
    args += [bblk, cblk, lam, state_in]
    win3 = pltpu.VMEM((S5_DIM // LANES, S5_WIN, LANES), F32)
    scratch = [win3, pltpu.VMEM((S5_WIN, ns2), F32),
               pltpu.VMEM((SUBLANES, ns2), F32), pltpu.VMEM((SUBLANES, ns2), F32),
               pltpu.VMEM((1, ns2), F32), win3]
    if final:
        _, d_skip, glu_w, glu_b, out_g = final_args
        in_specs += [const2((1, S5_DIM)), const2((S5_DIM, S5_DIM)), const2((1, S5_DIM)), const2((1, S5_DIM))]
        args += [d_skip.reshape(1, -1), glu_w.astype(BF16), glu_b.reshape(1, -1), out_g.reshape(1, -1)]
        scratch.append(win3)
    out, s_out = pl.pallas_call(
        functools.partial(_s5_kernel, reverse=reverse, final=final, colmajor=colmajor),
        out_shape=(jax.ShapeDtypeStruct(u.shape, F32), jax.ShapeDtypeStruct((n_batch, 1, ns2), F32)),
        grid=(n_batch, n_win),
        in_specs=in_specs,
        out_specs=(tok, st),
        scratch_shapes=scratch,
        compiler_params=_cparams(("arbitrary", "arbitrary"), VMEM_LIMIT),
        name="s5_" + ("rev" if reverse else "fwd") + ("_lat" if colmajor else "_ctx"),
    )(*args)
    return out, s_out


def _s5_operands(lam_re, lam_im, log_step, b_re, b_im, c_re, c_im):
    g, p, k = S5_GROUPS, S5_STATE, S5_GROUP
    lam = lax.complex(jnp.minimum(lam_re.astype(F32), -1e-4), lam_im.astype(F32))
    step = jnp.exp(log_step.astype(F32))[:, None]
    lam_bar = jnp.exp(lam * step)
    lam_seg = jnp.exp(lam * (step * S5_SEG))
    b_bar = ((lam_bar - 1.0) / lam)[..., None] * lax.complex(b_re.astype(F32), b_im.astype(F32))
    eye = jnp.eye(g, dtype=F32)
    bd_in = lambda m: jnp.einsum('gkp,gh->gkhp', jnp.transpose(m, (0, 2, 1)), eye).reshape(g * k, g * p)
    bblk = jnp.concatenate([bd_in(jnp.real(b_bar)), bd_in(jnp.imag(b_bar))], axis=1)
    bd_out = lambda m: jnp.einsum('gpk,gh->gphk', jnp.transpose(m, (0, 2, 1)), eye).reshape(g * p, g * k)
    cblk = jnp.concatenate([bd_out(c_re.astype(F32)), -bd_out(c_im.astype(F32))], axis=0)
    lam_rows = jnp.stack([jnp.real(lam_bar).reshape(-1), jnp.imag(lam_bar).reshape(-1),
                          jnp.real(lam_seg).reshape(-1), jnp.imag(lam_seg).reshape(-1)])
    return bblk.astype(BF16), cblk.astype(BF16), lam_rows


def _pack_bf16_pairs(v):
    n = v.shape[1] // 2
    bits = pltpu.bitcast(v.astype(BF16).astype(F32), U32)
    return (bits[:, :n] >> 16) | (bits[:, n:] & jnp.uint32(0xFFFF0000))


def _unpack_bf16_pairs(w):
    lo = pltpu.bitcast(w << 16, F32)
    hi = pltpu.bitcast(w & jnp.uint32(0xFFFF0000), F32)
    return lo, hi


def _mixout_kernel(r_ref, a_ref, b_ref, c_ref, wa_ref, wb_ref, wc_ref, g1_ref, ng_ref, sh_ref, sc_ref,
                   rw_ref, rb_ref, r1_ref, hp_ref, idx_ref, gate_ref):
    acc = _dot(a_ref[...], wa_ref[...]) + _dot(b_ref[...], wb_ref[...]) + _dot(c_ref[...], wc_ref[...])
    x = r_ref[...] + g1_ref[...] * acc
    r1_ref[...] = x
    ms = jnp.mean(x * x, axis=-1, keepdims=True)
    h = x * lax.rsqrt(ms + EPS) * ng_ref[...]
    h = h * (1.0 + sc_ref[...]) + sh_ref[...]
    hp_ref[...] = _pack_bf16_pairs(h)

    h_hi = h.astype(BF16)
    h_lo = (h - h_hi.astype(F32)).astype(BF16)
    rw = rw_ref[...]
    w_hi = rw.astype(BF16)
    w_lo = (rw - w_hi.astype(F32)).astype(BF16)
    logits = _dot(h_hi, w_hi) + _dot(h_lo, w_hi) + _dot(h_hi, w_lo) + rb_ref[...]

    tm = logits.shape[0]
    lane = lax.broadcasted_iota(I32, (tm, LANES), 1)
    lane_f = lane.astype(F32)
    work = logits
    tops, picks = [], []
    for _ in range(TOP_K):
        m = jnp.max(work, axis=-1, keepdims=True)
        pick = jnp.min(jnp.where(work == m, lane_f, float(LANES)), axis=-1, keepdims=True)
        work = jnp.where(lane_f == pick, -jnp.inf, work)
        tops.append(m)
        picks.append(pick)
    exps = [jnp.exp(t - tops[0]) for t in tops]
    denom = exps[0] + exps[1] + exps[2] + exps[3]
    idx_out = jnp.zeros((tm, LANES), F32)
    gate_out = jnp.zeros((tm, LANES), F32)
    for k in range(TOP_K):
        idx_out = jnp.where(lane == k, picks[k], idx_out)
        gate_out = jnp.where(lane == k, exps[k] / denom, gate_out)
    idx_ref[...] = idx_out.astype(I32)
    gate_ref[...] = gate_out


def _mix_out(r, mix_a, mix_b, mix_c, w_out, mod3, norm_g, router_w, router_b, lat_len, n_batch):
    nt, d = r.shape
    tm = ROW_TILE

    def grp(i):
        return jnp.minimum((i * tm) // lat_len, n_batch)

    rw = jnp.pad(router_w.astype(F32), ((0, 0), (0, LANES - N_EXPERTS)))
    rb = jnp.pad(router_b.astype(F32), (0, LANES - N_EXPERTS), constant_values=-1e30).reshape(1, LANES)
    modv = lambda k: pl.BlockSpec((None, 1, d), lambda i: (grp(i), 0, k))
    return pl.pallas_call(
        _mixout_kernel,
        out_shape=(jax.ShapeDtypeStruct((nt, d), F32), jax.ShapeDtypeStruct((nt, d // 2), U32),
                   jax.ShapeDtypeStruct((nt, LANES), I32), jax.ShapeDtypeStruct((nt, LANES), F32)),
        grid=(nt // tm,),
        in_specs=[pl.BlockSpec((tm, d), lambda i: (i, 0)),
                  pl.BlockSpec((tm, SSD_INNER), lambda i: (i, 0)),
                  pl.BlockSpec((tm, CONF_DIM), lambda i: (i, 0)),
                  pl.BlockSpec((tm, S5_DIM), lambda i: (i, 0)),
                  pl.BlockSpec((SSD_INNER, d), lambda i: (0, 0)),
                  pl.BlockSpec((CONF_DIM, d), lambda i: (SSD_INNER // CONF_DIM, 0)),
                  pl.BlockSpec((S5_DIM, d), lambda i: ((SSD_INNER + CONF_DIM) // S5_DIM, 0)),
                  modv(2),
                  pl.BlockSpec((1, d), lambda i: (0, 0)),
                  modv(3), modv(4),
                  pl.BlockSpec((d, LANES), lambda i: (0, 0)),
                  pl.BlockSpec((1, LANES), lambda i: (0, 0))],
        out_specs=(pl.BlockSpec((tm, d), lambda i: (i, 0)), pl.BlockSpec((tm, d // 2), lambda i: (i, 0)),
                   pl.BlockSpec((tm, LANES), lambda i: (i, 0)), pl.BlockSpec((tm, LANES), lambda i: (i, 0))),
        compiler_params=_cparams(("arbitrary",), VMEM_LIMIT),
        name="mix_out_router",
    )(r, mix_a, mix_b, mix_c, w_out, w_out, w_out, mod3, norm_g.reshape(1, d), mod3, mod3, rw, rb)


def _dispatch_kernel(dest_ref, lastblk_ref, h_hbm, xs_hbm, zero_ref, sem, *, tok_per_step):
    i = pl.program_id(0)
    n_assign = tok_per_step * TOP_K

    @pl.when(i == 0)
    def _():
        zero_ref[...] = jnp.zeros_like(zero_ref)

        def zero_block(b):
            return pltpu.make_async_copy(zero_ref, xs_hbm.at[pl.ds(b * MOE_BLOCK, MOE_BLOCK)], sem)

        for e in range(N_EXPERTS):
            zero_block(lastblk_ref[e]).start()
        for e in range(N_EXPERTS):
            zero_block(lastblk_ref[e]).wait()
        n_used = lastblk_ref[N_EXPERTS]
        n_blocks = xs_hbm.shape[0] // MOE_BLOCK

        def zstart(b, carry):
            zero_block(b).start()
            return carry

        def zwait(b, carry):
            zero_block(b).wait()
            return carry

        lax.fori_loop(n_used, n_blocks, zstart, 0)
        lax.fori_loop(n_used, n_blocks, zwait, 0)

    def row_copy(a):
        t = i * tok_per_step + a // TOP_K
        return pltpu.make_async_copy(h_hbm.at[pl.ds(t, 1)], xs_hbm.at[pl.ds(dest_ref[0, a], 1)], sem)

    def start(a, carry):
        row_copy(a).start()
        return carry

    def wait(a, carry):
        row_copy(a).wait()
        return carry

    lax.fori_loop(0, n_assign, start, 0)
    lax.fori_loop(0, n_assign, wait, 0)


def _dispatch(h_packed, dest, last_block, n_rows):
    nt, width = h_packed.shape
    tok_per_step = ROW_TILE
    n_steps = nt // tok_per_step
    return pl.pallas_call(
        functools.partial(_dispatch_kernel, tok_per_step=tok_per_step),
        out_shape=jax.ShapeDtypeStruct((n_rows, width), U32),
        grid=(n_steps,),
        in_specs=[pl.BlockSpec((None, 1, tok_per_step * TOP_K), lambda i: (i, 0, 0), memory_space=pltpu.SMEM),
                  pl.BlockSpec(memory_space=pltpu.SMEM),
                  pl.BlockSpec(memory_space=pl.ANY)],
        out_specs=pl.BlockSpec(memory_space=pl.ANY),
        scratch_shapes=[pltpu.VMEM((MOE_BLOCK, width), U32), pltpu.SemaphoreType.DMA],
        compiler_params=_cparams(("arbitrary",), VMEM_LIMIT),
        name="moe_dispatch",
    )(dest.reshape(n_steps, 1, tok_per_step * TOP_K), last_block, h_packed)


def _expert_kernel(be_ref, nv_ref, nu_ref, x_ref, wgu_ref, bgu_ref, wd_ref, bd_ref, o_ref):
    i = pl.program_id(0)
    nv = nv_ref[i]

    @pl.when(nv > 0)
    def _():
        half = D_MODEL // 2
        lo, hi = _unpack_bf16_pairs(x_ref[...])
        gu = (_dot(lo.astype(BF16), wgu_ref[0:half, :].astype(BF16))
              + _dot(hi.astype(BF16), wgu_ref[half:D_MODEL, :].astype(BF16)) + bgu_ref[...])
        gate = jnp.minimum(gu[:, :D_EXPERT], SWIGLU_LIMIT)
        lin = jnp.clip(gu[:, D_EXPERT:], -SWIGLU_LIMIT, SWIGLU_LIMIT)
        act = gate * jax.nn.sigmoid(SWIGLU_ALPHA * gate) * (lin + 1.0)
        y = _dot(act.astype(BF16), wd_ref[...].astype(BF16)) + bd_ref[...]
        o_ref[...] = _pack_bf16_pairs(y)

    @pl.when(nv == 0)
    def _():
        o_ref[...] = jnp.zeros_like(o_ref)


def _experts(xs, block_e, n_valid, w_gu, b_gu, w_down, b_down):
    n_rows, width = xs.shape
    n_blocks = n_rows // MOE_BLOCK
    ne = w_gu.shape[0]
    n_used = jnp.sum((n_valid > 0).astype(I32)).reshape(1)
    grid_spec = pltpu.PrefetchScalarGridSpec(
        num_scalar_prefetch=3,
        grid=(n_blocks,),
        in_specs=[pl.BlockSpec((MOE_BLOCK, width), lambda i, be, nv, nu: (jnp.minimum(i, nu[0] - 1), 0)),
                  pl.BlockSpec((None, D_MODEL, 2 * D_EXPERT), lambda i, be, nv, nu: (be[i], 0, 0)),
                  pl.BlockSpec((None, 1, 2 * D_EXPERT), lambda i, be, nv, nu: (be[i], 0, 0)),
                  pl.BlockSpec((None, D_EXPERT, D_MODEL), lambda i, be, nv, nu: (be[i], 0, 0)),
                  pl.BlockSpec((None, 1, D_MODEL), lambda i, be, nv, nu: (be[i], 0, 0))],
        out_specs=pl.BlockSpec((MOE_BLOCK, width), lambda i, be, nv, nu: (i, 0)),
    )
    return pl.pallas_call(
        _expert_kernel,
        out_shape=jax.ShapeDtypeStruct((n_rows, width), U32),
        grid_spec=grid_spec,
        compiler_params=_cparams(("arbitrary",), VMEM_LIMIT),
        name="moe_experts",
    )(block_e, n_valid, n_used, xs, w_gu, b_gu.reshape(ne, 1, -1), w_down, b_down.reshape(ne, 1, -1))


def _combine_kernel(dest_ref, r_ref, gate_ref, g2_ref, fg_ref, yb_hbm, o_ref, buf_ref, sem, *, last_layer):
    tm = ROW_TILE
    n_assign = tm * TOP_K

    def row_copy(a):
        return pltpu.make_async_copy(yb_hbm.at[pl.ds(dest_ref[0, a], 1)],
                                     buf_ref.at[a % TOP_K, pl.ds(a // TOP_K, 1)], sem)

    def start(a, carry):
        row_copy(a).start()
        return carry

    def wait(a, carry):
        row_copy(a).wait()
        return carry

    lax.fori_loop(0, n_assign, start, 0)
    lax.fori_loop(0, n_assign, wait, 0)

    half = D_MODEL // 2
    acc_lo = jnp.zeros((tm, half), F32)
    acc_hi = jnp.zeros((tm, half), F32)
    for k in range(TOP_K):
        lo, hi = _unpack_bf16_pairs(buf_ref[k])
        gk = gate_ref[:, k:k + 1]
        acc_lo = acc_lo + gk * lo
        acc_hi = acc_hi + gk * hi
    o_ref[:, 0:half] = r_ref[:, 0:half] + g2_ref[:, 0:half] * acc_lo
    o_ref[:, half:D_MODEL] = r_ref[:, half:D_MODEL] + g2_ref[:, half:D_MODEL] * acc_hi
    if last_layer:
        x = o_ref[...]
        ms = jnp.mean(x * x, axis=-1, keepdims=True)
        o_ref[...] = x * lax.rsqrt(ms + EPS) * fg_ref[...]


def _combine(r1, gates, dest, yb, mod3, final_g, lat_len, n_batch, last_layer):
    nt, d = r1.shape
    tm = ROW_TILE
    n_steps = nt // tm

    def grp(i):
        return jnp.minimum((i * tm) // lat_len, n_batch)

    return pl.pallas_call(
        functools.partial(_combine_kernel, last_layer=last_layer),
        out_shape=jax.ShapeDtypeStruct((nt, d), F32),
        grid=(n_steps,),
        in_specs=[pl.BlockSpec((None, 1, tm * TOP_K), lambda i: (i, 0, 0), memory_space=pltpu.SMEM),
                  pl.BlockSpec((tm, d), lambda i: (i, 0)),
                  pl.BlockSpec((tm, LANES), lambda i: (i, 0)),
                  pl.BlockSpec((None, 1, d), lambda i: (grp(i), 0, 5)),
                  pl.BlockSpec((1, d), lambda i: (0, 0)),
                  pl.BlockSpec(memory_space=pl.ANY)],
        out_specs=pl.BlockSpec((tm, d), lambda i: (i, 0)),
        scratch_shapes=[pltpu.VMEM((TOP_K, tm, d // 2), U32), pltpu.SemaphoreType.DMA],
        compiler_params=_cparams(("arbitrary",), VMEM_LIMIT),
        name="moe_combine",
    )(dest.reshape(n_steps, 1, tm * TOP_K), r1, gates, mod3, final_g.reshape(1, d), yb)


def _routing_plan(top_idx, n_blocks):
    flat_e = top_idx.reshape(-1)
    onehot = (flat_e[:, None] == jnp.arange(N_EXPERTS, dtype=I32)[None, :]).astype(I32)
    csum = jnp.cumsum(onehot, axis=0)
    rank = jnp.take_along_axis(csum, flat_e[:, None], axis=1)[:, 0] - 1
    counts = csum[-1]
    padded = (counts + MOE_BLOCK - 1) // MOE_BLOCK * MOE_BLOCK
    pad_end = jnp.cumsum(padded)
    pad_start = pad_end - padded
    dest = (pad_start[flat_e] + rank).astype(I32)
    blk_start = jnp.arange(n_blocks, dtype=I32) * MOE_BLOCK
    block_e = jnp.minimum(jnp.searchsorted(pad_end, blk_start, side='right'), N_EXPERTS - 1).astype(I32)
    used = blk_start < pad_end[-1]
    n_valid = jnp.where(used, jnp.clip(pad_start[block_e] + counts[block_e] - blk_start, 0, MOE_BLOCK), 0)
    last_block = jnp.maximum(pad_end // MOE_BLOCK - 1, 0)
    zero_plan = jnp.concatenate([last_block, pad_end[-1:] // MOE_BLOCK]).astype(I32)
    return dest, block_e, n_valid.astype(I32), zero_plan


def _forward(x, c, ctx, c_ctx, ada_w, ada_b, norm1_g, w_in, ssd_conv_w, ssd_conv_b, ssd_a_log,
             ssd_dt_bias, ssd_d, ssd_norm_g, conf_dw_w, conf_dw_b, conf_ln_g, conf_ln_b, conf_pw_w,
             conf_pw_b, conf_out_g, s5_lam_re, s5_lam_im, s5_log_step, s5_b_re, s5_b_im, s5_c_re,
             s5_c_im, s5_d, s5_glu_w, s5_glu_b, s5_out_g, w_out, norm2_g, router_w, router_b,
             w_gate_up, b_gate_up, w_down, b_down, final_norm_g):
    n_batch, lat_len, d = x.shape
    ctx_len = ctx.shape[1]
    depth = ada_w.shape[0]
    n_lat = n_batch * lat_len
    nt = n_lat + n_batch * ctx_len
    assert d == D_MODEL and ctx_len == ROW_TILE and lat_len % MM_TILE_M == 0
    assert lat_len // GRID_W == GRID_W

    r = jnp.concatenate([x.reshape(n_lat, d), ctx.reshape(n_batch * ctx_len, d)], axis=0).astype(F32)
    cond = jnp.zeros((SUBLANES, d), F32).at[:n_batch].set(c).at[n_batch].set(c_ctx)
    mods = _modulation(cond, ada_w, ada_b)

    n_blocks = nt * TOP_K // MOE_BLOCK + N_EXPERTS
    n_rows = n_blocks * MOE_BLOCK
    s5_zero = jnp.zeros((n_batch, 1, 2 * S5_NSTATE), F32)

    for l in range(depth):
        mod3 = mods[l].reshape(SUBLANES, 1, 6 * d)
        wl = w_in[l]
        c_dt = SSD_INNER + SSD_CONV_DIM
        c_conf = c_dt + SSD_HEADS
        c_s5 = c_conf + 2 * CONF_DIM
        w_main = jnp.concatenate([wl[:, :c_dt], wl[:, c_conf:c_s5]], axis=1).astype(BF16)
        w_side = jnp.concatenate([wl[:, c_s5:], wl[:, c_dt:c_conf],
                                  jnp.zeros((d, LANES - SSD_HEADS), wl.dtype)], axis=1).astype(BF16)

        p_main = _norm_matmul(r, norm1_g[l], mod3, 0, 1, w_main, lat_len, n_batch, "in_proj_main")
        p_s5, p_dt = _norm_matmul(r, norm1_g[l], mod3, 0, 1, w_side, lat_len, n_batch, "in_proj_side",
                                  splits=(S5_DIM, LANES))

        xbc = _ssd_conv(p_main, ssd_conv_w[l], ssd_conv_b[l], lat_len, ctx_len, n_batch)
        y_fwd = _ssd_scan(xbc, p_main, p_dt, ssd_dt_bias[l, 0], ssd_a_log[l, 0], lat_len, ctx_len, n_batch,
                          False)
        d_skip = jnp.repeat(ssd_d[l].astype(F32), SSD_HEAD_DIM)
        mix_a = _ssd_scan(xbc, p_main, p_dt, ssd_dt_bias[l, 1], ssd_a_log[l, 1], lat_len, ctx_len, n_batch,
                          True, (y_fwd, d_skip, ssd_norm_g[l]))

        mix_b = _conformer(p_main, conf_dw_w[l], conf_dw_b[l], conf_ln_g[l], conf_ln_b[l], conf_pw_w[l],
                           conf_pw_b[l], conf_out_g[l], lat_len, ctx_len, n_batch)

        u_lat = p_s5[:n_lat].reshape(n_batch, lat_len // GRID_W, GRID_W * S5_DIM)
        u_ctx = p_s5[n_lat:].reshape(n_batch, ctx_len, S5_DIM)
        ops = [_s5_operands(s5_lam_re[l, dd], s5_lam_im[l, dd], s5_log_step[l, dd], s5_b_re[l, dd],
                            s5_b_im[l, dd], s5_c_re[l, dd], s5_c_im[l, dd]) for dd in range(2)]
        yc_f, st_f = _s5_scan(u_ctx, *ops[0], s5_zero, False, False)
        yl_f, _ = _s5_scan(u_lat, *ops[0], st_f, False, True)
        fin = (s5_d[l], s5_glu_w[l], s5_glu_b[l], s5_out_g[l])
        mc_c, st_r = _s5_scan(u_ctx, *ops[1], s5_zero, True, False, (yc_f,) + fin)
        mc_l, _ = _s5_scan(u_lat, *ops[1], st_r, True, True, (yl_f,) + fin)
        mix_c = jnp.concatenate([mc_l.reshape(n_lat, S5_DIM), mc_c.reshape(n_batch * ctx_len, S5_DIM)], axis=0)

        r1, h_packed, top_idx, gates = _mix_out(
            r, mix_a.astype(BF16), mix_b.astype(BF16), mix_c.astype(BF16), w_out[l].astype(BF16), mod3,
            norm2_g[l], router_w[l], router_b[l], lat_len, n_batch)

        dest, block_e, n_valid, last_block = _routing_plan(top_idx[:, :TOP_K], n_blocks)
        xs = _dispatch(h_packed, dest, last_block, n_rows)
        yb = _experts(xs, block_e, n_valid, w_gate_up[l], b_gate_up[l], w_down[l], b_down[l])
        r = _combine(r1, gates, dest, yb, mod3, final_norm_g, lat_len, n_batch, l == depth - 1)

    return r[:n_lat].reshape(n_batch, lat_len, d).astype(x.dtype)


def kernel(x, c, ctx, c_ctx, ada_w, ada_b, norm1_g, w_in, ssd_conv_w, ssd_conv_b, ssd_a_log, ssd_dt_bias,
           ssd_d, ssd_norm_g, conf_dw_w, conf_dw_b, conf_ln_g, conf_ln_b, conf_pw_w, conf_pw_b, conf_out_g,
           s5_lam_re, s5_lam_im, s5_log_step, s5_b_re, s5_b_im, s5_c_re, s5_c_im, s5_d, s5_glu_w, s5_glu_b,
           s5_out_g, w_out, norm2_g, router_w, router_b, w_gate_up, b_gate_up, w_down, b_down, final_norm_g):
    return _forward(x, c, ctx, c_ctx, ada_w, ada_b, norm1_g, w_in, ssd_conv_w, ssd_conv_b, ssd_a_log,
                    ssd_dt_bias, ssd_d, ssd_norm_g, conf_dw_w, conf_dw_b, conf_ln_g, conf_ln_b, conf_pw_w,
                    conf_pw_b, conf_out_g, s5_lam_re, s5_lam_im, s5_log_step, s5_b_re, s5_b_im, s5_c_re,
                    s5_c_im, s5_d, s5_glu_w, s5_glu_b, s5_out_g, w_out, norm2_g, router_w, router_b,
                    w_gate_up, b_gate_up, w_down, b_down, final_norm_g)
```

```python
import functools
import math

import jax
import jax.numpy as jnp
from jax import lax
from jax.experimental import pallas as pl
from jax.experimental.pallas import tpu as pltpu

F32 = jnp.float32
BF16 = jnp.bfloat16
I32 = jnp.int32
U32 = jnp.uint32

D_MODEL = 2048
GRID_W = 64
SSD_INNER = 1024
SSD_HEAD_DIM = 64
SSD_HEADS = 16
SSD_GROUPS = 4
SSD_STATE = 128
SSD_CONV = 5
SSD_CHUNK = 128
SSD_CONV_DIM = SSD_INNER + 2 * SSD_GROUPS * SSD_STATE
CONF_DIM = 512
CONF_KERNEL = 31
S5_DIM = 512
S5_GROUP = 16
S5_GROUPS = 32
S5_STATE = 64
S5_NSTATE = S5_GROUPS * S5_STATE
N_EXPERTS = 32
TOP_K = 4
D_EXPERT = 768
SWIGLU_LIMIT = 7.0
SWIGLU_ALPHA = 1.702
MOE_BLOCK = 256
EPS = 1e-6

LANES = 128
SUBLANES = 8
ROW_TILE = 256
MM_TILE_M = 512
MM_TILE_N = 512
MAIN_COLS = 4096
COL_Z, COL_X, COL_CONF = 0, 1024, 3072
SIDE_COLS = S5_DIM + LANES
CONV_HALO = 8
CONF_HALO = 16
S5_SEG = 32
S5_WIN = SUBLANES * S5_SEG
VMEM_LIMIT = 56 * 1024 * 1024


def _cparams(sem, vmem=VMEM_LIMIT):
    return pltpu.CompilerParams(dimension_semantics=sem, vmem_limit_bytes=vmem)


def _silu(v):
    return v * jax.nn.sigmoid(v)


def _split3(v):
    hi = v.astype(BF16)
    r1 = v - hi.astype(F32)
    mid = r1.astype(BF16)
    lo = (r1 - mid.astype(F32)).astype(BF16)
    return hi, mid, lo


def _dot(a, b):
    return jnp.dot(a, b, preferred_element_type=F32)


def _layer_rows(arr, l):
    depth = arr.shape[0]
    a3 = arr.reshape(depth, 1, -1)
    return a3, pl.BlockSpec((None, 1, a3.shape[2]), lambda *_: (l, 0, 0))


def _layer_mat(arr, l):
    return arr, pl.BlockSpec((None,) + arr.shape[1:], lambda *_: (l, 0, 0))


def _mod_kernel(c_ref, w_ref, b_ref, o_ref):
    s = _silu(c_ref[...])
    w = w_ref[...]
    s_hi = s.astype(BF16)
    s_lo = (s - s_hi.astype(F32)).astype(BF16)
    w_hi = w.astype(BF16)
    w_lo = (w - w_hi.astype(F32)).astype(BF16)
    acc = _dot(s_hi, w_hi) + _dot(s_lo, w_hi) + _dot(s_hi, w_lo)
    o_ref[...] = acc + b_ref[...]


def _modulation(cond, ada_w, ada_b):
    depth, d, n = ada_w.shape
    tn = 1024
    return pl.pallas_call(
        _mod_kernel,
        out_shape=jax.ShapeDtypeStruct((depth, SUBLANES, n), F32),
        grid=(depth, n // tn),
        in_specs=[pl.BlockSpec((SUBLANES, d), lambda l, j: (0, 0)),
                  pl.BlockSpec((None, d, tn), lambda l, j: (l, 0, j)),
                  pl.BlockSpec((None, 1, tn), lambda l, j: (l, 0, j))],
        out_specs=pl.BlockSpec((None, SUBLANES, tn), lambda l, j: (l, 0, j)),
        compiler_params=_cparams(("arbitrary", "arbitrary")),
        name="adaln_mod",
    )(cond, ada_w, ada_b.reshape(depth, 1, n))


def _mod_spec(l, grp, k, d):
    return pl.BlockSpec((None, None, 1, d), lambda i, *_: (l, grp(i), 0, k))


def _normmm_kernel(x_ref, g_ref, sh_ref, sc_ref, w_ref, *rest):
    o_refs, hn_ref = rest[:-1], rest[-1]

    @pl.when(pl.program_id(1) == 0)
    def _():
        x = x_ref[...]
        ms = jnp.mean(x * x, axis=-1, keepdims=True)
        y = x * lax.rsqrt(ms + EPS) * g_ref[...]
        hn_ref[...] = (y * (1.0 + sc_ref[...]) + sh_ref[...]).astype(BF16)

    res = _dot(hn_ref[...], w_ref[...])
    col = 0
    for o_ref in o_refs:
        o_ref[...] = res[:, col:col + o_ref.shape[1]].astype(o_ref.dtype)
        col += o_ref.shape[1]


def _norm_matmul(r, gains, mods4, l, w_all, lat_len, n_batch, name, splits=None):
    nt, d = r.shape
    n = w_all.shape[2]
    tm = MM_TILE_M if nt % MM_TILE_M == 0 else ROW_TILE
    tn = n if splits else MM_TILE_N
    assert n % tn == 0
    widths = splits or (tn,)

    def grp(i):
        return jnp.minimum((i * tm) // lat_len, n_batch)

    g3, g_spec = _layer_rows(gains, l)
    outs = pl.pallas_call(
        _normmm_kernel,
        out_shape=tuple(jax.ShapeDtypeStruct((nt, n if not splits else wd), F32) for wd in widths),
        grid=(nt // tm, n // tn),
        in_specs=[pl.BlockSpec((tm, d), lambda i, j: (i, 0)),
                  g_spec, _mod_spec(l, grp, 0, d), _mod_spec(l, grp, 1, d),
                  pl.BlockSpec((None, d, tn), lambda i, j: (l, 0, j))],
        out_specs=tuple(pl.BlockSpec((tm, wd), lambda i, j: (i, j)) for wd in widths),
        scratch_shapes=[pltpu.VMEM((tm, d), BF16)],
        compiler_params=_cparams(("arbitrary", "arbitrary")),
        name=name,
    )(r, g3, mods4, mods4, w_all)
    return outs if splits else outs[0]


def _seq_edges(i, tiles_per_lat_seq, n_lat_tiles):
    is_lat = i < n_lat_tiles
    first = jnp.logical_or(jnp.logical_not(is_lat), (i % tiles_per_lat_seq) == 0)
    last = jnp.logical_or(jnp.logical_not(is_lat), (i % tiles_per_lat_seq) == tiles_per_lat_seq - 1)
    return first, last


def _conv5_kernel(cur_ref, prev_ref, next_ref, w_ref, b_ref, o_ref, ext_ref, *, tiles_per_lat_seq,
                  n_lat_tiles):
    i = pl.program_id(0)
    first, last = _seq_edges(i, tiles_per_lat_seq, n_lat_tiles)
    h, tm = CONV_HALO, ROW_TILE
    ext_ref[0:h, :] = jnp.where(first, 0.0, prev_ref[...])
    ext_ref[h:h + tm, :] = cur_ref[...]
    ext_ref[h + tm:h + tm + h, :] = jnp.where(last, 0.0, next_ref[...])
    pad = (SSD_CONV - 1) // 2
    cw = 512
    for c in range(0, ext_ref.shape[1], cw):
        acc = jnp.broadcast_to(b_ref[:, c:c + cw], (tm, cw))
        for j in range(SSD_CONV):
            acc = acc + w_ref[j:j + 1, c:c + cw] * ext_ref[h - pad + j:h - pad + j + tm, c:c + cw]
        o_ref[:, c:c + cw] = _silu(acc).astype(BF16)


def _ssd_conv(p_main, conv_w, conv_b, l, lat_len, ctx_len, n_batch):
    nt = p_main.shape[0]
    tm, h = ROW_TILE, CONV_HALO
    assert ctx_len == tm and lat_len % tm == 0
    n_lat_tiles = n_batch * lat_len // tm
    cw = SSD_CONV_DIM // 2
    xblk = COL_X // cw
    nhb = nt // h
    depth = conv_w.shape[0]
    kern = functools.partial(_conv5_kernel, tiles_per_lat_seq=lat_len // tm, n_lat_tiles=n_lat_tiles)
    return pl.pallas_call(
        kern,
        out_shape=jax.ShapeDtypeStruct((nt, SSD_CONV_DIM), BF16),
        grid=(nt // tm, 2),
        in_specs=[pl.BlockSpec((tm, cw), lambda i, j: (i, xblk + j)),
                  pl.BlockSpec((h, cw), lambda i, j: (jnp.maximum(i * (tm // h) - 1, 0), xblk + j)),
                  pl.BlockSpec((h, cw), lambda i, j: (jnp.minimum((i + 1) * (tm // h), nhb - 1), xblk + j)),
                  pl.BlockSpec((None, SSD_CONV, cw), lambda i, j: (l, 0, j)),
                  pl.BlockSpec((None, 1, cw), lambda i, j: (l, 0, j))],
        out_specs=pl.BlockSpec((tm, cw), lambda i, j: (i, j)),
        scratch_shapes=[pltpu.VMEM((tm + 2 * h, cw), F32)],
        compiler_params=_cparams(("arbitrary", "arbitrary")),
        name="ssd_conv",
    )(p_main, p_main, p_main, conv_w, conv_b.reshape(depth, 1, -1))


def _head_cols(vals, width):
    lane = lax.broadcasted_iota(I32, (1, LANES), 1)
    halves = []
    per_half = LANES // width
    for hh in range(len(vals) // per_half):
        sel = vals[hh * per_half + per_half - 1]
        for k in range(per_half - 2, -1, -1):
            sel = jnp.where(lane < (k + 1) * width, vals[hh * per_half + k], sel)
        halves.append(sel)
    return jnp.concatenate(halves, axis=1)


def _ssd_kernel(*refs, reverse, final):
    if final:
        (x_ref, b_ref, c_ref, dt_ref, dtb_ref, alog_ref, yprev_ref, z_ref, dskip_ref, ng_ref,
         o_ref, state_ref, tmp_ref) = refs
    else:
        x_ref, b_ref, c_ref, dt_ref, dtb_ref, alog_ref, o_ref, state_ref = refs
    q = SSD_CHUNK
    r = SSD_HEADS // SSD_GROUPS
    gw = r * SSD_HEAD_DIM

    @pl.when(pl.program_id(1) == 0)
    def _():
        state_ref[...] = jnp.zeros_like(state_ref)

    lane = lax.broadcasted_iota(I32, (1, LANES), 1)
    dtp = jax.nn.softplus(dt_ref[...] + dtb_ref[...])
    a_head = -jnp.exp(alog_ref[...])
    a = jnp.where(lane < SSD_HEADS, dtp * a_head, 0.0)
    ri = lax.broadcasted_iota(I32, (q, q), 0)
    ci = lax.broadcasted_iota(I32, (q, q), 1)
    tri = (ci >= ri) if reverse else (ci <= ri)
    tri_b = jnp.where(tri, 1.0, 0.0).astype(BF16)
    a_hi, a_mid, a_lo = _split3(a)
    a_cs = _dot(tri_b, a_hi) + _dot(tri_b, a_mid) + _dot(tri_b, a_lo)
    a_cs_t = a_cs.T
    dtp_t = dtp.T
    a_end = a_cs[0:1, :] if reverse else a_cs[q - 1:q, :]
    lane_g = lax.broadcasted_iota(I32, (1, gw), 1)

    x = x_ref[...]
    for g in range(SSD_GROUPS):
        cg = c_ref[:, g * SSD_STATE:(g + 1) * SSD_STATE]
        bg = b_ref[:, g * SSD_STATE:(g + 1) * SSD_STATE]
        cb = lax.dot_general(cg, bg, (((1,), (1,)), ((), ())), preferred_element_type=F32)
        xg = x[:, g * gw:(g + 1) * gw]
        yg = jnp.zeros((q, gw), F32)
        ecols, wcols, dec = [], [], []
        for hl in range(r):
            h = g * r + hl
            col = a_cs[:, h:h + 1]
            row = a_cs_t[h:h + 1, :]
            lm = jnp.where(tri, jnp.exp(col - row), 0.0)
            m = (cb * lm * dtp_t[h:h + 1, :]).astype(BF16)
            in_head = jnp.logical_and(lane_g >= hl * SSD_HEAD_DIM, lane_g < (hl + 1) * SSD_HEAD_DIM)
            xm = jnp.where(in_head, xg, jnp.zeros_like(xg))
            yg = yg + _dot(m, xm)
            e_h = a_end[:, h:h + 1]
            ecols.append(jnp.exp(col))
            wcols.append(dtp[:, h:h + 1] * jnp.exp(e_h - col))
            dec.append(jnp.exp(e_h))
        s_old = state_ref[g]
        yg = yg + _head_cols(ecols, SSD_HEAD_DIM) * _dot(cg, s_old.astype(BF16))
        xw = (xg.astype(F32) * _head_cols(wcols, SSD_HEAD_DIM)).astype(BF16)
        upd = lax.dot_general(bg, xw, (((0,), (0,)), ((), ())), preferred_element_type=F32)
        state_ref[g] = _head_cols(dec, SSD_HEAD_DIM) * s_old + upd
        sl = slice(g * gw, (g + 1) * gw)
        if final:
            ytot = yprev_ref[:, sl] + yg + dskip_ref[:, sl] * xg.astype(F32)
            tmp_ref[:, sl] = ytot * _silu(z_ref[:, sl])
        else:
            o_ref[:, sl] = yg
    if final:
        gated = tmp_ref[...]
        ms = jnp.mean(gated * gated, axis=-1, keepdims=True)
        o_ref[...] = (gated * lax.rsqrt(ms + EPS) * ng_ref[...]).astype(o_ref.dtype)


def _ssd_scan(xbc, p_main, p_dt, dtb_all, alog_all, l, lat_len, ctx_len, n_batch, reverse, final_args=None):
    nt = xbc.shape[0]
    q = SSD_CHUNK
    ncl, ncc = lat_len // q, ctx_len // q
    ctx0 = n_batch * ncl
    dd = 1 if reverse else 0

    def blk(b, j):
        if reverse:
            return jnp.where(j < ncc, ctx0 + b * ncc + (ncc - 1 - j), b * ncl + (ncl - 1 - (j - ncc)))
        return jnp.where(j < ncc, ctx0 + b * ncc + j, b * ncl + (j - ncc))

    head_vec = pl.BlockSpec((None, 1, LANES), lambda b, j: (2 * l + dd, 0, 0))
    final = final_args is not None
    in_specs = [pl.BlockSpec((q, SSD_INNER), lambda b, j: (blk(b, j), 0)),
                pl.BlockSpec((q, SSD_GROUPS * SSD_STATE), lambda b, j: (blk(b, j), 2)),
                pl.BlockSpec((q, SSD_GROUPS * SSD_STATE), lambda b, j: (blk(b, j), 3)),
                pl.BlockSpec((q, LANES), lambda b, j: (blk(b, j), 0)),
                head_vec, head_vec]
    args = [xbc, xbc, xbc, p_dt, dtb_all, alog_all]
    scratch = [pltpu.VMEM((SSD_GROUPS, SSD_STATE, (SSD_HEADS // SSD_GROUPS) * SSD_HEAD_DIM), F32)]
    if final:
        y_prev, d_skip_all, norm_g_all = final_args
        ds3, ds_spec = _layer_rows(d_skip_all, l)
        ng3, ng_spec = _layer_rows(norm_g_all, l)
        in_specs += [pl.BlockSpec((q, SSD_INNER), lambda b, j: (blk(b, j), 0)),
                     pl.BlockSpec((q, SSD_INNER), lambda b, j: (blk(b, j), COL_Z // SSD_INNER)),
                     ds_spec, ng_spec]
        args += [y_prev, p_main, ds3, ng3]
        scratch.append(pltpu.VMEM((q, SSD_INNER), F32))
    return pl.pallas_call(
        functools.partial(_ssd_kernel, reverse=reverse, final=final),
        out_shape=jax.ShapeDtypeStruct((nt, SSD_INNER), BF16 if final else F32),
        grid=(n_batch, ncc + ncl),
        in_specs=in_specs,
        out_specs=pl.BlockSpec((q, SSD_INNER), lambda b, j: (blk(b, j), 0)),
        scratch_shapes=scratch,
        compiler_params=_cparams(("arbitrary", "arbitrary")),
        name="ssd_scan_rev" if reverse else "ssd_scan_fwd",
    )(*args)


def _glu(ref):
    v = ref[:, 0:CONF_DIM]
    gt = ref[:, CONF_DIM:2 * CONF_DIM]
    return v * jax.nn.sigmoid(gt)


def _conf_kernel(cur_ref, prev_ref, next_ref, dww_ref, dwb_ref, lng_ref, lnb_ref, pww_ref, pwb_ref,
                 og_ref, o_ref, ext_ref, acc_ref, *, tiles_per_lat_seq, n_lat_tiles):
    i = pl.program_id(0)
    first, last = _seq_edges(i, tiles_per_lat_seq, n_lat_tiles)
    h, tm = CONF_HALO, ROW_TILE
    ext_ref[0:h, :] = jnp.where(first, 0.0, _glu(prev_ref))
    ext_ref[h:h + tm, :] = _glu(cur_ref)
    ext_ref[h + tm:h + tm + h, :] = jnp.where(last, 0.0, _glu(next_ref))
    pad = (CONF_KERNEL - 1) // 2
    cw = 256
    for c in range(0, CONF_DIM, cw):
        acc = jnp.broadcast_to(dwb_ref[:, c:c + cw], (tm, cw))
        for j in range(CONF_KERNEL):
            acc = acc + dww_ref[j:j + 1, c:c + cw] * ext_ref[h - pad + j:h - pad + j + tm, c:c + cw]
        acc_ref[:, c:c + cw] = acc
    u = acc_ref[...]
    mu = jnp.mean(u, axis=-1, keepdims=True)
    var = jnp.mean(jnp.square(u - mu), axis=-1, keepdims=True)
    y = (u - mu) * lax.rsqrt(var + EPS) * lng_ref[...] + lnb_ref[...]
    y = _silu(y)
    v = _dot(y.astype(BF16), pww_ref[...]) + pwb_ref[...]
    ms = jnp.mean(v * v, axis=-1, keepdims=True)
    o_ref[...] = (v * lax.rsqrt(ms + EPS) * og_ref[...]).astype(o_ref.dtype)


def _conformer(p_main, dw_w, dw_b, ln_g, ln_b, pw_w_bf, pw_b, out_g, l, lat_len, ctx_len, n_batch):
    nt = p_main.shape[0]
    tm, h = ROW_TILE, CONF_HALO
    n_lat_tiles = n_batch * lat_len // tm
    cblk = COL_CONF // (2 * CONF_DIM)
    nhb = nt // h
    kern = functools.partial(_conf_kernel, tiles_per_lat_seq=lat_len // tm, n_lat_tiles=n_lat_tiles)
    rows = [_layer_rows(a, l) for a in (dw_b, ln_g, ln_b)]
    rows2 = [_layer_rows(a, l) for a in (pw_b, out_g)]
    dww, dww_spec = _layer_mat(dw_w, l)
    pww, pww_spec = _layer_mat(pw_w_bf, l)
    return pl.pallas_call(
        kern,
        out_shape=jax.ShapeDtypeStruct((nt, CONF_DIM), BF16),
        grid=(nt // tm,),
        in_specs=[pl.BlockSpec((tm, 2 * CONF_DIM), lambda i: (i, cblk)),
                  pl.BlockSpec((h, 2 * CONF_DIM), lambda i: (jnp.maximum(i * (tm // h) - 1, 0), cblk)),
                  pl.BlockSpec((h, 2 * CONF_DIM), lambda i: (jnp.minimum((i + 1) * (tm // h), nhb - 1), cblk)),
                  dww_spec] + [s for _, s in rows] + [pww_spec] + [s for _, s in rows2],
        out_specs=pl.BlockSpec((tm, CONF_DIM), lambda i: (i, 0)),
        scratch_shapes=[pltpu.VMEM((tm + 2 * h, CONF_DIM), F32), pltpu.VMEM((tm, CONF_DIM), F32)],
        compiler_params=_cparams(("arbitrary",)),
        name="conformer",
    )(p_main, p_main, p_main, dww, *[a for a, _ in rows], pww, *[a for a, _ in rows2])


def _gelu_tanh(v):
    return 0.5 * v * (1.0 + jnp.tanh(math.sqrt(2.0 / math.pi) * (v + 0.044715 * (v * v * v))))


def _s5_kernel(*refs, reverse, final, colmajor):
    if final:
        (u_ref, yprev_ref, bblk_ref, cblk_ref, lam_ref, sin_ref, dskip_ref, gw_ref, gb_ref, og_ref,
         o_ref, sout_ref, lhs_ref, h_ref, fin_ref, init_ref, carry_ref, y_ref, tot_ref) = refs
    else:
        (u_ref, bblk_ref, cblk_ref, lam_ref, sin_ref,
         o_ref, sout_ref, lhs_ref, h_ref, fin_ref, init_ref, carry_ref, y_ref) = refs
    ns = S5_NSTATE
    seg, nsub = S5_SEG, SUBLANES
    jw = pl.program_id(1)

    @pl.when(jw == 0)
    def _():
        carry_ref[...] = sin_ref[...]

    def sub_block(s):
        if colmajor:
            half = GRID_W // seg
            return (slice((s % half) * seg, (s % half + 1) * seg),
                    slice((s // half) * S5_DIM, (s // half + 1) * S5_DIM))
        return slice(s * seg, (s + 1) * seg), slice(0, S5_DIM)

    n_lb = S5_DIM // LANES

    def lane_block(cs, k):
        return slice(cs.start + k * LANES, cs.start + (k + 1) * LANES)

    def gathered(ref):
        return jnp.concatenate([ref[k] for k in range(n_lb)], axis=1)

    for s in range(nsub):
        rs, cs = sub_block(s)
        for k in range(n_lb):
            lhs_ref[k, pl.ds(s, seg, stride=nsub), :] = u_ref[rs, lane_block(cs, k)]
            if final:
                tot_ref[k, pl.ds(s, seg, stride=nsub), :] = yprev_ref[rs, lane_block(cs, k)]

    u_win = gathered(lhs_ref)
    h_ref[...] = _dot(u_win.astype(BF16), bblk_ref[...])

    cw = 512
    n_chunks = ns // cw

    def lam_chunk(row, c):
        return (jnp.broadcast_to(lam_ref[row:row + 1, c * cw:(c + 1) * cw], (nsub, cw)),
                jnp.broadcast_to(lam_ref[row + 1:row + 2, c * cw:(c + 1) * cw], (nsub, cw)))

    def row0(i):
        step = (seg - 1 - i) if reverse else i
        return pl.multiple_of(step * nsub, nsub)

    for c in range(n_chunks):
        lre, lim = lam_chunk(0, c)
        cre = slice(c * cw, (c + 1) * cw)
        cim = slice(ns + c * cw, ns + (c + 1) * cw)

        def step1(i, hc, cre=cre, cim=cim, lre=lre, lim=lim):
            hre, him = hc
            r0 = row0(i)
            nre = lre * hre - lim * him + h_ref[pl.ds(r0, nsub), cre]
            nim = lre * him + lim * hre + h_ref[pl.ds(r0, nsub), cim]
            h_ref[pl.ds(r0, nsub), cre] = nre
            h_ref[pl.ds(r0, nsub), cim] = nim
            return nre, nim

        z0 = jnp.zeros((nsub, cw), F32)
        fre, fim = lax.fori_loop(0, seg, step1, (z0, z0), unroll=4)
        fin_ref[:, cre] = fre
        fin_ref[:, cim] = fim

    gre, gim = lam_ref[2:3, :], lam_ref[3:4, :]
    cur_re, cur_im = carry_ref[:, 0:ns], carry_ref[:, ns:2 * ns]
    order = range(nsub - 1, -1, -1) if reverse else range(nsub)
    for s in order:
        init_ref[s:s + 1, 0:ns] = cur_re
        init_ref[s:s + 1, ns:2 * ns] = cur_im
        f_re, f_im = fin_ref[s:s + 1, 0:ns], fin_ref[s:s + 1, ns:2 * ns]
        cur_re, cur_im = gre * cur_re - gim * cur_im + f_re, gre * cur_im + gim * cur_re + f_im
    carry_ref[:, 0:ns] = cur_re
    carry_ref[:, ns:2 * ns] = cur_im

    for c in range(n_chunks):
        lre, lim = lam_chunk(0, c)
        cre = slice(c * cw, (c + 1) * cw)
        cim = slice(ns + c * cw, ns + (c + 1) * cw)

        def step2(i, gc, cre=cre, cim=cim, lre=lre, lim=lim):
            g_re, g_im = gc
            n_re = lre * g_re - lim * g_im
            n_im = lre * g_im + lim * g_re
            r0 = row0(i)
            h_ref[pl.ds(r0, nsub), cre] = h_ref[pl.ds(r0, nsub), cre] + n_re
            h_ref[pl.ds(r0, nsub), cim] = h_ref[pl.ds(r0, nsub), cim] + n_im
            return n_re, n_im

        lax.fori_loop(0, seg, step2, (init_ref[:, cre], init_ref[:, cim]), unroll=4)

    y_win = _dot(h_ref[...].astype(BF16), cblk_ref[...])

    if final:
        tot = gathered(tot_ref) + y_win + dskip_ref[...] * gathered(lhs_ref)
        gl = _gelu_tanh(tot)
        gate = jax.nn.sigmoid(_dot(gl.astype(BF16), gw_ref[...]) + gb_ref[...])
        v = gl * gate
        ms = jnp.mean(v * v, axis=-1, keepdims=True)
        y_win = v * lax.rsqrt(ms + EPS) * og_ref[...]

    for k in range(n_lb):
        y_ref[k] = y_win[:, k * LANES:(k + 1) * LANES]
    for s in range(nsub):
        rs, cs = sub_block(s)
        for k in range(n_lb):
            o_ref[rs, lane_block(cs, k)] = y_ref[k, pl.ds(s, seg, stride=nsub), :].astype(o_ref.dtype)

    @pl.when(jw == pl.num_programs(1) - 1)
    def _():
        sout_ref[...] = carry_ref[...]


def _s5_scan(u2d, ops, ld, state_in, n_batch, lat_len, reverse, colmajor, final_args=None):
    ns2 = 2 * S5_NSTATE
    bblk, cblk, lam = ops
    if colmajor:
        tile = (GRID_W, (S5_WIN // GRID_W) * S5_DIM)
        n_win = GRID_W * S5_DIM // tile[1]
        out_rows, out_cols = n_batch * GRID_W, GRID_W * S5_DIM
        imap = (lambda b, j: (b, n_win - 1 - j)) if reverse else (lambda b, j: (b, j))
        in_map = imap
    else:
        tile = (S5_WIN, S5_DIM)
        n_win = 1
        out_rows, out_cols = n_batch * S5_WIN, S5_DIM
        ctx_blk0 = n_batch * lat_len // S5_WIN
        imap = lambda b, j: (b, 0)
        in_map = lambda b, j: (ctx_blk0 + b, 0)
    st = pl.BlockSpec((None, 1, ns2), lambda b, j: (b, 0, 0))
    stacked = lambda a: pl.BlockSpec((None,) + a.shape[1:], lambda b, j: (ld, 0, 0))
    final = final_args is not None
    in_specs = [pl.BlockSpec(tile, in_map)]
    args = [u2d]
    if final:
        in_specs.append(pl.BlockSpec(tile, imap))
        args.append(final_args[0])
    in_specs += [stacked(bblk), stacked(cblk), stacked(lam), st]
    args += [bblk, cblk, lam, state_in]
    win3 = pltpu.VMEM((S5_DIM // LANES, S5_WIN, LANES), F32)
    scratch = [win3, pltpu.VMEM((S5_WIN, ns2), F32),
               pltpu.VMEM((SUBLANES, ns2), F32), pltpu.VMEM((SUBLANES, ns2), F32),
               pltpu.VMEM((1, ns2), F32), win3]
    if final:
        _, l, d_skip, glu_w_bf, glu_b, out_g = final_args
        for a in (d_skip,):
            a3, sp = _layer_rows(a, l)
            in_specs.append(sp)
            args.append(a3)
        gw, gw_spec = _layer_mat(glu_w_bf, l)
        in_specs.append(gw_spec)
        args.append(gw)
        for a in (glu_b, out_g):
            a3, sp = _layer_rows(a, l)
            in_specs.append(sp)
            args.append(a3)
        scratch.append(win3)
    out, s_out = pl.pallas_call(
        functools.partial(_s5_kernel, reverse=reverse, final=final, colmajor=colmajor),
        out_shape=(jax.ShapeDtypeStruct((out_rows, out_cols), BF16 if final else F32),
                   jax.ShapeDtypeStruct((n_batch, 1, ns2), F32)),
        grid=(n_batch, n_win),
        in_specs=in_specs,
        out_specs=(pl.BlockSpec(tile, imap), st),
        scratch_shapes=scratch,
        compiler_params=_cparams(("arbitrary", "arbitrary")),
        name="s5_" + ("rev" if reverse else "fwd") + ("_lat" if colmajor else "_ctx"),
    )(*args)
    return out, s_out


def _s5_operands(lam_re, lam_im, log_step, b_re, b_im, c_re, c_im):
    g, p, k = S5_GROUPS, S5_STATE, S5_GROUP
    lam = lax.complex(jnp.minimum(lam_re.astype(F32), -1e-4), lam_im.astype(F32))
    step = jnp.exp(log_step.astype(F32))[:, None]
    lam_bar = jnp.exp(lam * step)
    lam_seg = jnp.exp(lam * (step * S5_SEG))
    b_bar = ((lam_bar - 1.0) / lam)[..., None] * lax.complex(b_re.astype(F32), b_im.astype(F32))
    eye = jnp.eye(g, dtype=F32)
    bd_in = lambda m: jnp.einsum('gkp,gh->gkhp', jnp.transpose(m, (0, 2, 1)), eye).reshape(g * k, g * p)
    bblk = jnp.concatenate([bd_in(jnp.real(b_bar)), bd_in(jnp.imag(b_bar))], axis=1)
    bd_out = lambda m: jnp.einsum('gpk,gh->gphk', jnp.transpose(m, (0, 2, 1)), eye).reshape(g * p, g * k)
    cblk = jnp.concatenate([bd_out(c_re.astype(F32)), -bd_out(c_im.astype(F32))], axis=0)
    lam_rows = jnp.stack([jnp.real(lam_bar).reshape(-1), jnp.imag(lam_bar).reshape(-1),
                          jnp.real(lam_seg).reshape(-1), jnp.imag(lam_seg).reshape(-1)])
    return bblk.astype(BF16), cblk.astype(BF16), lam_rows


def _pack_bf16_pairs(v):
    n = v.shape[1] // 2
    bits = pltpu.bitcast(v.astype(BF16).astype(F32), U32)
    return (bits[:, :n] >> 16) | (bits[:, n:] & jnp.uint32(0xFFFF0000))


def _unpack_bf16_pairs(w):
    lo = pltpu.bitcast(w << 16, F32)
    hi = pltpu.bitcast(w & jnp.uint32(0xFFFF0000), F32)
    return lo, hi


def _mixout_kernel(r_ref, a_ref, b_ref, cl_ref, cc_ref, wa_ref, wb_ref, wc_ref, g1_ref, ng_ref, sh_ref,
                   sc_ref, rw_ref, rb_ref, r1_ref, hp_ref, idx_ref, gate_ref, rank_ref, cnt_ref,
                   carry_ref, *, n_lat_tiles):
    i = pl.program_id(0)

    @pl.when(i == 0)
    def _():
        carry_ref[...] = jnp.zeros_like(carry_ref)

    mix_c = jnp.where(i < n_lat_tiles, cl_ref[...], cc_ref[...])
    acc = _dot(a_ref[...], wa_ref[...]) + _dot(b_ref[...], wb_ref[...]) + _dot(mix_c, wc_ref[...])
    x = r_ref[...] + g1_ref[...] * acc
    r1_ref[...] = x
    ms = jnp.mean(x * x, axis=-1, keepdims=True)
    h = x * lax.rsqrt(ms + EPS) * ng_ref[...]
    h = h * (1.0 + sc_ref[...]) + sh_ref[...]
    hp_ref[...] = _pack_bf16_pairs(h)

    h_hi = h.astype(BF16)
    h_lo = (h - h_hi.astype(F32)).astype(BF16)
    rw = rw_ref[...]
    w_hi = rw.astype(BF16)
    w_lo = (rw - w_hi.astype(F32)).astype(BF16)
    logits = _dot(h_hi, w_hi) + _dot(h_lo, w_hi) + _dot(h_hi, w_lo) + rb_ref[...]

    tm = logits.shape[0]
    lane = lax.broadcasted_iota(I32, (tm, LANES), 1)
    lane_f = lane.astype(F32)
    work = logits
    tops, picks = [], []
    for _ in range(TOP_K):
        m = jnp.max(work, axis=-1, keepdims=True)
        pick = jnp.min(jnp.where(work == m, lane_f, float(LANES)), axis=-1, keepdims=True)
        work = jnp.where(lane_f == pick, -jnp.inf, work)
        tops.append(m)
        picks.append(pick)
    exps = [jnp.exp(t - tops[0]) for t in tops]
    denom = exps[0] + exps[1] + exps[2] + exps[3]

    onehot = jnp.zeros((tm, LANES), F32)
    for k in range(TOP_K):
        onehot = onehot + jnp.where(lane_f == picks[k], 1.0, 0.0)
    ri = lax.broadcasted_iota(I32, (tm, tm), 0)
    ci = lax.broadcasted_iota(I32, (tm, tm), 1)
    before = jnp.where(ci < ri, 1.0, 0.0).astype(BF16)
    base = carry_ref[0:1, :] + _dot(before, onehot.astype(BF16))
    carry_ref[...] = carry_ref[...] + jnp.sum(onehot, axis=0, keepdims=True)
    cnt_ref[...] = carry_ref[...]

    idx_out = jnp.zeros((tm, LANES), F32)
    gate_out = jnp.zeros((tm, LANES), F32)
    rank_out = jnp.zeros((tm, LANES), F32)
    for k in range(TOP_K):
        rank_k = jnp.sum(jnp.where(lane_f == picks[k], base, 0.0), axis=-1, keepdims=True)
        idx_out = jnp.where(lane == k, picks[k], idx_out)
        gate_out = jnp.where(lane == k, exps[k] / denom, gate_out)
        rank_out = jnp.where(lane == k, rank_k, rank_out)
    idx_ref[...] = idx_out.astype(I32)
    gate_ref[...] = gate_out
    rank_ref[...] = rank_out.astype(I32)


def _mix_out(r, mix_a, mix_b, mix_c_lat, mix_c_ctx, w_out_bf, mods4, norm2_g, router_w_pad, router_b_pad,
             l, lat_len, n_batch):
    nt, d = r.shape
    tm = ROW_TILE
    n_lat_tiles = n_batch * lat_len // tm

    def grp(i):
        return jnp.minimum((i * tm) // lat_len, n_batch)

    ng3, ng_spec = _layer_rows(norm2_g, l)
    rb3, rb_spec = _layer_rows(router_b_pad, l)
    tile = lambda w: pl.BlockSpec((tm, w), lambda i: (i, 0))
    return pl.pallas_call(
        functools.partial(_mixout_kernel, n_lat_tiles=n_lat_tiles),
        out_shape=(jax.ShapeDtypeStruct((nt, d), F32), jax.ShapeDtypeStruct((nt, d // 2), U32),
                   jax.ShapeDtypeStruct((nt, LANES), I32), jax.ShapeDtypeStruct((nt, LANES), F32),
                   jax.ShapeDtypeStruct((nt, LANES), I32), jax.ShapeDtypeStruct((SUBLANES, LANES), F32)),
        grid=(nt // tm,),
        in_specs=[tile(d), tile(SSD_INNER), tile(CONF_DIM),
                  pl.BlockSpec((tm, S5_DIM), lambda i: (jnp.minimum(i, n_lat_tiles - 1), 0)),
                  pl.BlockSpec((tm, S5_DIM), lambda i: (jnp.maximum(i - n_lat_tiles, 0), 0)),
                  pl.BlockSpec((None, SSD_INNER, d), lambda i: (l, 0, 0)),
                  pl.BlockSpec((None, CONF_DIM, d), lambda i: (l, SSD_INNER // CONF_DIM, 0)),
                  pl.BlockSpec((None, S5_DIM, d), lambda i: (l, (SSD_INNER + CONF_DIM) // S5_DIM, 0)),
                  _mod_spec(l, grp, 2, d), ng_spec, _mod_spec(l, grp, 3, d), _mod_spec(l, grp, 4, d),
                  pl.BlockSpec((None, d, LANES), lambda i: (l, 0, 0)), rb_spec],
        out_specs=(tile(d), tile(d // 2), tile(LANES), tile(LANES), tile(LANES),
                   pl.BlockSpec((SUBLANES, LANES), lambda i: (0, 0))),
        scratch_shapes=[pltpu.VMEM((SUBLANES, LANES), F32)],
        compiler_params=_cparams(("arbitrary",)),
        name="mix_out_router",
    )(r, mix_a, mix_b, mix_c_lat, mix_c_ctx, w_out_bf, w_out_bf, w_out_bf, mods4, ng3, mods4, mods4,
      router_w_pad, rb3)


def _expert_kernel(be_ref, nv_ref, first_ref, nxt_ref, ra_cur_ref, ra_nxt_ref, h_hbm, wgu_hbm, bgu_ref,
                   wd_hbm, bd_ref, ya_hbm, xbuf, ybuf, wgu_st, wd_st, wgu_bf, wd_bf, gsem, ssem, wsem,
                   *, layer, n_tok):
    i = pl.program_id(0)
    n_steps = pl.num_programs(0)
    slot = i % 2
    nv = nv_ref[i]
    blk = MOE_BLOCK

    def weight_copies(e):
        return (pltpu.make_async_copy(wgu_hbm.at[layer, e], wgu_st, wsem.at[0]),
                pltpu.make_async_copy(wd_hbm.at[layer, e], wd_st, wsem.at[1]))

    def gather_rows(idx_ref, dst_slot):
        def body(rr, carry):
            tok = jnp.minimum(idx_ref[0, rr] >> 2, n_tok - 1)
            pltpu.make_async_copy(h_hbm.at[pl.ds(tok, 1)], xbuf.at[dst_slot, pl.ds(rr, 1)],
                                  gsem.at[dst_slot]).start()
            return carry
        lax.fori_loop(0, blk, body, 0, unroll=8)

    def whole_block_wait(src, dst, sem):
        pltpu.make_async_copy(src, dst, sem).wait()

    @pl.when(i == 0)
    def _():
        for cp in weight_copies(be_ref[0]):
            cp.start()
        gather_rows(ra_cur_ref, 0)

    @pl.when(i >= 2)
    def _():
        whole_block_wait(ybuf.at[slot], ya_hbm.at[pl.ds(0, blk)], ssem.at[slot])

    @pl.when(nv > 0)
    def _():
        @pl.when(first_ref[i] == 1)
        def _():
            for cp in weight_copies(be_ref[i]):
                cp.wait()
            wgu_bf[...] = wgu_st[...].astype(BF16)
            wd_bf[...] = wd_st[...].astype(BF16)

            @pl.when(nxt_ref[i] >= 0)
            def _():
                for cp in weight_copies(nxt_ref[i]):
                    cp.start()

        whole_block_wait(h_hbm.at[pl.ds(0, blk)], xbuf.at[slot], gsem.at[slot])

        @pl.when(jnp.logical_and(i + 1 < n_steps, nv_ref[jnp.minimum(i + 1, n_steps - 1)] > 0))
        def _():
            gather_rows(ra_nxt_ref, 1 - slot)

        half = D_MODEL // 2
        lo, hi = _unpack_bf16_pairs(xbuf[slot])
        gu = (_dot(lo.astype(BF16), wgu_bf[0:half, :]) + _dot(hi.astype(BF16), wgu_bf[half:D_MODEL, :])
              + bgu_ref[...])
        gate = jnp.minimum(gu[:, :D_EXPERT], SWIGLU_LIMIT)
        lin = jnp.clip(gu[:, D_EXPERT:], -SWIGLU_LIMIT, SWIGLU_LIMIT)
        act = gate * jax.nn.sigmoid(SWIGLU_ALPHA * gate) * (lin + 1.0)
        y = _dot(act.astype(BF16), wd_bf[...]) + bd_ref[...]
        ybuf[slot] = _pack_bf16_pairs(y)

        def body(rr, carry):
            pltpu.make_async_copy(ybuf.at[slot, pl.ds(rr, 1)], ya_hbm.at[pl.ds(ra_cur_ref[0, rr], 1)],
                                  ssem.at[slot]).start()
            return carry
        lax.fori_loop(0, blk, body, 0, unroll=8)

    @pl.when(nv == 0)
    def _():
        ybuf[slot] = jnp.zeros((blk, D_MODEL // 2), U32)
        pltpu.make_async_copy(ybuf.at[slot], ya_hbm.at[pl.ds(pl.multiple_of(i * blk, blk), blk)],
                              ssem.at[slot]).start()

    @pl.when(i == n_steps - 1)
    def _():
        whole_block_wait(ybuf.at[slot], ya_hbm.at[pl.ds(0, blk)], ssem.at[slot])

        @pl.when(i >= 1)
        def _():
            whole_block_wait(ybuf.at[1 - slot], ya_hbm.at[pl.ds(0, blk)], ssem.at[1 - slot])


def _experts(h_packed, row_assign, block_e, n_valid, first, nxt, w_gu, b_gu, w_down, b_down, layer):
    n_tok, width = h_packed.shape
    n_rows = row_assign.shape[0]
    n_blocks = n_rows // MOE_BLOCK
    depth, ne = w_gu.shape[:2]
    ra3 = row_assign.reshape(n_blocks, 1, MOE_BLOCK)
    grid_spec = pltpu.PrefetchScalarGridSpec(
        num_scalar_prefetch=4,
        grid=(n_blocks,),
        in_specs=[pl.BlockSpec((None, 1, MOE_BLOCK), lambda i, *_: (i, 0, 0), memory_space=pltpu.SMEM),
                  pl.BlockSpec((None, 1, MOE_BLOCK), lambda i, *_: (jnp.minimum(i + 1, n_blocks - 1), 0, 0),
                               memory_space=pltpu.SMEM),
                  pl.BlockSpec(memory_space=pl.ANY),
                  pl.BlockSpec(memory_space=pl.ANY),
                  pl.BlockSpec((None, None, 1, 2 * D_EXPERT), lambda i, be, *_: (layer, be[i], 0, 0)),
                  pl.BlockSpec(memory_space=pl.ANY),
                  pl.BlockSpec((None, None, 1, D_MODEL), lambda i, be, *_: (layer, be[i], 0, 0))],
        out_specs=pl.BlockSpec(memory_space=pl.ANY),
        scratch_shapes=[pltpu.VMEM((2, MOE_BLOCK, width), U32), pltpu.VMEM((2, MOE_BLOCK, width), U32),
                        pltpu.VMEM((D_MODEL, 2 * D_EXPERT), F32), pltpu.VMEM((D_EXPERT, D_MODEL), F32),
                        pltpu.VMEM((D_MODEL, 2 * D_EXPERT), BF16), pltpu.VMEM((D_EXPERT, D_MODEL), BF16),
                        pltpu.SemaphoreType.DMA((2,)), pltpu.SemaphoreType.DMA((2,)),
                        pltpu.SemaphoreType.DMA((2,))],
    )
    return pl.pallas_call(
        functools.partial(_expert_kernel, layer=layer, n_tok=n_tok),
        out_shape=jax.ShapeDtypeStruct((n_rows, width), U32),
        grid_spec=grid_spec,
        compiler_params=_cparams(("arbitrary",)),
        name="moe_experts",
    )(block_e, n_valid, first, nxt, ra3, ra3, h_packed, w_gu, b_gu.reshape(depth, ne, 1, -1), w_down,
      b_down.reshape(depth, ne, 1, -1))


def _combine_kernel(r_ref, gate_ref, ya_ref, g2_ref, fg_ref, o_ref, *, last_layer):
    tm = r_ref.shape[0]
    half = D_MODEL // 2
    acc_lo = jnp.zeros((tm, half), F32)
    acc_hi = jnp.zeros((tm, half), F32)
    for k in range(TOP_K):
        lo, hi = _unpack_bf16_pairs(ya_ref[:, k * half:(k + 1) * half])
        gk = gate_ref[:, k:k + 1]
        acc_lo = acc_lo + gk * lo
        acc_hi = acc_hi + gk * hi
    o_ref[:, 0:half] = r_ref[:, 0:half] + g2_ref[:, 0:half] * acc_lo
    o_ref[:, half:D_MODEL] = r_ref[:, half:D_MODEL] + g2_ref[:, half:D_MODEL] * acc_hi
    if last_layer:
        x = o_ref[...]
        ms = jnp.mean(x * x, axis=-1, keepdims=True)
        o_ref[...] = x * lax.rsqrt(ms + EPS) * fg_ref[...]


def _combine(r1, gates, ya, mods4, final_g, l, lat_len, n_batch, last_layer):
    nt, d = r1.shape
    tm = ROW_TILE
    n_out = n_batch * lat_len if last_layer else nt
    ya4 = ya.reshape(ya.shape[0] // TOP_K, TOP_K * ya.shape[1])

    def grp(i):
        return jnp.minimum((i * tm) // lat_len, n_batch)

    return pl.pallas_call(
        functools.partial(_combine_kernel, last_layer=last_layer),
        out_shape=jax.ShapeDtypeStruct((n_out, d), F32),
        grid=(n_out // tm,),
        in_specs=[pl.BlockSpec((tm, d), lambda i: (i, 0)),
                  pl.BlockSpec((tm, LANES), lambda i: (i, 0)),
                  pl.BlockSpec((tm, TOP_K * (d // 2)), lambda i: (i, 0)),
                  _mod_spec(l, grp, 5, d),
                  pl.BlockSpec((1, d), lambda i: (0, 0))],
        out_specs=pl.BlockSpec((tm, d), lambda i: (i, 0)),
        compiler_params=_cparams(("arbitrary",)),
        name="moe_combine",
    )(r1, gates, ya4, mods4, final_g.reshape(1, d))


def _routing_plan(top_idx, rank, counts, n_blocks):
    n_assign = top_idx.size
    n_rows = n_blocks * MOE_BLOCK
    flat_e = top_idx.reshape(-1)
    counts = counts.astype(I32)
    padded = (counts + MOE_BLOCK - 1) // MOE_BLOCK * MOE_BLOCK
    pad_end = jnp.cumsum(padded)
    pad_start = pad_end - padded
    dest = pad_start[flat_e] + rank.reshape(-1)
    row_assign = jnp.full((n_rows,), -1, I32).at[dest].set(jnp.arange(n_assign, dtype=I32),
                                                           unique_indices=True, mode='drop')
    empty = row_assign < 0
    row_assign = jnp.where(empty, n_assign + jnp.cumsum(empty.astype(I32)) - 1, row_assign)
    blk_start = jnp.arange(n_blocks, dtype=I32) * MOE_BLOCK
    block_e = jnp.minimum(jnp.searchsorted(pad_end, blk_start, side='right'), N_EXPERTS - 1).astype(I32)
    used = blk_start < pad_end[-1]
    n_valid = jnp.where(used, jnp.clip(pad_start[block_e] + counts[block_e] - blk_start, 0, MOE_BLOCK), 0)
    first = jnp.logical_and(used, blk_start == pad_start[block_e]).astype(I32)
    nxt_start = pad_end[block_e]
    nxt_e = jnp.minimum(jnp.searchsorted(pad_end, nxt_start, side='right'), N_EXPERTS - 1).astype(I32)
    nxt = jnp.where(nxt_start < pad_end[-1], nxt_e, -1).astype(I32)
    return row_assign, block_e, n_valid.astype(I32), first, nxt


def _forward(x, c, ctx, c_ctx, ada_w, ada_b, norm1_g, w_in, ssd_conv_w, ssd_conv_b, ssd_a_log,
             ssd_dt_bias, ssd_d, ssd_norm_g, conf_dw_w, conf_dw_b, conf_ln_g, conf_ln_b, conf_pw_w,
             conf_pw_b, conf_out_g, s5_lam_re, s5_lam_im, s5_log_step, s5_b_re, s5_b_im, s5_c_re,
             s5_c_im, s5_d, s5_glu_w, s5_glu_b, s5_out_g, w_out, norm2_g, router_w, router_b,
             w_gate_up, b_gate_up, w_down, b_down, final_norm_g):
    n_batch, lat_len, d = x.shape
    ctx_len = ctx.shape[1]
    depth = ada_w.shape[0]
    n_lat = n_batch * lat_len
    nt = n_lat + n_batch * ctx_len
    assert d == D_MODEL and ctx_len == ROW_TILE and lat_len % MM_TILE_M == 0
    assert lat_len // GRID_W == GRID_W

    r = jnp.concatenate([x.reshape(n_lat, d), ctx.reshape(n_batch * ctx_len, d)], axis=0).astype(F32)
    cond = jnp.zeros((SUBLANES, d), F32).at[:n_batch].set(c).at[n_batch].set(c_ctx)
    mods4 = _modulation(cond, ada_w, ada_b).reshape(depth, SUBLANES, 1, 6 * d)

    c_dt = SSD_INNER + SSD_CONV_DIM
    c_conf = c_dt + SSD_HEADS
    c_s5 = c_conf + 2 * CONF_DIM
    w_main = jnp.concatenate([w_in[:, :, :c_dt], w_in[:, :, c_conf:c_s5]], axis=2).astype(BF16)
    w_side = jnp.concatenate([w_in[:, :, c_s5:], w_in[:, :, c_dt:c_conf],
                              jnp.zeros((depth, d, LANES - SSD_HEADS), w_in.dtype)], axis=2).astype(BF16)
    w_out_bf = w_out.astype(BF16)
    conf_pw_bf = conf_pw_w.astype(BF16)
    s5_glu_bf = s5_glu_w.astype(BF16)
    head_pad = lambda a: jnp.pad(a.astype(F32), ((0, 0), (0, 0), (0, LANES - SSD_HEADS))).reshape(
        depth * 2, 1, LANES)
    dtb_all, alog_all = head_pad(ssd_dt_bias), head_pad(ssd_a_log)
    d_skip_all = jnp.repeat(ssd_d.astype(F32), SSD_HEAD_DIM, axis=1)
    router_w_pad = jnp.pad(router_w.astype(F32), ((0, 0), (0, 0), (0, LANES - N_EXPERTS)))
    router_b_pad = jnp.pad(router_b.astype(F32), ((0, 0), (0, LANES - N_EXPERTS)), constant_values=-1e30)
    flat2 = lambda a: a.reshape((depth * 2,) + a.shape[2:])
    s5_ops = jax.vmap(_s5_operands)(*[flat2(a) for a in (s5_lam_re, s5_lam_im, s5_log_step, s5_b_re,
                                                         s5_b_im, s5_c_re, s5_c_im)])

    n_blocks = nt * TOP_K // MOE_BLOCK + N_EXPERTS
    s5_zero = jnp.zeros((n_batch, 1, 2 * S5_NSTATE), F32)

    for l in range(depth):
        p_main = _norm_matmul(r, norm1_g, mods4, l, w_main, lat_len, n_batch, "in_proj_main")
        p_s5, p_dt = _norm_matmul(r, norm1_g, mods4, l, w_side, lat_len, n_batch, "in_proj_side",
                                  splits=(S5_DIM, LANES))

        xbc = _ssd_conv(p_main, ssd_conv_w, ssd_conv_b, l, lat_len, ctx_len, n_batch)
        y_fwd = _ssd_scan(xbc, p_main, p_dt, dtb_all, alog_all, l, lat_len, ctx_len, n_batch, False)
        mix_a = _ssd_scan(xbc, p_main, p_dt, dtb_all, alog_all, l, lat_len, ctx_len, n_batch, True,
                          (y_fwd, d_skip_all, ssd_norm_g))

        mix_b = _conformer(p_main, conf_dw_w, conf_dw_b, conf_ln_g, conf_ln_b, conf_pw_bf, conf_pw_b,
                           conf_out_g, l, lat_len, ctx_len, n_batch)

        u_cm = p_s5.reshape(nt // GRID_W, GRID_W * S5_DIM)
        yc_f, st_f = _s5_scan(p_s5, s5_ops, 2 * l, s5_zero, n_batch, lat_len, False, False)
        yl_f, _ = _s5_scan(u_cm, s5_ops, 2 * l, st_f, n_batch, lat_len, False, True)
        fin = (l, s5_d, s5_glu_bf, s5_glu_b, s5_out_g)
        mc_c, st_r = _s5_scan(p_s5, s5_ops, 2 * l + 1, s5_zero, n_batch, lat_len, True, False, (yc_f,) + fin)
        mc_l, _ = _s5_scan(u_cm, s5_ops, 2 * l + 1, st_r, n_batch, lat_len, True, True, (yl_f,) + fin)

        r1, h_packed, top_idx, gates, rank, counts = _mix_out(
            r, mix_a, mix_b, mc_l.reshape(n_lat, S5_DIM), mc_c, w_out_bf, mods4, norm2_g, router_w_pad,
            router_b_pad, l, lat_len, n_batch)

        row_assign, block_e, n_valid, first, nxt = _routing_plan(
            top_idx[:, :TOP_K], rank[:, :TOP_K], counts[0, :N_EXPERTS], n_blocks)
        ya = _experts(h_packed, row_assign, block_e, n_valid, first, nxt, w_gate_up, b_gate_up, w_down,
                      b_down, l)
        r = _combine(r1, gates, ya, mods4, final_norm_g, l, lat_len, n_batch, l == depth - 1)

    return r.reshape(n_batch, lat_len, d).astype(x.dtype)


def kernel(x, c, ctx, c_ctx, ada_w, ada_b, norm1_g, w_in, ssd_conv_w, ssd_conv_b, ssd_a_log, ssd_dt_bias,
           ssd_d, ssd_norm_g, conf_dw_w, conf_dw_b, conf_ln_g, conf_ln_b, conf_pw_w, conf_pw_b, conf_out_g,
           s5_lam_re, s5_lam_im, s5_log_step, s5_b_re, s5_b_im, s5_c_re, s5_c_im, s5_d, s5_glu_w, s5_glu_b,
           s5_out_g, w_out, norm2_g, router_w, router_b, w_gate_up, b_gate_up, w_down, b_down, final_norm_g):
    return _forward(x, c, ctx, c_ctx, ada_w, ada_b, norm1_g, w_in, ssd_conv_w, ssd_conv_b, ssd_a_log,
                    ssd_dt_bias, ssd_d, ssd_norm_g, conf_dw_w, conf_dw_b, conf_ln_g, conf_ln_b, conf_pw_w,
                    conf_pw_b, conf_out_g, s5_lam_re, s5_lam_im, s5_log_step, s5_b_re, s5_b_im, s5_c_re,
                    s5_c_im, s5_d, s5_glu_w, s5_glu_b, s5_out_g, w_out, norm2_g, router_w, router_b,
                    w_gate_up, b_gate_up, w_down, b_down, final_norm_g)
```

```python
import functools
import math

import jax
import jax.numpy as jnp
from jax import lax
from jax.experimental import pallas as pl
from jax.experimental.pallas import tpu as pltpu

F32 = jnp.float32
BF16 = jnp.bfloat16
I32 = jnp.int32
U32 = jnp.uint32

D_MODEL = 2048
GRID_W = 64
SSD_INNER = 1024
SSD_HEAD_DIM = 64
SSD_HEADS = 16
SSD_GROUPS = 4
SSD_STATE = 128
SSD_CONV = 5
SSD_CHUNK = 128
SSD_CONV_DIM = SSD_INNER + 2 * SSD_GROUPS * SSD_STATE
CONF_DIM = 512
CONF_KERNEL = 31
S5_DIM = 512
S5_GROUP = 16
S5_GROUPS = 32
S5_STATE = 64
S5_NSTATE = S5_GROUPS * S5_STATE
N_EXPERTS = 32
TOP_K = 4
D_EXPERT = 768
SWIGLU_LIMIT = 7.0
SWIGLU_ALPHA = 1.702
MOE_BLOCK = 256
EPS = 1e-6

LANES = 128
SUBLANES = 8
ROW_TILE = 256
MM_TILE_M = 512
MM_TILE_N = 512
MAIN_COLS = 4096
COL_Z, COL_X, COL_CONF = 0, 1024, 3072
SIDE_COLS = S5_DIM + LANES
CONV_HALO = 16
CONF_HALO = 16
S5_SEG = 32
S5_WIN = SUBLANES * S5_SEG
VMEM_LIMIT = 56 * 1024 * 1024


def _cparams(sem, vmem=VMEM_LIMIT):
    return pltpu.CompilerParams(dimension_semantics=sem, vmem_limit_bytes=vmem)


def _silu(v):
    return v * jax.nn.sigmoid(v)


def _split3(v):
    hi = v.astype(BF16)
    r1 = v - hi.astype(F32)
    mid = r1.astype(BF16)
    lo = (r1 - mid.astype(F32)).astype(BF16)
    return hi, mid, lo


def _dot(a, b):
    return jnp.dot(a, b, preferred_element_type=F32)


def _layer_rows(arr, l):
    depth = arr.shape[0]
    a3 = arr.reshape(depth, 1, -1)
    return a3, pl.BlockSpec((None, 1, a3.shape[2]), lambda *_: (l, 0, 0))


def _layer_mat(arr, l):
    return arr, pl.BlockSpec((None,) + arr.shape[1:], lambda *_: (l, 0, 0))


def _mod_kernel(c_ref, w_ref, b_ref, o_ref):
    s = _silu(c_ref[...])
    w = w_ref[...]
    s_hi = s.astype(BF16)
    s_lo = (s - s_hi.astype(F32)).astype(BF16)
    w_hi = w.astype(BF16)
    w_lo = (w - w_hi.astype(F32)).astype(BF16)
    acc = _dot(s_hi, w_hi) + _dot(s_lo, w_hi) + _dot(s_hi, w_lo)
    o_ref[...] = acc + b_ref[...]


def _modulation(cond, ada_w, ada_b):
    depth, d, n = ada_w.shape
    tn = 1024
    return pl.pallas_call(
        _mod_kernel,
        out_shape=jax.ShapeDtypeStruct((depth, SUBLANES, n), F32),
        grid=(depth, n // tn),
        in_specs=[pl.BlockSpec((SUBLANES, d), lambda l, j: (0, 0)),
                  pl.BlockSpec((None, d, tn), lambda l, j: (l, 0, j)),
                  pl.BlockSpec((None, 1, tn), lambda l, j: (l, 0, j))],
        out_specs=pl.BlockSpec((None, SUBLANES, tn), lambda l, j: (l, 0, j)),
        compiler_params=_cparams(("arbitrary", "arbitrary")),
        name="adaln_mod",
    )(cond, ada_w, ada_b.reshape(depth, 1, n))


def _mod_spec(l, grp, k, d):
    return pl.BlockSpec((None, None, 1, d), lambda i, *_: (l, grp(i), 0, k))


def _normmm_kernel(x_ref, g_ref, sh_ref, sc_ref, w_ref, *rest, n_tiles):
    o_refs, hn_ref = rest[:-1], rest[-1]
    s = pl.program_id(2)
    live = 2 * pl.program_id(0) + s < n_tiles

    @pl.when(jnp.logical_and(pl.program_id(1) == 0, live))
    def _():
        x = x_ref[...]
        ms = jnp.mean(x * x, axis=-1, keepdims=True)
        y = x * lax.rsqrt(ms + EPS) * g_ref[...]
        hn_ref[s] = (y * (1.0 + sc_ref[...]) + sh_ref[...]).astype(BF16)

    @pl.when(live)
    def _():
        res = _dot(hn_ref[s], w_ref[...])
        col = 0
        for o_ref in o_refs:
            o_ref[...] = res[:, col:col + o_ref.shape[1]].astype(o_ref.dtype)
            col += o_ref.shape[1]


def _norm_matmul(r, gains, mods4, l, w_all, lat_len, n_batch, name, out_dtypes, splits=None):
    nt, d = r.shape
    n = w_all.shape[2]
    tm = MM_TILE_M if nt % MM_TILE_M == 0 else ROW_TILE
    tn = n if splits else MM_TILE_N
    assert n % tn == 0
    widths = splits or (tn,)
    n_tiles = nt // tm
    last = n_tiles - 1

    def tile(p, j, s):
        return jnp.minimum(2 * p + s, last)

    def x_tile(p, j, s):
        return jnp.where(j == 0, tile(p, j, s), jnp.minimum(2 * p + 1, last))

    def grp(t):
        return jnp.minimum((t * tm) // lat_len, n_batch)

    def mod_spec(k):
        return pl.BlockSpec((None, None, 1, d), lambda p, j, s: (l, grp(tile(p, j, s)), 0, k))

    g3, g_spec = _layer_rows(gains, l)
    outs = pl.pallas_call(
        functools.partial(_normmm_kernel, n_tiles=n_tiles),
        out_shape=tuple(jax.ShapeDtypeStruct((nt, n if not splits else wd), dt)
                        for wd, dt in zip(widths, out_dtypes)),
        grid=(pl.cdiv(n_tiles, 2), n // tn, 2),
        in_specs=[pl.BlockSpec((tm, d), lambda p, j, s: (x_tile(p, j, s), 0)),
                  g_spec, mod_spec(0), mod_spec(1),
                  pl.BlockSpec((None, d, tn), lambda p, j, s: (l, 0, j))],
        out_specs=tuple(pl.BlockSpec((tm, wd), lambda p, j, s: (tile(p, j, s), j)) for wd in widths),
        scratch_shapes=[pltpu.VMEM((2, tm, d), BF16)],
        compiler_params=_cparams(("arbitrary", "arbitrary", "arbitrary")),
        name=name,
    )(r, g3, mods4, mods4, w_all)
    return outs if splits else outs[0]


def _seq_edges(i, tiles_per_lat_seq, n_lat_tiles):
    is_lat = i < n_lat_tiles
    first = jnp.logical_or(jnp.logical_not(is_lat), (i % tiles_per_lat_seq) == 0)
    last = jnp.logical_or(jnp.logical_not(is_lat), (i % tiles_per_lat_seq) == tiles_per_lat_seq - 1)
    return first, last


def _conv5_kernel(cur_ref, prev_ref, next_ref, w_ref, b_ref, o_ref, ext_ref, *, tiles_per_lat_seq,
                  n_lat_tiles):
    i = pl.program_id(0)
    first, last = _seq_edges(i, tiles_per_lat_seq, n_lat_tiles)
    h, tm = CONV_HALO, ROW_TILE
    ext_ref[0:h, :] = jnp.where(first, 0.0, prev_ref[...].astype(F32))
    ext_ref[h:h + tm, :] = cur_ref[...].astype(F32)
    ext_ref[h + tm:h + tm + h, :] = jnp.where(last, 0.0, next_ref[...].astype(F32))
    pad = (SSD_CONV - 1) // 2
    cw = 512
    for c in range(0, ext_ref.shape[1], cw):
        acc = jnp.broadcast_to(b_ref[:, c:c + cw], (tm, cw))
        for j in range(SSD_CONV):
            acc = acc + w_ref[j:j + 1, c:c + cw] * ext_ref[h - pad + j:h - pad + j + tm, c:c + cw]
        o_ref[:, c:c + cw] = _silu(acc).astype(BF16)


def _ssd_conv(p_main, conv_w, conv_b, l, lat_len, ctx_len, n_batch):
    nt = p_main.shape[0]
    tm, h = ROW_TILE, CONV_HALO
    assert ctx_len == tm and lat_len % tm == 0
    n_lat_tiles = n_batch * lat_len // tm
    cw = SSD_CONV_DIM // 2
    xblk = COL_X // cw
    nhb = nt // h
    depth = conv_w.shape[0]
    kern = functools.partial(_conv5_kernel, tiles_per_lat_seq=lat_len // tm, n_lat_tiles=n_lat_tiles)
    return pl.pallas_call(
        kern,
        out_shape=jax.ShapeDtypeStruct((nt, SSD_CONV_DIM), BF16),
        grid=(nt // tm, 2),
        in_specs=[pl.BlockSpec((tm, cw), lambda i, j: (i, xblk + j)),
                  pl.BlockSpec((h, cw), lambda i, j: (jnp.maximum(i * (tm // h) - 1, 0), xblk + j)),
                  pl.BlockSpec((h, cw), lambda i, j: (jnp.minimum((i + 1) * (tm // h), nhb - 1), xblk + j)),
                  pl.BlockSpec((None, SSD_CONV, cw), lambda i, j: (l, 0, j)),
                  pl.BlockSpec((None, 1, cw), lambda i, j: (l, 0, j))],
        out_specs=pl.BlockSpec((tm, cw), lambda i, j: (i, j)),
        scratch_shapes=[pltpu.VMEM((tm + 2 * h, cw), F32)],
        compiler_params=_cparams(("arbitrary", "arbitrary")),
        name="ssd_conv",
    )(p_main, p_main, p_main, conv_w, conv_b.reshape(depth, 1, -1))


def _head_cols(vals, width):
    lane = lax.broadcasted_iota(I32, (1, LANES), 1)
    halves = []
    per_half = LANES // width
    for hh in range(len(vals) // per_half):
        sel = vals[hh * per_half + per_half - 1]
        for k in range(per_half - 2, -1, -1):
            sel = jnp.where(lane < (k + 1) * width, vals[hh * per_half + k], sel)
        halves.append(sel)
    return jnp.concatenate(halves, axis=1)


def _ssd_kernel(*refs, reverse, final):
    if final:
        (x_ref, b_ref, c_ref, dt_ref, dtb_ref, alog_ref, yprev_ref, z_ref, dskip_ref, ng_ref,
         o_ref, state_ref, tmp_ref) = refs
    else:
        x_ref, b_ref, c_ref, dt_ref, dtb_ref, alog_ref, o_ref, state_ref = refs
    q = SSD_CHUNK
    r = SSD_HEADS // SSD_GROUPS
    gw = r * SSD_HEAD_DIM

    @pl.when(pl.program_id(1) == 0)
    def _():
        state_ref[...] = jnp.zeros_like(state_ref)

    lane = lax.broadcasted_iota(I32, (1, LANES), 1)
    dtp = jax.nn.softplus(dt_ref[...] + dtb_ref[...])
    a_head = -jnp.exp(alog_ref[...])
    a = jnp.where(lane < SSD_HEADS, dtp * a_head, 0.0)
    ri = lax.broadcasted_iota(I32, (q, q), 0)
    ci = lax.broadcasted_iota(I32, (q, q), 1)
    tri = (ci >= ri) if reverse else (ci <= ri)
    tri_b = jnp.where(tri, 1.0, 0.0).astype(BF16)
    a_hi, a_mid, a_lo = _split3(a)
    a_cs = _dot(tri_b, a_hi) + _dot(tri_b, a_mid) + _dot(tri_b, a_lo)
    a_cs_t = a_cs.T
    dtp_t = dtp.T
    a_end = a_cs[0:1, :] if reverse else a_cs[q - 1:q, :]
    lane_g = lax.broadcasted_iota(I32, (1, gw), 1)

    x = x_ref[...]
    for g in range(SSD_GROUPS):
        cg = c_ref[:, g * SSD_STATE:(g + 1) * SSD_STATE]
        bg = b_ref[:, g * SSD_STATE:(g + 1) * SSD_STATE]
        cb = lax.dot_general(cg, bg, (((1,), (1,)), ((), ())), preferred_element_type=F32)
        xg = x[:, g * gw:(g + 1) * gw]
        yg = jnp.zeros((q, gw), F32)
        ecols, wcols, dec = [], [], []
        for hl in range(r):
            h = g * r + hl
            col = a_cs[:, h:h + 1]
            row = a_cs_t[h:h + 1, :]
            lm = jnp.where(tri, jnp.exp(col - row), 0.0)
            m = (cb * lm * dtp_t[h:h + 1, :]).astype(BF16)
            in_head = jnp.logical_and(lane_g >= hl * SSD_HEAD_DIM, lane_g < (hl + 1) * SSD_HEAD_DIM)
            xm = jnp.where(in_head, xg, jnp.zeros_like(xg))
            yg = yg + _dot(m, xm)
            e_h = a_end[:, h:h + 1]
            ecols.append(jnp.exp(col))
            wcols.append(dtp[:, h:h + 1] * jnp.exp(e_h - col))
            dec.append(jnp.exp(e_h))
        s_old = state_ref[g]
        yg = yg + _head_cols(ecols, SSD_HEAD_DIM) * _dot(cg, s_old.astype(BF16))
        xw = (xg.astype(F32) * _head_cols(wcols, SSD_HEAD_DIM)).astype(BF16)
        upd = lax.dot_general(bg, xw, (((0,), (0,)), ((), ())), preferred_element_type=F32)
        state_ref[g] = _head_cols(dec, SSD_HEAD_DIM) * s_old + upd
        sl = slice(g * gw, (g + 1) * gw)
        if final:
            ytot = yprev_ref[:, sl] + yg + dskip_ref[:, sl] * xg.astype(F32)
            tmp_ref[:, sl] = ytot * _silu(z_ref[:, sl].astype(F32))
        else:
            o_ref[:, sl] = yg
    if final:
        gated = tmp_ref[...]
        ms = jnp.mean(gated * gated, axis=-1, keepdims=True)
        o_ref[...] = (gated * lax.rsqrt(ms + EPS) * ng_ref[...]).astype(o_ref.dtype)


def _ssd_scan(xbc, p_main, p_dt, dtb_all, alog_all, l, lat_len, ctx_len, n_batch, reverse, final_args=None):
    nt = xbc.shape[0]
    q = SSD_CHUNK
    ncl, ncc = lat_len // q, ctx_len // q
    ctx0 = n_batch * ncl
    dd = 1 if reverse else 0

    def blk(b, j):
        if reverse:
            return jnp.where(j < ncc, ctx0 + b * ncc + (ncc - 1 - j), b * ncl + (ncl - 1 - (j - ncc)))
        return jnp.where(j < ncc, ctx0 + b * ncc + j, b * ncl + (j - ncc))

    head_vec = pl.BlockSpec((None, 1, LANES), lambda b, j: (2 * l + dd, 0, 0))
    final = final_args is not None
    in_specs = [pl.BlockSpec((q, SSD_INNER), lambda b, j: (blk(b, j), 0)),
                pl.BlockSpec((q, SSD_GROUPS * SSD_STATE), lambda b, j: (blk(b, j), 2)),
                pl.BlockSpec((q, SSD_GROUPS * SSD_STATE), lambda b, j: (blk(b, j), 3)),
                pl.BlockSpec((q, LANES), lambda b, j: (blk(b, j), 0)),
                head_vec, head_vec]
    args = [xbc, xbc, xbc, p_dt, dtb_all, alog_all]
    scratch = [pltpu.VMEM((SSD_GROUPS, SSD_STATE, (SSD_HEADS // SSD_GROUPS) * SSD_HEAD_DIM), F32)]
    if final:
        y_prev, d_skip_all, norm_g_all = final_args
        ds3, ds_spec = _layer_rows(d_skip_all, l)
        ng3, ng_spec = _layer_rows(norm_g_all, l)
        in_specs += [pl.BlockSpec((q, SSD_INNER), lambda b, j: (blk(b, j), 0)),
                     pl.BlockSpec((q, SSD_INNER), lambda b, j: (blk(b, j), COL_Z // SSD_INNER)),
                     ds_spec, ng_spec]
        args += [y_prev, p_main, ds3, ng3]
        scratch.append(pltpu.VMEM((q, SSD_INNER), F32))
    return pl.pallas_call(
        functools.partial(_ssd_kernel, reverse=reverse, final=final),
        out_shape=jax.ShapeDtypeStruct((nt, SSD_INNER), BF16 if final else F32),
        grid=(n_batch, ncc + ncl),
        in_specs=in_specs,
        out_specs=pl.BlockSpec((q, SSD_INNER), lambda b, j: (blk(b, j), 0)),
        scratch_shapes=scratch,
        compiler_params=_cparams(("arbitrary", "arbitrary")),
        name="ssd_scan_rev" if reverse else "ssd_scan_fwd",
    )(*args)


def _glu(ref):
    v = ref[:, 0:CONF_DIM].astype(F32)
    gt = ref[:, CONF_DIM:2 * CONF_DIM].astype(F32)
    return v * jax.nn.sigmoid(gt)


def _conf_kernel(cur_ref, prev_ref, next_ref, dww_ref, dwb_ref, lng_ref, lnb_ref, pww_ref, pwb_ref,
                 og_ref, o_ref, ext_ref, acc_ref, *, tiles_per_lat_seq, n_lat_tiles):
    i = pl.program_id(0)
    first, last = _seq_edges(i, tiles_per_lat_seq, n_lat_tiles)
    h, tm = CONF_HALO, ROW_TILE
    ext_ref[0:h, :] = jnp.where(first, 0.0, _glu(prev_ref))
    ext_ref[h:h + tm, :] = _glu(cur_ref)
    ext_ref[h + tm:h + tm + h, :] = jnp.where(last, 0.0, _glu(next_ref))
    pad = (CONF_KERNEL - 1) // 2
    cw = 256
    for c in range(0, CONF_DIM, cw):
        acc = jnp.broadcast_to(dwb_ref[:, c:c + cw], (tm, cw))
        for j in range(CONF_KERNEL):
            acc = acc + dww_ref[j:j + 1, c:c + cw] * ext_ref[h - pad + j:h - pad + j + tm, c:c + cw]
        acc_ref[:, c:c + cw] = acc
    u = acc_ref[...]
    mu = jnp.mean(u, axis=-1, keepdims=True)
    var = jnp.mean(jnp.square(u - mu), axis=-1, keepdims=True)
    y = (u - mu) * lax.rsqrt(var + EPS) * lng_ref[...] + lnb_ref[...]
    y = _silu(y)
    v = _dot(y.astype(BF16), pww_ref[...]) + pwb_ref[...]
    ms = jnp.mean(v * v, axis=-1, keepdims=True)
    o_ref[...] = (v * lax.rsqrt(ms + EPS) * og_ref[...]).astype(o_ref.dtype)


def _conformer(p_main, dw_w, dw_b, ln_g, ln_b, pw_w_bf, pw_b, out_g, l, lat_len, ctx_len, n_batch):
    nt = p_main.shape[0]
    tm, h = ROW_TILE, CONF_HALO
    n_lat_tiles = n_batch * lat_len // tm
    cblk = COL_CONF // (2 * CONF_DIM)
    nhb = nt // h
    kern = functools.partial(_conf_kernel, tiles_per_lat_seq=lat_len // tm, n_lat_tiles=n_lat_tiles)
    rows = [_layer_rows(a, l) for a in (dw_b, ln_g, ln_b)]
    rows2 = [_layer_rows(a, l) for a in (pw_b, out_g)]
    dww, dww_spec = _layer_mat(dw_w, l)
    pww, pww_spec = _layer_mat(pw_w_bf, l)
    return pl.pallas_call(
        kern,
        out_shape=jax.ShapeDtypeStruct((nt, CONF_DIM), BF16),
        grid=(nt // tm,),
        in_specs=[pl.BlockSpec((tm, 2 * CONF_DIM), lambda i: (i, cblk)),
                  pl.BlockSpec((h, 2 * CONF_DIM), lambda i: (jnp.maximum(i * (tm // h) - 1, 0), cblk)),
                  pl.BlockSpec((h, 2 * CONF_DIM), lambda i: (jnp.minimum((i + 1) * (tm // h), nhb - 1), cblk)),
                  dww_spec] + [s for _, s in rows] + [pww_spec] + [s for _, s in rows2],
        out_specs=pl.BlockSpec((tm, CONF_DIM), lambda i: (i, 0)),
        scratch_shapes=[pltpu.VMEM((tm + 2 * h, CONF_DIM), F32), pltpu.VMEM((tm, CONF_DIM), F32)],
        compiler_params=_cparams(("arbitrary",)),
        name="conformer",
    )(p_main, p_main, p_main, dww, *[a for a, _ in rows], pww, *[a for a, _ in rows2])


def _gelu_tanh(v):
    return 0.5 * v * (1.0 + jnp.tanh(math.sqrt(2.0 / math.pi) * (v + 0.044715 * (v * v * v))))


def _s5_kernel(*refs, reverse, final, colmajor, seg):
    if final:
        (u_ref, yprev_ref, bblk_ref, cblk_ref, lam_ref, sin_ref, dskip_ref, gw_ref, gb_ref, og_ref,
         o_ref, sout_ref, h_ref, fin_ref, init_ref, carry_ref, *perm_refs) = refs
    else:
        (u_ref, bblk_ref, cblk_ref, lam_ref, sin_ref,
         o_ref, sout_ref, h_ref, fin_ref, init_ref, carry_ref, *perm_refs) = refs
    ns = S5_NSTATE
    nsub = SUBLANES
    win = seg * nsub
    jw = pl.program_id(1)

    @pl.when(jw == 0)
    def _():
        carry_ref[...] = sin_ref[...]

    n_lb = S5_DIM // LANES

    def permuted(tile_ref, buf_ref):
        for s in range(nsub):
            for k in range(n_lb):
                buf_ref[k, pl.ds(s, seg, stride=nsub), :] = tile_ref[s * seg:(s + 1) * seg,
                                                                     k * LANES:(k + 1) * LANES]
        return jnp.concatenate([buf_ref[k] for k in range(n_lb)], axis=1)

    if colmajor:
        u_win = u_ref[...].reshape(win, S5_DIM)
        prev_win = yprev_ref[...].reshape(win, S5_DIM) if final else None
    else:
        u_win = permuted(u_ref, perm_refs[0])
        prev_win = permuted(yprev_ref, perm_refs[1]) if final else None

    u_bf = u_win.astype(BF16)
    ch_per_tile = 2 * LANES // S5_STATE * S5_GROUP
    for j in range(2 * ns // (2 * LANES)):
        c0 = (j * ch_per_tile) % S5_DIM // LANES * LANES
        h_ref[:, j * 2 * LANES:(j + 1) * 2 * LANES] = _dot(
            u_bf[:, c0:c0 + LANES], bblk_ref[c0:c0 + LANES, j * 2 * LANES:(j + 1) * 2 * LANES])

    cw = 512
    n_chunks = ns // cw

    def lam_chunk(row, c):
        return (jnp.broadcast_to(lam_ref[row:row + 1, c * cw:(c + 1) * cw], (nsub, cw)),
                jnp.broadcast_to(lam_ref[row + 1:row + 2, c * cw:(c + 1) * cw], (nsub, cw)))

    def row0(i):
        step = (seg - 1 - i) if reverse else i
        return pl.multiple_of(step * nsub, nsub)

    for c in range(n_chunks):
        lre, lim = lam_chunk(0, c)
        cre = slice(c * cw, (c + 1) * cw)
        cim = slice(ns + c * cw, ns + (c + 1) * cw)

        def step1(i, hc, cre=cre, cim=cim, lre=lre, lim=lim):
            hre, him = hc
            r0 = row0(i)
            nre = lre * hre - lim * him + h_ref[pl.ds(r0, nsub), cre]
            nim = lre * him + lim * hre + h_ref[pl.ds(r0, nsub), cim]
            h_ref[pl.ds(r0, nsub), cre] = nre
            h_ref[pl.ds(r0, nsub), cim] = nim
            return nre, nim

        z0 = jnp.zeros((nsub, cw), F32)
        fre, fim = lax.fori_loop(0, seg, step1, (z0, z0), unroll=4)
        fin_ref[:, cre] = fre
        fin_ref[:, cim] = fim

    seg_row = {32: 2, 64: 4}[seg]
    gre, gim = lam_ref[seg_row:seg_row + 1, :], lam_ref[seg_row + 1:seg_row + 2, :]
    cur_re, cur_im = carry_ref[:, 0:ns], carry_ref[:, ns:2 * ns]
    order = range(nsub - 1, -1, -1) if reverse else range(nsub)
    for s in order:
        init_ref[s:s + 1, 0:ns] = cur_re
        init_ref[s:s + 1, ns:2 * ns] = cur_im
        f_re, f_im = fin_ref[s:s + 1, 0:ns], fin_ref[s:s + 1, ns:2 * ns]
        cur_re, cur_im = gre * cur_re - gim * cur_im + f_re, gre * cur_im + gim * cur_re + f_im
    carry_ref[:, 0:ns] = cur_re
    carry_ref[:, ns:2 * ns] = cur_im

    for c in range(n_chunks):
        lre, lim = lam_chunk(0, c)
        cre = slice(c * cw, (c + 1) * cw)
        cim = slice(ns + c * cw, ns + (c + 1) * cw)

        def step2(i, gc, cre=cre, cim=cim, lre=lre, lim=lim):
            g_re, g_im = gc
            n_re = lre * g_re - lim * g_im
            n_im = lre * g_im + lim * g_re
            r0 = row0(i)
            h_ref[pl.ds(r0, nsub), cre] = h_ref[pl.ds(r0, nsub), cre] + n_re
            h_ref[pl.ds(r0, nsub), cim] = h_ref[pl.ds(r0, nsub), cim] + n_im
            return n_re, n_im

        lax.fori_loop(0, seg, step2, (init_ref[:, cre], init_ref[:, cim]), unroll=4)

    halves = []
    st_per_tile = 2 * LANES // S5_GROUP * S5_STATE
    for n in range(S5_DIM // (2 * LANES)):
        oc = slice(n * 2 * LANES, (n + 1) * 2 * LANES)
        s_re = slice(n * st_per_tile, (n + 1) * st_per_tile)
        s_im = slice(ns + n * st_per_tile, ns + (n + 1) * st_per_tile)
        halves.append(_dot(h_ref[:, s_re].astype(BF16), cblk_ref[s_re, oc])
                      + _dot(h_ref[:, s_im].astype(BF16), cblk_ref[s_im, oc]))
    y_win = jnp.concatenate(halves, axis=1)

    if final:
        tot = prev_win + y_win + dskip_ref[...] * u_win
        gl = _gelu_tanh(tot)
        gate = jax.nn.sigmoid(_dot(gl.astype(BF16), gw_ref[...]) + gb_ref[...])
        v = gl * gate
        ms = jnp.mean(v * v, axis=-1, keepdims=True)
        y_win = v * lax.rsqrt(ms + EPS) * og_ref[...]

    if colmajor:
        o_ref[...] = y_win.reshape(o_ref.shape).astype(o_ref.dtype)
    else:
        y_ref = perm_refs[0]
        for k in range(n_lb):
            y_ref[k] = y_win[:, k * LANES:(k + 1) * LANES]
        for s in range(nsub):
            for k in range(n_lb):
                o_ref[s * seg:(s + 1) * seg, k * LANES:(k + 1) * LANES] = (
                    y_ref[k, pl.ds(s, seg, stride=nsub), :].astype(o_ref.dtype))

    @pl.when(jw == pl.num_programs(1) - 1)
    def _():
        sout_ref[...] = carry_ref[...]


def _s5_scan(u2d, ops, ld, state_in, n_batch, lat_len, reverse, colmajor, final_args=None):
    ns2 = 2 * S5_NSTATE
    bblk, cblk, lam = ops
    if colmajor:
        seg = GRID_W
        tile = (seg, SUBLANES, S5_DIM)
        n_win = GRID_W // SUBLANES
        u2d = u2d.reshape(u2d.shape[0] // GRID_W, GRID_W, S5_DIM)
        out_shape = (n_batch * seg, GRID_W, S5_DIM)
        imap = (lambda b, j: (b, n_win - 1 - j, 0)) if reverse else (lambda b, j: (b, j, 0))
        in_map = imap
    else:
        seg = S5_SEG
        tile = (S5_WIN, S5_DIM)
        n_win = 1
        out_shape = (n_batch * S5_WIN, S5_DIM)
        ctx_blk0 = n_batch * lat_len // S5_WIN
        imap = lambda b, j: (b, 0)
        in_map = lambda b, j: (ctx_blk0 + b, 0)
    win = seg * SUBLANES
    st = pl.BlockSpec((None, 1, ns2), lambda b, j: (b, 0, 0))
    stacked = lambda a: pl.BlockSpec((None,) + a.shape[1:], lambda b, j: (ld, 0, 0))
    final = final_args is not None
    in_specs = [pl.BlockSpec(tile, in_map)]
    args = [u2d]
    if final:
        in_specs.append(pl.BlockSpec(tile, imap))
        args.append(final_args[0])
    in_specs += [stacked(bblk), stacked(cblk), stacked(lam), st]
    args += [bblk, cblk, lam, state_in]
    scratch = [pltpu.VMEM((win, ns2), F32),
               pltpu.VMEM((SUBLANES, ns2), F32), pltpu.VMEM((SUBLANES, ns2), F32),
               pltpu.VMEM((1, ns2), F32)]
    if not colmajor:
        perm = pltpu.VMEM((S5_DIM // LANES, win, LANES), F32)
        scratch += [perm, perm] if final else [perm]
    if final:
        _, l, d_skip, glu_w_bf, glu_b, out_g = final_args
        for a in (d_skip,):
            a3, sp = _layer_rows(a, l)
            in_specs.append(sp)
            args.append(a3)
        gw, gw_spec = _layer_mat(glu_w_bf, l)
        in_specs.append(gw_spec)
        args.append(gw)
        for a in (glu_b, out_g):
            a3, sp = _layer_rows(a, l)
            in_specs.append(sp)
            args.append(a3)
    out, s_out = pl.pallas_call(
        functools.partial(_s5_kernel, reverse=reverse, final=final, colmajor=colmajor, seg=seg),
        out_shape=(jax.ShapeDtypeStruct(out_shape, F32),
                   jax.ShapeDtypeStruct((n_batch, 1, ns2), F32)),
        grid=(n_batch, n_win),
        in_specs=in_specs,
        out_specs=(pl.BlockSpec(tile, imap), st),
        scratch_shapes=scratch,
        compiler_params=_cparams(("arbitrary", "arbitrary")),
        name="s5_" + ("rev" if reverse else "fwd") + ("_lat" if colmajor else "_ctx"),
    )(*args)
    return out, s_out


def _s5_operands(lam_re, lam_im, log_step, b_re, b_im, c_re, c_im):
    g, p, k = S5_GROUPS, S5_STATE, S5_GROUP
    lam = lax.complex(jnp.minimum(lam_re.astype(F32), -1e-4), lam_im.astype(F32))
    step = jnp.exp(log_step.astype(F32))[:, None]
    lam_bar = jnp.exp(lam * step)
    lam_seg = jnp.exp(lam * (step * S5_SEG))
    lam_col = jnp.exp(lam * (step * GRID_W))
    b_bar = ((lam_bar - 1.0) / lam)[..., None] * lax.complex(b_re.astype(F32), b_im.astype(F32))
    eye = jnp.eye(g, dtype=F32)
    bd_in = lambda m: jnp.einsum('gkp,gh->gkhp', jnp.transpose(m, (0, 2, 1)), eye).reshape(g * k, g * p)
    bblk = jnp.concatenate([bd_in(jnp.real(b_bar)), bd_in(jnp.imag(b_bar))], axis=1)
    bd_out = lambda m: jnp.einsum('gpk,gh->gphk', jnp.transpose(m, (0, 2, 1)), eye).reshape(g * p, g * k)
    cblk = jnp.concatenate([bd_out(c_re.astype(F32)), -bd_out(c_im.astype(F32))], axis=0)
    zeros = jnp.zeros((g * p,), F32)
    lam_rows = jnp.stack([jnp.real(lam_bar).reshape(-1), jnp.imag(lam_bar).reshape(-1),
                          jnp.real(lam_seg).reshape(-1), jnp.imag(lam_seg).reshape(-1),
                          jnp.real(lam_col).reshape(-1), jnp.imag(lam_col).reshape(-1), zeros, zeros])
    return bblk.astype(BF16), cblk.astype(BF16), lam_rows


def _pack_bf16_pairs(v):
    n = v.shape[1] // 2
    bits = pltpu.bitcast(v.astype(BF16).astype(F32), U32)
    return (bits[:, :n] >> 16) | (bits[:, n:] & jnp.uint32(0xFFFF0000))


def _unpack_bf16_pairs(w):
    lo = pltpu.bitcast(w << 16, F32)
    hi = pltpu.bitcast(w & jnp.uint32(0xFFFF0000), F32)
    return lo, hi


def _store_token_tiles(ref, v):
    rows = v.shape[0]
    for c in range(SUBLANES):
        ref[pl.ds(c, rows, stride=SUBLANES), :] = v[:, c * LANES:(c + 1) * LANES]


def _load_token_tiles(ref, rows):
    return [ref[pl.ds(c, rows, stride=SUBLANES), :] for c in range(SUBLANES)]


def _mixout_kernel(r_ref, a_ref, b_ref, cl_ref, cc_ref, wa_ref, wb_ref, wc_ref, g1_ref, ng_ref, sh_ref,
                   sc_ref, rw_ref, rb_ref, r1_ref, hp_ref, idx_ref, gate_ref, rank_ref, cnt_ref,
                   carry_ref, *, n_lat_tiles):
    i = pl.program_id(0)

    @pl.when(i == 0)
    def _():
        carry_ref[...] = jnp.zeros_like(carry_ref)

    mix_c = jnp.where(i < n_lat_tiles, cl_ref[...], cc_ref[...]).astype(BF16)
    acc = _dot(a_ref[...], wa_ref[...]) + _dot(b_ref[...], wb_ref[...]) + _dot(mix_c, wc_ref[...])
    x = r_ref[...] + g1_ref[...] * acc
    r1_ref[...] = x
    ms = jnp.mean(x * x, axis=-1, keepdims=True)
    h = x * lax.rsqrt(ms + EPS) * ng_ref[...]
    h = h * (1.0 + sc_ref[...]) + sh_ref[...]
    _store_token_tiles(hp_ref, _pack_bf16_pairs(h))

    h_hi = h.astype(BF16)
    h_lo = (h - h_hi.astype(F32)).astype(BF16)
    rw = rw_ref[...]
    w_hi = rw.astype(BF16)
    w_lo = (rw - w_hi.astype(F32)).astype(BF16)
    logits = _dot(h_hi, w_hi) + _dot(h_lo, w_hi) + _dot(h_hi, w_lo) + rb_ref[...]

    tm = logits.shape[0]
    lane = lax.broadcasted_iota(I32, (tm, LANES), 1)
    lane_f = lane.astype(F32)
    work = logits
    tops, picks = [], []
    for _ in range(TOP_K):
        m = jnp.max(work, axis=-1, keepdims=True)
        pick = jnp.min(jnp.where(work == m, lane_f, float(LANES)), axis=-1, keepdims=True)
        work = jnp.where(lane_f == pick, -jnp.inf, work)
        tops.append(m)
        picks.append(pick)
    exps = [jnp.exp(t - tops[0]) for t in tops]
    denom = exps[0] + exps[1] + exps[2] + exps[3]

    onehot = jnp.zeros((tm, LANES), F32)
    for k in range(TOP_K):
        onehot = onehot + jnp.where(lane_f == picks[k], 1.0, 0.0)
    ri = lax.broadcasted_iota(I32, (tm, tm), 0)
    ci = lax.broadcasted_iota(I32, (tm, tm), 1)
    before = jnp.where(ci < ri, 1.0, 0.0).astype(BF16)
    base = carry_ref[0:1, :] + _dot(before, onehot.astype(BF16))
    carry_ref[...] = carry_ref[...] + jnp.sum(onehot, axis=0, keepdims=True)
    cnt_ref[...] = carry_ref[...]

    idx_out = jnp.zeros((tm, LANES), F32)
    gate_out = jnp.zeros((tm, LANES), F32)
    rank_out = jnp.zeros((tm, LANES), F32)
    for k in range(TOP_K):
        rank_k = jnp.sum(jnp.where(lane_f == picks[k], base, 0.0), axis=-1, keepdims=True)
        idx_out = jnp.where(lane == k, picks[k], idx_out)
        gate_out = jnp.where(lane == k, exps[k] / denom, gate_out)
        rank_out = jnp.where(lane == k, rank_k, rank_out)
    idx_ref[...] = idx_out.astype(I32)
    gate_ref[...] = gate_out
    rank_ref[...] = rank_out.astype(I32)


def _mix_out(r, mix_a, mix_b, mix_c_lat, mix_c_ctx, w_out_bf, mods4, norm2_g, router_w_pad, router_b_pad,
             l, lat_len, n_batch):
    nt, d = r.shape
    tm = ROW_TILE
    n_lat_tiles = n_batch * lat_len // tm

    def grp(i):
        return jnp.minimum((i * tm) // lat_len, n_batch)

    ng3, ng_spec = _layer_rows(norm2_g, l)
    rb3, rb_spec = _layer_rows(router_b_pad, l)
    tile = lambda w: pl.BlockSpec((tm, w), lambda i: (i, 0))
    return pl.pallas_call(
        functools.partial(_mixout_kernel, n_lat_tiles=n_lat_tiles),
        out_shape=(jax.ShapeDtypeStruct((nt, d), F32), jax.ShapeDtypeStruct((nt * SUBLANES, LANES), U32),
                   jax.ShapeDtypeStruct((nt, LANES), I32), jax.ShapeDtypeStruct((nt, LANES), F32),
                   jax.ShapeDtypeStruct((nt, LANES), I32), jax.ShapeDtypeStruct((SUBLANES, LANES), F32)),
        grid=(nt // tm,),
        in_specs=[tile(d), tile(SSD_INNER), tile(CONF_DIM),
                  pl.BlockSpec((tm, S5_DIM), lambda i: (jnp.minimum(i, n_lat_tiles - 1), 0)),
                  pl.BlockSpec((tm, S5_DIM), lambda i: (jnp.maximum(i - n_lat_tiles, 0), 0)),
                  pl.BlockSpec((None, SSD_INNER, d), lambda i: (l, 0, 0)),
                  pl.BlockSpec((None, CONF_DIM, d), lambda i: (l, SSD_INNER // CONF_DIM, 0)),
                  pl.BlockSpec((None, S5_DIM, d), lambda i: (l, (SSD_INNER + CONF_DIM) // S5_DIM, 0)),
                  _mod_spec(l, grp, 2, d), ng_spec, _mod_spec(l, grp, 3, d), _mod_spec(l, grp, 4, d),
                  pl.BlockSpec((None, d, LANES), lambda i: (l, 0, 0)), rb_spec],
        out_specs=(tile(d), pl.BlockSpec((tm * SUBLANES, LANES), lambda i: (i, 0)), tile(LANES), tile(LANES), tile(LANES),
                   pl.BlockSpec((SUBLANES, LANES), lambda i: (0, 0))),
        scratch_shapes=[pltpu.VMEM((SUBLANES, LANES), F32)],
        compiler_params=_cparams(("arbitrary",)),
        name="mix_out_router",
    )(r, mix_a, mix_b, mix_c_lat, mix_c_ctx, w_out_bf, w_out_bf, w_out_bf, mods4, ng3, mods4, mods4,
      router_w_pad, rb3)


def _expert_kernel(be_ref, nv_ref, first_ref, nxt_ref, g_cur_ref, g_nxt_ref, s_cur_ref, h_hbm, wgu_hbm,
                   bgu_ref, wd_hbm, bd_ref, ya_hbm, xbuf, ybuf, wgu_st, wd_st, wgu_bf, wd_bf, gsem, ssem,
                   wsem, *, layer):
    i = pl.program_id(0)
    n_steps = pl.num_programs(0)
    slot = i % 2
    nv = nv_ref[i]
    blk = MOE_BLOCK
    tile_rows = blk * SUBLANES

    def weight_copies(e):
        return (pltpu.make_async_copy(wgu_hbm.at[layer, e], wgu_st, wsem.at[0]),
                pltpu.make_async_copy(wd_hbm.at[layer, e], wd_st, wsem.at[1]))

    def token_tile(ref, row0):
        return ref.at[pl.ds(pl.multiple_of(row0, SUBLANES), SUBLANES)]

    def gather_rows(idx_ref, dst_slot):
        def body(rr, carry):
            pltpu.make_async_copy(token_tile(h_hbm, idx_ref[0, rr]),
                                  token_tile(xbuf.at[dst_slot], rr * SUBLANES), gsem.at[dst_slot]).start()
            return carry
        lax.fori_loop(0, blk, body, 0, unroll=16)

    def whole_block_wait(src, dst, sem):
        pltpu.make_async_copy(src, dst, sem).wait()

    @pl.when(i == 0)
    def _():
        for cp in weight_copies(be_ref[0]):
            cp.start()
        gather_rows(g_cur_ref, 0)

    @pl.when(i >= 2)
    def _():
        whole_block_wait(ybuf.at[slot], ya_hbm.at[pl.ds(0, tile_rows)], ssem.at[slot])

    @pl.when(nv > 0)
    def _():
        @pl.when(first_ref[i] == 1)
        def _():
            for cp in weight_copies(be_ref[i]):
                cp.wait()
            wgu_bf[...] = wgu_st[...].astype(BF16)
            wd_bf[...] = wd_st[...].astype(BF16)

            @pl.when(nxt_ref[i] >= 0)
            def _():
                for cp in weight_copies(nxt_ref[i]):
                    cp.start()

        whole_block_wait(h_hbm.at[pl.ds(0, tile_rows)], xbuf.at[slot], gsem.at[slot])

        @pl.when(jnp.logical_and(i + 1 < n_steps, nv_ref[jnp.minimum(i + 1, n_steps - 1)] > 0))
        def _():
            gather_rows(g_nxt_ref, 1 - slot)

        half = D_MODEL // 2
        xw = jnp.concatenate(_load_token_tiles(xbuf.at[slot], blk), axis=1)
        lo, hi = _unpack_bf16_pairs(xw)
        gu = (_dot(lo.astype(BF16), wgu_bf[0:half, :]) + _dot(hi.astype(BF16), wgu_bf[half:D_MODEL, :])
              + bgu_ref[...])
        gate = jnp.minimum(gu[:, :D_EXPERT], SWIGLU_LIMIT)
        lin = jnp.clip(gu[:, D_EXPERT:], -SWIGLU_LIMIT, SWIGLU_LIMIT)
        act = gate * jax.nn.sigmoid(SWIGLU_ALPHA * gate) * (lin + 1.0)
        y = _dot(act.astype(BF16), wd_bf[...]) + bd_ref[...]
        _store_token_tiles(ybuf.at[slot], _pack_bf16_pairs(y))

        def body(rr, carry):
            pltpu.make_async_copy(token_tile(ybuf.at[slot], rr * SUBLANES),
                                  token_tile(ya_hbm, s_cur_ref[0, rr]), ssem.at[slot]).start()
            return carry
        lax.fori_loop(0, blk, body, 0, unroll=16)

    @pl.when(nv == 0)
    def _():
        ybuf[slot] = jnp.zeros((tile_rows, LANES), U32)
        pltpu.make_async_copy(ybuf.at[slot],
                              ya_hbm.at[pl.ds(pl.multiple_of(i * tile_rows, tile_rows), tile_rows)],
                              ssem.at[slot]).start()

    @pl.when(i == n_steps - 1)
    def _():
        whole_block_wait(ybuf.at[slot], ya_hbm.at[pl.ds(0, tile_rows)], ssem.at[slot])

        @pl.when(i >= 1)
        def _():
            whole_block_wait(ybuf.at[1 - slot], ya_hbm.at[pl.ds(0, tile_rows)], ssem.at[1 - slot])


def _experts(h_tiles, gather_row, scatter_row, block_e, n_valid, first, nxt, w_gu, b_gu, w_down, b_down,
             layer):
    n_rows = gather_row.shape[0]
    n_blocks = n_rows // MOE_BLOCK
    depth, ne = w_gu.shape[:2]
    g3 = gather_row.reshape(n_blocks, 1, MOE_BLOCK)
    s3 = scatter_row.reshape(n_blocks, 1, MOE_BLOCK)
    width = D_MODEL // 2
    tile_rows = MOE_BLOCK * SUBLANES
    idx_block = lambda imap: pl.BlockSpec((None, 1, MOE_BLOCK), imap, memory_space=pltpu.SMEM)
    grid_spec = pltpu.PrefetchScalarGridSpec(
        num_scalar_prefetch=4,
        grid=(n_blocks,),
        in_specs=[idx_block(lambda i, *_: (i, 0, 0)),
                  idx_block(lambda i, *_: (jnp.minimum(i + 1, n_blocks - 1), 0, 0)),
                  idx_block(lambda i, *_: (i, 0, 0)),
                  pl.BlockSpec(memory_space=pl.ANY),
                  pl.BlockSpec(memory_space=pl.ANY),
                  pl.BlockSpec((None, None, 1, 2 * D_EXPERT), lambda i, be, *_: (layer, be[i], 0, 0)),
                  pl.BlockSpec(memory_space=pl.ANY),
                  pl.BlockSpec((None, None, 1, D_MODEL), lambda i, be, *_: (layer, be[i], 0, 0))],
        out_specs=pl.BlockSpec(memory_space=pl.ANY),
        scratch_shapes=[pltpu.VMEM((2, tile_rows, LANES), U32), pltpu.VMEM((2, tile_rows, LANES), U32),
                        pltpu.VMEM((D_MODEL, 2 * D_EXPERT), F32), pltpu.VMEM((D_EXPERT, D_MODEL), F32),
                        pltpu.VMEM((D_MODEL, 2 * D_EXPERT), BF16), pltpu.VMEM((D_EXPERT, D_MODEL), BF16),
                        pltpu.SemaphoreType.DMA((2,)), pltpu.SemaphoreType.DMA((2,)),
                        pltpu.SemaphoreType.DMA((2,))],
    )
    return pl.pallas_call(
        functools.partial(_expert_kernel, layer=layer),
        out_shape=jax.ShapeDtypeStruct((n_rows * SUBLANES, LANES), U32),
        grid_spec=grid_spec,
        compiler_params=_cparams(("arbitrary",)),
        name="moe_experts",
    )(block_e, n_valid, first, nxt, g3, g3, s3, h_tiles, w_gu, b_gu.reshape(depth, ne, 1, -1), w_down,
      b_down.reshape(depth, ne, 1, -1))


def _combine_kernel(r_ref, gate_ref, y0_ref, y1_ref, y2_ref, y3_ref, g2_ref, fg_ref, o_ref, *, last_layer):
    tm = r_ref.shape[0]
    half = D_MODEL // 2
    gates = [gate_ref[:, k:k + 1] for k in range(TOP_K)]
    y_refs = (y0_ref, y1_ref, y2_ref, y3_ref)
    for c in range(SUBLANES):
        acc_lo = jnp.zeros((tm, LANES), F32)
        acc_hi = jnp.zeros((tm, LANES), F32)
        for k in range(TOP_K):
            lo, hi = _unpack_bf16_pairs(y_refs[k][pl.ds(c, tm, stride=SUBLANES), :])
            acc_lo = acc_lo + gates[k] * lo
            acc_hi = acc_hi + gates[k] * hi
        lo_cols = slice(c * LANES, (c + 1) * LANES)
        hi_cols = slice(half + c * LANES, half + (c + 1) * LANES)
        o_ref[:, lo_cols] = r_ref[:, lo_cols] + g2_ref[:, lo_cols] * acc_lo
        o_ref[:, hi_cols] = r_ref[:, hi_cols] + g2_ref[:, hi_cols] * acc_hi
    if last_layer:
        x = o_ref[...]
        ms = jnp.mean(x * x, axis=-1, keepdims=True)
        o_ref[...] = x * lax.rsqrt(ms + EPS) * fg_ref[...]


def _combine(r1, gates, ya, mods4, final_g, l, lat_len, n_batch, last_layer):
    nt, d = r1.shape
    tm = ROW_TILE
    n_out = n_batch * lat_len if last_layer else nt
    tiles_per_k = nt // tm

    def grp(i):
        return jnp.minimum((i * tm) // lat_len, n_batch)

    def choice(k):
        return pl.BlockSpec((tm * SUBLANES, LANES), lambda i: (k * tiles_per_k + i, 0))

    return pl.pallas_call(
        functools.partial(_combine_kernel, last_layer=last_layer),
        out_shape=jax.ShapeDtypeStruct((n_out, d), F32),
        grid=(n_out // tm,),
        in_specs=[pl.BlockSpec((tm, d), lambda i: (i, 0)),
                  pl.BlockSpec((tm, LANES), lambda i: (i, 0)),
                  choice(0), choice(1), choice(2), choice(3),
                  _mod_spec(l, grp, 5, d),
                  pl.BlockSpec((1, d), lambda i: (0, 0))],
        out_specs=pl.BlockSpec((tm, d), lambda i: (i, 0)),
        compiler_params=_cparams(("arbitrary",)),
        name="moe_combine",
    )(r1, gates, ya, ya, ya, ya, mods4, final_g.reshape(1, d))


def _routing_plan(top_idx, rank, counts, n_blocks):
    n_tok = top_idx.shape[0]
    n_assign = top_idx.size
    n_rows = n_blocks * MOE_BLOCK
    flat_e = top_idx.reshape(-1)
    counts = counts.astype(I32)
    padded = (counts + MOE_BLOCK - 1) // MOE_BLOCK * MOE_BLOCK
    pad_end = jnp.cumsum(padded)
    pad_start = pad_end - padded
    count_end = jnp.cumsum(counts)
    dest = pad_start[flat_e] + rank.reshape(-1)
    row_assign = jnp.full((n_rows,), -1, I32).at[dest].set(jnp.arange(n_assign, dtype=I32),
                                                           unique_indices=True, mode='drop')
    blk_start = jnp.arange(n_blocks, dtype=I32) * MOE_BLOCK

    def expert_at(pos):
        return jnp.minimum(jnp.sum((pad_end[None, :] <= pos[:, None]).astype(I32), axis=1), N_EXPERTS - 1)

    block_e = expert_at(blk_start)
    used = blk_start < pad_end[-1]
    n_valid = jnp.where(used, jnp.clip(pad_start[block_e] + counts[block_e] - blk_start, 0, MOE_BLOCK), 0)
    first = jnp.logical_and(used, blk_start == pad_start[block_e]).astype(I32)
    nxt_start = pad_end[block_e]
    nxt = jnp.where(nxt_start < pad_end[-1], expert_at(nxt_start), -1).astype(I32)

    row = jnp.arange(n_rows, dtype=I32)
    empty = row_assign < 0
    dump = n_assign + row - jnp.repeat(count_end[block_e], MOE_BLOCK)
    tok = jnp.minimum(row_assign >> 2, n_tok - 1)
    gather_row = jnp.where(empty, 0, tok) * SUBLANES
    scatter_row = jnp.where(empty, dump, (row_assign & 3) * n_tok + (row_assign >> 2)) * SUBLANES
    return gather_row.astype(I32), scatter_row.astype(I32), block_e.astype(I32), n_valid.astype(I32), first, nxt


def _forward(x, c, ctx, c_ctx, ada_w, ada_b, norm1_g, w_in, ssd_conv_w, ssd_conv_b, ssd_a_log,
             ssd_dt_bias, ssd_d, ssd_norm_g, conf_dw_w, conf_dw_b, conf_ln_g, conf_ln_b, conf_pw_w,
             conf_pw_b, conf_out_g, s5_lam_re, s5_lam_im, s5_log_step, s5_b_re, s5_b_im, s5_c_re,
             s5_c_im, s5_d, s5_glu_w, s5_glu_b, s5_out_g, w_out, norm2_g, router_w, router_b,
             w_gate_up, b_gate_up, w_down, b_down, final_norm_g):
    n_batch, lat_len, d = x.shape
    ctx_len = ctx.shape[1]
    depth = ada_w.shape[0]
    n_lat = n_batch * lat_len
    nt = n_lat + n_batch * ctx_len
    assert d == D_MODEL and ctx_len == ROW_TILE and lat_len % MM_TILE_M == 0
    assert lat_len // GRID_W == GRID_W

    r = jnp.concatenate([x.reshape(n_lat, d), ctx.reshape(n_batch * ctx_len, d)], axis=0).astype(F32)
    cond = jnp.zeros((SUBLANES, d), F32).at[:n_batch].set(c).at[n_batch].set(c_ctx)
    mods4 = _modulation(cond, ada_w, ada_b).reshape(depth, SUBLANES, 1, 6 * d)

    c_dt = SSD_INNER + SSD_CONV_DIM
    c_conf = c_dt + SSD_HEADS
    c_s5 = c_conf + 2 * CONF_DIM
    w_main = jnp.concatenate([w_in[:, :, :c_dt], w_in[:, :, c_conf:c_s5]], axis=2).astype(BF16)
    w_side = jnp.concatenate([w_in[:, :, c_s5:], w_in[:, :, c_dt:c_conf],
                              jnp.zeros((depth, d, LANES - SSD_HEADS), w_in.dtype)], axis=2).astype(BF16)
    w_out_bf = w_out.astype(BF16)
    conf_pw_bf = conf_pw_w.astype(BF16)
    s5_glu_bf = s5_glu_w.astype(BF16)
    head_pad = lambda a: jnp.pad(a.astype(F32), ((0, 0), (0, 0), (0, LANES - SSD_HEADS))).reshape(
        depth * 2, 1, LANES)
    dtb_all, alog_all = head_pad(ssd_dt_bias), head_pad(ssd_a_log)
    d_skip_all = jnp.repeat(ssd_d.astype(F32), SSD_HEAD_DIM, axis=1)
    router_w_pad = jnp.pad(router_w.astype(F32), ((0, 0), (0, 0), (0, LANES - N_EXPERTS)))
    router_b_pad = jnp.pad(router_b.astype(F32), ((0, 0), (0, LANES - N_EXPERTS)), constant_values=-1e30)
    flat2 = lambda a: a.reshape((depth * 2,) + a.shape[2:])
    s5_ops = jax.vmap(_s5_operands)(*[flat2(a) for a in (s5_lam_re, s5_lam_im, s5_log_step, s5_b_re,
                                                         s5_b_im, s5_c_re, s5_c_im)])

    n_blocks = nt * TOP_K // MOE_BLOCK + N_EXPERTS
    s5_zero = jnp.zeros((n_batch, 1, 2 * S5_NSTATE), F32)

    for l in range(depth):
        p_main = _norm_matmul(r, norm1_g, mods4, l, w_main, lat_len, n_batch, "in_proj_main", (BF16,))
        p_s5, p_dt = _norm_matmul(r, norm1_g, mods4, l, w_side, lat_len, n_batch, "in_proj_side",
                                  (F32, F32), splits=(S5_DIM, LANES))

        xbc = _ssd_conv(p_main, ssd_conv_w, ssd_conv_b, l, lat_len, ctx_len, n_batch)
        y_fwd = _ssd_scan(xbc, p_main, p_dt, dtb_all, alog_all, l, lat_len, ctx_len, n_batch, False)
        mix_a = _ssd_scan(xbc, p_main, p_dt, dtb_all, alog_all, l, lat_len, ctx_len, n_batch, True,
                          (y_fwd, d_skip_all, ssd_norm_g))

        mix_b = _conformer(p_main, conf_dw_w, conf_dw_b, conf_ln_g, conf_ln_b, conf_pw_bf, conf_pw_b,
                           conf_out_g, l, lat_len, ctx_len, n_batch)

        yc_f, st_f = _s5_scan(p_s5, s5_ops, 2 * l, s5_zero, n_batch, lat_len, False, False)
        yl_f, _ = _s5_scan(p_s5, s5_ops, 2 * l, st_f, n_batch, lat_len, False, True)
        fin = (l, s5_d, s5_glu_bf, s5_glu_b, s5_out_g)
        mc_c, st_r = _s5_scan(p_s5, s5_ops, 2 * l + 1, s5_zero, n_batch, lat_len, True, False, (yc_f,) + fin)
        mc_l, _ = _s5_scan(p_s5, s5_ops, 2 * l + 1, st_r, n_batch, lat_len, True, True, (yl_f,) + fin)

        r1, h_tiles, top_idx, gates, rank, counts = _mix_out(
            r, mix_a, mix_b, mc_l.reshape(n_lat, S5_DIM), mc_c, w_out_bf, mods4, norm2_g, router_w_pad,
            router_b_pad, l, lat_len, n_batch)

        gather_row, scatter_row, block_e, n_valid, first, nxt = _routing_plan(
            top_idx[:, :TOP_K], rank[:, :TOP_K], counts[0, :N_EXPERTS], n_blocks)
        ya = _experts(h_tiles, gather_row, scatter_row, block_e, n_valid, first, nxt, w_gate_up, b_gate_up,
                      w_down, b_down, l)
        r = _combine(r1, gates, ya, mods4, final_norm_g, l, lat_len, n_batch, l == depth - 1)

    return r.reshape(n_batch, lat_len, d).astype(x.dtype)


def kernel(x, c, ctx, c_ctx, ada_w, ada_b, norm1_g, w_in, ssd_conv_w, ssd_conv_b, ssd_a_log, ssd_dt_bias,
           ssd_d, ssd_norm_g, conf_dw_w, conf_dw_b, conf_ln_g, conf_ln_b, conf_pw_w, conf_pw_b, conf_out_g,
           s5_lam_re, s5_lam_im, s5_log_step, s5_b_re, s5_b_im, s5_c_re, s5_c_im, s5_d, s5_glu_w, s5_glu_b,
           s5_out_g, w_out, norm2_g, router_w, router_b, w_gate_up, b_gate_up, w_down, b_down, final_norm_g):
    return _forward(x, c, ctx, c_ctx, ada_w, ada_b, norm1_g, w_in, ssd_conv_w, ssd_conv_b, ssd_a_log,
                    ssd_dt_bias, ssd_d, ssd_norm_g, conf_dw_w, conf_dw_b, conf_ln_g, conf_ln_b, conf_pw_w,
                    conf_pw_b, conf_out_g, s5_lam_re, s5_lam_im, s5_log_step, s5_b_re, s5_b_im, s5_c_re,
                    s5_c_im, s5_d, s5_glu_w, s5_glu_b, s5_out_g, w_out, norm2_g, router_w, router_b,
                    w_gate_up, b_gate_up, w_down, b_down, final_norm_g)
```

```python
import functools
import math

import jax
import jax.numpy as jnp
from jax import lax
from jax.experimental import pallas as pl
from jax.experimental.pallas import tpu as pltpu

F32 = jnp.float32
BF16 = jnp.bfloat16
I32 = jnp.int32
U32 = jnp.uint32

D_MODEL = 2048
GRID_W = 64
SSD_INNER = 1024
SSD_HEAD_DIM = 64
SSD_HEADS = 16
SSD_GROUPS = 4
SSD_STATE = 128
SSD_CONV = 5
SSD_CHUNK = 128
SSD_CONV_DIM = SSD_INNER + 2 * SSD_GROUPS * SSD_STATE
CONF_DIM = 512
CONF_KERNEL = 31
S5_DIM = 512
S5_GROUP = 16
S5_GROUPS = 32
S5_STATE = 64
S5_NSTATE = S5_GROUPS * S5_STATE
N_EXPERTS = 32
TOP_K = 4
D_EXPERT = 768
SWIGLU_LIMIT = 7.0
SWIGLU_ALPHA = 1.702
MOE_BLOCK = 256
EPS = 1e-6

LANES = 128
SUBLANES = 8
ROW_TILE = 256
MM_TILE_M = 512
MM_TILE_N = 1024
MAIN_COLS = 4096
COL_Z, COL_X, COL_CONF = 0, 1024, 3072
SIDE_COLS = S5_DIM + LANES
CONV_HALO = 16
CONF_HALO = 16
S5_SEG = 32
S5_WIN = SUBLANES * S5_SEG
VMEM_LIMIT = 56 * 1024 * 1024


def _cparams(sem, vmem=VMEM_LIMIT):
    return pltpu.CompilerParams(dimension_semantics=sem, vmem_limit_bytes=vmem)


def _silu(v):
    return v * jax.nn.sigmoid(v)


def _split3(v):
    hi = v.astype(BF16)
    r1 = v - hi.astype(F32)
    mid = r1.astype(BF16)
    lo = (r1 - mid.astype(F32)).astype(BF16)
    return hi, mid, lo


def _dot(a, b):
    return jnp.dot(a, b, preferred_element_type=F32)


def _layer_rows(arr, l):
    depth = arr.shape[0]
    a3 = arr.reshape(depth, 1, -1)
    return a3, pl.BlockSpec((None, 1, a3.shape[2]), lambda *_: (l, 0, 0))


def _layer_mat(arr, l):
    return arr, pl.BlockSpec((None,) + arr.shape[1:], lambda *_: (l, 0, 0))


def _mod_kernel(c_ref, w_ref, b_ref, o_ref):
    s = _silu(c_ref[...])
    w = w_ref[...]
    s_hi = s.astype(BF16)
    s_lo = (s - s_hi.astype(F32)).astype(BF16)
    w_hi = w.astype(BF16)
    w_lo = (w - w_hi.astype(F32)).astype(BF16)
    acc = _dot(s_hi, w_hi) + _dot(s_lo, w_hi) + _dot(s_hi, w_lo)
    o_ref[...] = acc + b_ref[...]


def _modulation(cond, ada_w, ada_b):
    depth, d, n = ada_w.shape
    tn = 1024
    return pl.pallas_call(
        _mod_kernel,
        out_shape=jax.ShapeDtypeStruct((depth, SUBLANES, n), F32),
        grid=(depth, n // tn),
        in_specs=[pl.BlockSpec((SUBLANES, d), lambda l, j: (0, 0)),
                  pl.BlockSpec((None, d, tn), lambda l, j: (l, 0, j)),
                  pl.BlockSpec((None, 1, tn), lambda l, j: (l, 0, j))],
        out_specs=pl.BlockSpec((None, SUBLANES, tn), lambda l, j: (l, 0, j)),
        compiler_params=_cparams(("arbitrary", "arbitrary")),
        name="adaln_mod",
    )(cond, ada_w, ada_b.reshape(depth, 1, n))


def _mod_spec(l, grp, k, d):
    return pl.BlockSpec((None, None, 1, d), lambda i, *_: (l, grp(i), 0, k))


def _normmm_kernel(x_ref, g_ref, sh_ref, sc_ref, w_ref, *rest, n_tiles):
    o_refs, hn_ref = rest[:-1], rest[-1]
    s = pl.program_id(2)
    live = 2 * pl.program_id(0) + s < n_tiles

    @pl.when(jnp.logical_and(pl.program_id(1) == 0, live))
    def _():
        x = x_ref[...]
        ms = jnp.mean(x * x, axis=-1, keepdims=True)
        y = x * lax.rsqrt(ms + EPS) * g_ref[...]
        hn_ref[s] = (y * (1.0 + sc_ref[...]) + sh_ref[...]).astype(BF16)

    @pl.when(live)
    def _():
        res = _dot(hn_ref[s], w_ref[...])
        col = 0
        for o_ref in o_refs:
            wd = o_ref.shape[-1]
            o_ref[...] = res[:, col:col + wd].reshape(o_ref.shape).astype(o_ref.dtype)
            col += wd


def _norm_matmul(r, gains, mods4, l, w_all, lat_len, n_batch, name, out_dtypes, splits=None,
                 grid_rows=()):
    nt, d = r.shape
    n = w_all.shape[2]
    tm = MM_TILE_M if nt % MM_TILE_M == 0 else ROW_TILE
    tn = n if splits else MM_TILE_N
    assert n % tn == 0
    widths = splits or (tn,)
    n_tiles = nt // tm
    last = n_tiles - 1

    def tile(p, j, s):
        return jnp.minimum(2 * p + s, last)

    def x_tile(p, j, s):
        return jnp.where(j == 0, tile(p, j, s), jnp.minimum(2 * p + 1, last))

    def grp(t):
        return jnp.minimum((t * tm) // lat_len, n_batch)

    def mod_spec(k):
        return pl.BlockSpec((None, None, 1, d), lambda p, j, s: (l, grp(tile(p, j, s)), 0, k))

    g3, g_spec = _layer_rows(gains, l)
    out_shapes, out_specs = [], []
    for k, (wd, dt) in enumerate(zip(widths, out_dtypes)):
        if k in grid_rows:
            out_shapes.append(jax.ShapeDtypeStruct((nt // GRID_W, GRID_W, wd), dt))
            out_specs.append(pl.BlockSpec((tm // GRID_W, GRID_W, wd), lambda p, j, s: (tile(p, j, s), 0, 0)))
        else:
            out_shapes.append(jax.ShapeDtypeStruct((nt, n if not splits else wd), dt))
            out_specs.append(pl.BlockSpec((tm, wd), lambda p, j, s: (tile(p, j, s), j)))
    outs = pl.pallas_call(
        functools.partial(_normmm_kernel, n_tiles=n_tiles),
        out_shape=tuple(out_shapes),
        grid=(pl.cdiv(n_tiles, 2), n // tn, 2),
        in_specs=[pl.BlockSpec((tm, d), lambda p, j, s: (x_tile(p, j, s), 0)),
                  g_spec, mod_spec(0), mod_spec(1),
                  pl.BlockSpec((None, d, tn), lambda p, j, s: (l, 0, j))],
        out_specs=tuple(out_specs),
        scratch_shapes=[pltpu.VMEM((2, tm, d), BF16)],
        compiler_params=_cparams(("arbitrary", "arbitrary", "arbitrary")),
        name=name,
    )(r, g3, mods4, mods4, w_all)
    return outs if splits else outs[0]


def _seq_edges(i, tiles_per_lat_seq, n_lat_tiles):
    is_lat = i < n_lat_tiles
    first = jnp.logical_or(jnp.logical_not(is_lat), (i % tiles_per_lat_seq) == 0)
    last = jnp.logical_or(jnp.logical_not(is_lat), (i % tiles_per_lat_seq) == tiles_per_lat_seq - 1)
    return first, last


def _conv5_kernel(cur_ref, prev_ref, next_ref, w_ref, b_ref, o_ref, ext_ref, *, tiles_per_lat_seq,
                  n_lat_tiles):
    i = pl.program_id(0)
    first, last = _seq_edges(i, tiles_per_lat_seq, n_lat_tiles)
    h, tm = CONV_HALO, ROW_TILE
    ext_ref[0:h, :] = jnp.where(first, 0.0, prev_ref[...].astype(F32))
    ext_ref[h:h + tm, :] = cur_ref[...].astype(F32)
    ext_ref[h + tm:h + tm + h, :] = jnp.where(last, 0.0, next_ref[...].astype(F32))
    pad = (SSD_CONV - 1) // 2
    cw = 512
    for c in range(0, ext_ref.shape[1], cw):
        acc = jnp.broadcast_to(b_ref[:, c:c + cw], (tm, cw))
        for j in range(SSD_CONV):
            acc = acc + w_ref[j:j + 1, c:c + cw] * ext_ref[h - pad + j:h - pad + j + tm, c:c + cw]
        o_ref[:, c:c + cw] = _silu(acc).astype(BF16)


def _ssd_conv(p_main, conv_w, conv_b, l, lat_len, ctx_len, n_batch):
    nt = p_main.shape[0]
    tm, h = ROW_TILE, CONV_HALO
    assert ctx_len == tm and lat_len % tm == 0
    n_lat_tiles = n_batch * lat_len // tm
    cw = SSD_CONV_DIM // 2
    xblk = COL_X // cw
    nhb = nt // h
    depth = conv_w.shape[0]
    kern = functools.partial(_conv5_kernel, tiles_per_lat_seq=lat_len // tm, n_lat_tiles=n_lat_tiles)
    return pl.pallas_call(
        kern,
        out_shape=jax.ShapeDtypeStruct((nt, SSD_CONV_DIM), BF16),
        grid=(nt // tm, 2),
        in_specs=[pl.BlockSpec((tm, cw), lambda i, j: (i, xblk + j)),
                  pl.BlockSpec((h, cw), lambda i, j: (jnp.maximum(i * (tm // h) - 1, 0), xblk + j)),
                  pl.BlockSpec((h, cw), lambda i, j: (jnp.minimum((i + 1) * (tm // h), nhb - 1), xblk + j)),
                  pl.BlockSpec((None, SSD_CONV, cw), lambda i, j: (l, 0, j)),
                  pl.BlockSpec((None, 1, cw), lambda i, j: (l, 0, j))],
        out_specs=pl.BlockSpec((tm, cw), lambda i, j: (i, j)),
        scratch_shapes=[pltpu.VMEM((tm + 2 * h, cw), F32)],
        compiler_params=_cparams(("arbitrary", "arbitrary")),
        name="ssd_conv",
    )(p_main, p_main, p_main, conv_w, conv_b.reshape(depth, 1, -1))


def _head_cols(vals, width):
    lane = lax.broadcasted_iota(I32, (1, LANES), 1)
    halves = []
    per_half = LANES // width
    for hh in range(len(vals) // per_half):
        sel = vals[hh * per_half + per_half - 1]
        for k in range(per_half - 2, -1, -1):
            sel = jnp.where(lane < (k + 1) * width, vals[hh * per_half + k], sel)
        halves.append(sel)
    return jnp.concatenate(halves, axis=1)


def _ssd_kernel(*refs, reverse, final):
    if final:
        (x_ref, b_ref, c_ref, dt_ref, dtb_ref, alog_ref, hx_ref, yprev_ref, z_ref, dskip_ref, ng_ref,
         o_ref, state_ref, tmp_ref) = refs
    else:
        x_ref, b_ref, c_ref, dt_ref, dtb_ref, alog_ref, hx_ref, o_ref, state_ref = refs
    q = SSD_CHUNK
    r = SSD_HEADS // SSD_GROUPS
    gw = r * SSD_HEAD_DIM

    @pl.when(pl.program_id(1) == 0)
    def _():
        state_ref[...] = jnp.zeros_like(state_ref)

    lane = lax.broadcasted_iota(I32, (1, LANES), 1)
    dtp = jax.nn.softplus(dt_ref[...] + dtb_ref[...])
    a_head = -jnp.exp(alog_ref[...])
    a = jnp.where(lane < SSD_HEADS, dtp * a_head, 0.0)
    ri = lax.broadcasted_iota(I32, (q, q), 0)
    ci = lax.broadcasted_iota(I32, (q, q), 1)
    tri = (ci >= ri) if reverse else (ci <= ri)
    tri_b = jnp.where(tri, 1.0, 0.0).astype(BF16)
    a_hi, a_mid, a_lo = _split3(a)
    a_cs = _dot(tri_b, a_hi) + _dot(tri_b, a_mid) + _dot(tri_b, a_lo)
    a_cs_t = a_cs.T
    dtp_t = dtp.T
    a_end = a_cs[0:1, :] if reverse else a_cs[q - 1:q, :]
    lane_g = lax.broadcasted_iota(I32, (1, gw), 1)

    pieces = []
    for fac in (jnp.exp(a_cs), dtp * jnp.exp(a_end - a_cs)):
        hi = fac.astype(BF16)
        pieces += [hi, (fac - hi.astype(F32)).astype(BF16)]
    spread = _dot(jnp.concatenate(pieces, axis=0), hx_ref[...])
    e_all = spread[0:q] + spread[q:2 * q]
    w_all = spread[2 * q:3 * q] + spread[3 * q:4 * q]

    x = x_ref[...]
    for g in range(SSD_GROUPS):
        cg = c_ref[:, g * SSD_STATE:(g + 1) * SSD_STATE]
        bg = b_ref[:, g * SSD_STATE:(g + 1) * SSD_STATE]
        cb = lax.dot_general(cg, bg, (((1,), (1,)), ((), ())), preferred_element_type=F32)
        xg = x[:, g * gw:(g + 1) * gw]
        yg = jnp.zeros((q, gw), F32)
        dec = []
        for hl in range(r):
            h = g * r + hl
            col = a_cs[:, h:h + 1]
            row = a_cs_t[h:h + 1, :]
            lm = jnp.where(tri, jnp.exp(col - row), 0.0)
            m = (cb * lm * dtp_t[h:h + 1, :]).astype(BF16)
            in_head = jnp.logical_and(lane_g >= hl * SSD_HEAD_DIM, lane_g < (hl + 1) * SSD_HEAD_DIM)
            xm = jnp.where(in_head, xg, jnp.zeros_like(xg))
            yg = yg + _dot(m, xm)
            dec.append(jnp.exp(a_end[:, h:h + 1]))
        s_old = state_ref[g]
        gs = slice(g * gw, (g + 1) * gw)
        yg = yg + e_all[:, gs] * _dot(cg, s_old.astype(BF16))
        xw = (xg.astype(F32) * w_all[:, gs]).astype(BF16)
        upd = lax.dot_general(bg, xw, (((0,), (0,)), ((), ())), preferred_element_type=F32)
        state_ref[g] = _head_cols(dec, SSD_HEAD_DIM) * s_old + upd
        sl = slice(g * gw, (g + 1) * gw)
        if final:
            ytot = yprev_ref[:, sl] + yg + dskip_ref[:, sl] * xg.astype(F32)
            tmp_ref[:, sl] = ytot * _silu(z_ref[:, sl].astype(F32))
        else:
            o_ref[:, sl] = yg
    if final:
        gated = tmp_ref[...]
        ms = jnp.mean(gated * gated, axis=-1, keepdims=True)
        o_ref[...] = (gated * lax.rsqrt(ms + EPS) * ng_ref[...]).astype(o_ref.dtype)


def _ssd_scan(xbc, p_main, p_dt, dtb_all, alog_all, l, lat_len, ctx_len, n_batch, reverse, final_args=None):
    nt = xbc.shape[0]
    q = SSD_CHUNK
    ncl, ncc = lat_len // q, ctx_len // q
    ctx0 = n_batch * ncl
    dd = 1 if reverse else 0

    def blk(b, j):
        if reverse:
            return jnp.where(j < ncc, ctx0 + b * ncc + (ncc - 1 - j), b * ncl + (ncl - 1 - (j - ncc)))
        return jnp.where(j < ncc, ctx0 + b * ncc + j, b * ncl + (j - ncc))

    head_vec = pl.BlockSpec((None, 1, LANES), lambda b, j: (2 * l + dd, 0, 0))
    final = final_args is not None
    in_specs = [pl.BlockSpec((q, SSD_INNER), lambda b, j: (blk(b, j), 0)),
                pl.BlockSpec((q, SSD_GROUPS * SSD_STATE), lambda b, j: (blk(b, j), 2)),
                pl.BlockSpec((q, SSD_GROUPS * SSD_STATE), lambda b, j: (blk(b, j), 3)),
                pl.BlockSpec((q, LANES), lambda b, j: (blk(b, j), 0)),
                head_vec, head_vec,
                pl.BlockSpec((LANES, SSD_INNER), lambda b, j: (0, 0))]
    head_spread = (jnp.arange(SSD_INNER, dtype=I32)[None, :] // SSD_HEAD_DIM
                   == jnp.arange(LANES, dtype=I32)[:, None]).astype(BF16)
    args = [xbc, xbc, xbc, p_dt, dtb_all, alog_all, head_spread]
    scratch = [pltpu.VMEM((SSD_GROUPS, SSD_STATE, (SSD_HEADS // SSD_GROUPS) * SSD_HEAD_DIM), F32)]
    if final:
        y_prev, d_skip_all, norm_g_all = final_args
        ds3, ds_spec = _layer_rows(d_skip_all, l)
        ng3, ng_spec = _layer_rows(norm_g_all, l)
        in_specs += [pl.BlockSpec((q, SSD_INNER), lambda b, j: (blk(b, j), 0)),
                     pl.BlockSpec((q, SSD_INNER), lambda b, j: (blk(b, j), COL_Z // SSD_INNER)),
                     ds_spec, ng_spec]
        args += [y_prev, p_main, ds3, ng3]
        scratch.append(pltpu.VMEM((q, SSD_INNER), F32))
    return pl.pallas_call(
        functools.partial(_ssd_kernel, reverse=reverse, final=final),
        out_shape=jax.ShapeDtypeStruct((nt, SSD_INNER), BF16 if final else F32),
        grid=(n_batch, ncc + ncl),
        in_specs=in_specs,
        out_specs=pl.BlockSpec((q, SSD_INNER), lambda b, j: (blk(b, j), 0)),
        scratch_shapes=scratch,
        compiler_params=_cparams(("arbitrary", "arbitrary")),
        name="ssd_scan_rev" if reverse else "ssd_scan_fwd",
    )(*args)


def _glu(ref):
    v = ref[:, 0:CONF_DIM].astype(F32)
    gt = ref[:, CONF_DIM:2 * CONF_DIM].astype(F32)
    return v * jax.nn.sigmoid(gt)


def _conf_kernel(cur_ref, prev_ref, next_ref, dww_ref, dwb_ref, lng_ref, lnb_ref, pww_ref, pwb_ref,
                 og_ref, o_ref, ext_ref, acc_ref, sh_ref, *, tiles_per_lat_seq, n_lat_tiles):
    i = pl.program_id(0)
    first, last = _seq_edges(i, tiles_per_lat_seq, n_lat_tiles)
    h, tm = CONF_HALO, ROW_TILE
    ext_ref[0:h, :] = jnp.where(first, 0.0, _glu(prev_ref))
    ext_ref[h:h + tm, :] = _glu(cur_ref)
    ext_ref[h + tm:h + tm + h, :] = jnp.where(last, 0.0, _glu(next_ref))
    pad = (CONF_KERNEL - 1) // 2
    span = (CONF_KERNEL - 1) // SUBLANES * SUBLANES + tm
    for b in range(SUBLANES):
        sh_ref[b] = ext_ref[h - pad + b:h - pad + b + span, :]
    cw = 256
    for c in range(0, CONF_DIM, cw):
        acc = jnp.broadcast_to(dwb_ref[:, c:c + cw], (tm, cw))
        for j in range(CONF_KERNEL):
            a8 = j // SUBLANES * SUBLANES
            acc = acc + dww_ref[j:j + 1, c:c + cw] * sh_ref[j % SUBLANES, a8:a8 + tm, c:c + cw]
        acc_ref[:, c:c + cw] = acc
    u = acc_ref[...]
    mu = jnp.mean(u, axis=-1, keepdims=True)
    var = jnp.mean(jnp.square(u - mu), axis=-1, keepdims=True)
    y = (u - mu) * lax.rsqrt(var + EPS) * lng_ref[...] + lnb_ref[...]
    y = _silu(y)
    v = _dot(y.astype(BF16), pww_ref[...]) + pwb_ref[...]
    ms = jnp.mean(v * v, axis=-1, keepdims=True)
    o_ref[...] = (v * lax.rsqrt(ms + EPS) * og_ref[...]).astype(o_ref.dtype)


def _conformer(p_main, dw_w, dw_b, ln_g, ln_b, pw_w_bf, pw_b, out_g, l, lat_len, ctx_len, n_batch):
    nt = p_main.shape[0]
    tm, h = ROW_TILE, CONF_HALO
    n_lat_tiles = n_batch * lat_len // tm
    cblk = COL_CONF // (2 * CONF_DIM)
    nhb = nt // h
    kern = functools.partial(_conf_kernel, tiles_per_lat_seq=lat_len // tm, n_lat_tiles=n_lat_tiles)
    rows = [_layer_rows(a, l) for a in (dw_b, ln_g, ln_b)]
    rows2 = [_layer_rows(a, l) for a in (pw_b, out_g)]
    dww, dww_spec = _layer_mat(dw_w, l)
    pww, pww_spec = _layer_mat(pw_w_bf, l)
    return pl.pallas_call(
        kern,
        out_shape=jax.ShapeDtypeStruct((nt, CONF_DIM), BF16),
        grid=(nt // tm,),
        in_specs=[pl.BlockSpec((tm, 2 * CONF_DIM), lambda i: (i, cblk)),
                  pl.BlockSpec((h, 2 * CONF_DIM), lambda i: (jnp.maximum(i * (tm // h) - 1, 0), cblk)),
                  pl.BlockSpec((h, 2 * CONF_DIM), lambda i: (jnp.minimum((i + 1) * (tm // h), nhb - 1), cblk)),
                  dww_spec] + [s for _, s in rows] + [pww_spec] + [s for _, s in rows2],
        out_specs=pl.BlockSpec((tm, CONF_DIM), lambda i: (i, 0)),
        scratch_shapes=[pltpu.VMEM((tm + 2 * h, CONF_DIM), F32), pltpu.VMEM((tm, CONF_DIM), F32),
                        pltpu.VMEM((SUBLANES, (CONF_KERNEL - 1) // SUBLANES * SUBLANES + tm, CONF_DIM), F32)],
        compiler_params=_cparams(("arbitrary",)),
        name="conformer",
    )(p_main, p_main, p_main, dww, *[a for a, _ in rows], pww, *[a for a, _ in rows2])


def _gelu_tanh(v):
    return 0.5 * v * (1.0 + jnp.tanh(math.sqrt(2.0 / math.pi) * (v + 0.044715 * (v * v * v))))


def _s5_kernel(*refs, reverse, final, colmajor, seg):
    if final:
        (u_ref, yprev_ref, bblk_ref, cblk_ref, lam_ref, sin_ref, dskip_ref, gw_ref, gb_ref, og_ref,
         o_ref, sout_ref, h_ref, fin_ref, init_ref, carry_ref, *perm_refs) = refs
    else:
        (u_ref, bblk_ref, cblk_ref, lam_ref, sin_ref,
         o_ref, sout_ref, h_ref, fin_ref, init_ref, carry_ref, *perm_refs) = refs
    ns = S5_NSTATE
    nsub = SUBLANES
    win = seg * nsub
    jw = pl.program_id(1)

    @pl.when(jw == 0)
    def _():
        carry_ref[...] = sin_ref[...]

    n_lb = S5_DIM // LANES

    per_row = GRID_W // seg

    def sub_seg(s):
        return s // per_row, slice((s % per_row) * seg, (s % per_row + 1) * seg)

    def permuted(tile_ref, buf_ref):
        for s in range(nsub):
            g, rows = sub_seg(s)
            for k in range(n_lb):
                buf_ref[k, pl.ds(s, seg, stride=nsub), :] = tile_ref[g, rows, k * LANES:(k + 1) * LANES]
        return jnp.concatenate([buf_ref[k] for k in range(n_lb)], axis=1)

    if colmajor:
        u_win = u_ref[...].reshape(win, S5_DIM)
        prev_win = yprev_ref[...].reshape(win, S5_DIM) if final else None
    else:
        u_win = permuted(u_ref, perm_refs[0])
        prev_win = permuted(yprev_ref, perm_refs[1]) if final else None

    u_bf = u_win.astype(BF16)
    ch_per_tile = 2 * LANES // S5_STATE * S5_GROUP
    for j in range(2 * ns // (2 * LANES)):
        c0 = (j * ch_per_tile) % S5_DIM // LANES * LANES
        h_ref[:, j * 2 * LANES:(j + 1) * 2 * LANES] = _dot(
            u_bf[:, c0:c0 + LANES], bblk_ref[c0:c0 + LANES, j * 2 * LANES:(j + 1) * 2 * LANES])

    cw = 512
    n_chunks = ns // cw

    def lam_chunk(row, c):
        return (jnp.broadcast_to(lam_ref[row:row + 1, c * cw:(c + 1) * cw], (nsub, cw)),
                jnp.broadcast_to(lam_ref[row + 1:row + 2, c * cw:(c + 1) * cw], (nsub, cw)))

    def row0(i):
        step = (seg - 1 - i) if reverse else i
        return pl.multiple_of(step * nsub, nsub)

    for c in range(n_chunks):
        lre, lim = lam_chunk(0, c)
        cre = slice(c * cw, (c + 1) * cw)
        cim = slice(ns + c * cw, ns + (c + 1) * cw)

        def step1(i, hc, cre=cre, cim=cim, lre=lre, lim=lim):
            hre, him = hc
            r0 = row0(i)
            nre = lre * hre - lim * him + h_ref[pl.ds(r0, nsub), cre]
            nim = lre * him + lim * hre + h_ref[pl.ds(r0, nsub), cim]
            h_ref[pl.ds(r0, nsub), cre] = nre
            h_ref[pl.ds(r0, nsub), cim] = nim
            return nre, nim

        z0 = jnp.zeros((nsub, cw), F32)
        fre, fim = lax.fori_loop(0, seg, step1, (z0, z0), unroll=4)
        fin_ref[:, cre] = fre
        fin_ref[:, cim] = fim

    seg_row = {32: 2, 64: 4}[seg]
    gre, gim = lam_ref[seg_row:seg_row + 1, :], lam_ref[seg_row + 1:seg_row + 2, :]
    cur_re, cur_im = carry_ref[:, 0:ns], carry_ref[:, ns:2 * ns]
    order = range(nsub - 1, -1, -1) if reverse else range(nsub)
    for s in order:
        init_ref[s:s + 1, 0:ns] = cur_re
        init_ref[s:s + 1, ns:2 * ns] = cur_im
        f_re, f_im = fin_ref[s:s + 1, 0:ns], fin_ref[s:s + 1, ns:2 * ns]
        cur_re, cur_im = gre * cur_re - gim * cur_im + f_re, gre * cur_im + gim * cur_re + f_im
    carry_ref[:, 0:ns] = cur_re
    carry_ref[:, ns:2 * ns] = cur_im

    for c in range(n_chunks):
        lre, lim = lam_chunk(0, c)
        cre = slice(c * cw, (c + 1) * cw)
        cim = slice(ns + c * cw, ns + (c + 1) * cw)

        def step2(i, gc, cre=cre, cim=cim, lre=lre, lim=lim):
            g_re, g_im = gc
            n_re = lre * g_re - lim * g_im
            n_im = lre * g_im + lim * g_re
            r0 = row0(i)
            h_ref[pl.ds(r0, nsub), cre] = h_ref[pl.ds(r0, nsub), cre] + n_re
            h_ref[pl.ds(r0, nsub), cim] = h_ref[pl.ds(r0, nsub), cim] + n_im
            return n_re, n_im

        lax.fori_loop(0, seg, step2, (init_ref[:, cre], init_ref[:, cim]), unroll=4)

    halves = []
    st_per_tile = 2 * LANES // S5_GROUP * S5_STATE
    for n in range(S5_DIM // (2 * LANES)):
        oc = slice(n * 2 * LANES, (n + 1) * 2 * LANES)
        s_re = slice(n * st_per_tile, (n + 1) * st_per_tile)
        s_im = slice(ns + n * st_per_tile, ns + (n + 1) * st_per_tile)
        halves.append(_dot(h_ref[:, s_re].astype(BF16), cblk_ref[s_re, oc])
                      + _dot(h_ref[:, s_im].astype(BF16), cblk_ref[s_im, oc]))
    y_win = jnp.concatenate(halves, axis=1)

    if final:
        tot = prev_win + y_win + dskip_ref[...] * u_win
        gl = _gelu_tanh(tot)
        gate = jax.nn.sigmoid(_dot(gl.astype(BF16), gw_ref[...]) + gb_ref[...])
        v = gl * gate
        ms = jnp.mean(v * v, axis=-1, keepdims=True)
        y_win = v * lax.rsqrt(ms + EPS) * og_ref[...]

    if colmajor:
        o_ref[...] = y_win.reshape(o_ref.shape).astype(o_ref.dtype)
    else:
        y_ref = perm_refs[0]
        for k in range(n_lb):
            y_ref[k] = y_win[:, k * LANES:(k + 1) * LANES]
        for s in range(nsub):
            g, rows = sub_seg(s)
            for k in range(n_lb):
                o_ref[g, rows, k * LANES:(k + 1) * LANES] = (
                    y_ref[k, pl.ds(s, seg, stride=nsub), :].astype(o_ref.dtype))

    @pl.when(jw == pl.num_programs(1) - 1)
    def _():
        sout_ref[...] = carry_ref[...]


def _s5_scan(u3d, ops, ld, state_in, n_batch, lat_len, reverse, colmajor, final_args=None):
    ns2 = 2 * S5_NSTATE
    bblk, cblk, lam = ops
    if colmajor:
        seg = GRID_W
        tile = (seg, SUBLANES, S5_DIM)
        n_win = GRID_W // SUBLANES
        out_shape = (n_batch * seg, GRID_W, S5_DIM)
        imap = (lambda b, j: (b, n_win - 1 - j, 0)) if reverse else (lambda b, j: (b, j, 0))
        in_map = imap
    else:
        seg = S5_SEG
        tile = (S5_WIN // GRID_W, GRID_W, S5_DIM)
        n_win = 1
        out_shape = (n_batch * S5_WIN // GRID_W, GRID_W, S5_DIM)
        ctx_blk0 = n_batch * lat_len // S5_WIN
        imap = lambda b, j: (b, 0, 0)
        in_map = lambda b, j: (ctx_blk0 + b, 0, 0)
    win = seg * SUBLANES
    st = pl.BlockSpec((None, 1, ns2), lambda b, j: (b, 0, 0))
    stacked = lambda a: pl.BlockSpec((None,) + a.shape[1:], lambda b, j: (ld, 0, 0))
    final = final_args is not None
    in_specs = [pl.BlockSpec(tile, in_map)]
    args = [u3d]
    if final:
        in_specs.append(pl.BlockSpec(tile, imap))
        args.append(final_args[0])
    in_specs += [stacked(bblk), stacked(cblk), stacked(lam), st]
    args += [bblk, cblk, lam, state_in]
    scratch = [pltpu.VMEM((win, ns2), F32),
               pltpu.VMEM((SUBLANES, ns2), F32), pltpu.VMEM((SUBLANES, ns2), F32),
               pltpu.VMEM((1, ns2), F32)]
    if not colmajor:
        perm = pltpu.VMEM((S5_DIM // LANES, win, LANES), F32)
        scratch += [perm, perm] if final else [perm]
    if final:
        _, l, d_skip, glu_w_bf, glu_b, out_g = final_args
        for a in (d_skip,):
            a3, sp = _layer_rows(a, l)
            in_specs.append(sp)
            args.append(a3)
        gw, gw_spec = _layer_mat(glu_w_bf, l)
        in_specs.append(gw_spec)
        args.append(gw)
        for a in (glu_b, out_g):
            a3, sp = _layer_rows(a, l)
            in_specs.append(sp)
            args.append(a3)
    out, s_out = pl.pallas_call(
        functools.partial(_s5_kernel, reverse=reverse, final=final, colmajor=colmajor, seg=seg),
        out_shape=(jax.ShapeDtypeStruct(out_shape, F32),
                   jax.ShapeDtypeStruct((n_batch, 1, ns2), F32)),
        grid=(n_batch, n_win),
        in_specs=in_specs,
        out_specs=(pl.BlockSpec(tile, imap), st),
        scratch_shapes=scratch,
        compiler_params=_cparams(("arbitrary", "arbitrary")),
        name="s5_" + ("rev" if reverse else "fwd") + ("_lat" if colmajor else "_ctx"),
    )(*args)
    return out, s_out


def _s5_operands(lam_re, lam_im, log_step, b_re, b_im, c_re, c_im):
    g, p, k = S5_GROUPS, S5_STATE, S5_GROUP
    lam = lax.complex(jnp.minimum(lam_re.astype(F32), -1e-4), lam_im.astype(F32))
    step = jnp.exp(log_step.astype(F32))[:, None]
    lam_bar = jnp.exp(lam * step)
    lam_seg = jnp.exp(lam * (step * S5_SEG))
    lam_col = jnp.exp(lam * (step * GRID_W))
    b_bar = ((lam_bar - 1.0) / lam)[..., None] * lax.complex(b_re.astype(F32), b_im.astype(F32))
    eye = jnp.eye(g, dtype=F32)
    bd_in = lambda m: jnp.einsum('gkp,gh->gkhp', jnp.transpose(m, (0, 2, 1)), eye).reshape(g * k, g * p)
    bblk = jnp.concatenate([bd_in(jnp.real(b_bar)), bd_in(jnp.imag(b_bar))], axis=1)
    bd_out = lambda m: jnp.einsum('gpk,gh->gphk', jnp.transpose(m, (0, 2, 1)), eye).reshape(g * p, g * k)
    cblk = jnp.concatenate([bd_out(c_re.astype(F32)), -bd_out(c_im.astype(F32))], axis=0)
    zeros = jnp.zeros((g * p,), F32)
    lam_rows = jnp.stack([jnp.real(lam_bar).reshape(-1), jnp.imag(lam_bar).reshape(-1),
                          jnp.real(lam_seg).reshape(-1), jnp.imag(lam_seg).reshape(-1),
                          jnp.real(lam_col).reshape(-1), jnp.imag(lam_col).reshape(-1), zeros, zeros])
    return bblk.astype(BF16), cblk.astype(BF16), lam_rows


def _pack_bf16_pairs(v):
    n = v.shape[1] // 2
    bits = pltpu.bitcast(v.astype(BF16).astype(F32), U32)
    return (bits[:, :n] >> 16) | (bits[:, n:] & jnp.uint32(0xFFFF0000))


def _unpack_bf16_pairs(w):
    lo = pltpu.bitcast(w << 16, F32)
    hi = pltpu.bitcast(w & jnp.uint32(0xFFFF0000), F32)
    return lo, hi


def _store_token_tiles(ref, v):
    rows = v.shape[0]
    for c in range(SUBLANES):
        ref[pl.ds(c, rows, stride=SUBLANES), :] = v[:, c * LANES:(c + 1) * LANES]


def _load_token_tiles(ref, rows):
    return [ref[pl.ds(c, rows, stride=SUBLANES), :] for c in range(SUBLANES)]


def _mixout_kernel(r_ref, a_ref, b_ref, cl_ref, cc_ref, wa_ref, wb_ref, wc_ref, g1_ref, ng_ref, sh_ref,
                   sc_ref, rw_ref, rb_ref, r1_ref, hp_ref, idx_ref, gate_ref, rank_ref, cnt_ref,
                   carry_ref, *, n_lat_tiles):
    i = pl.program_id(0)

    @pl.when(i == 0)
    def _():
        carry_ref[...] = jnp.zeros_like(carry_ref)

    mix_c = jnp.where(i < n_lat_tiles, cl_ref[...], cc_ref[...]).reshape(a_ref.shape[0], S5_DIM).astype(BF16)
    acc = _dot(a_ref[...], wa_ref[...]) + _dot(b_ref[...], wb_ref[...]) + _dot(mix_c, wc_ref[...])
    x = r_ref[...] + g1_ref[...] * acc
    r1_ref[...] = x
    ms = jnp.mean(x * x, axis=-1, keepdims=True)
    h = x * lax.rsqrt(ms + EPS) * ng_ref[...]
    h = h * (1.0 + sc_ref[...]) + sh_ref[...]
    _store_token_tiles(hp_ref, _pack_bf16_pairs(h))

    h_hi = h.astype(BF16)
    h_lo = (h - h_hi.astype(F32)).astype(BF16)
    rw = rw_ref[...]
    w_hi = rw.astype(BF16)
    w_lo = (rw - w_hi.astype(F32)).astype(BF16)
    logits = _dot(h_hi, w_hi) + _dot(h_lo, w_hi) + _dot(h_hi, w_lo) + rb_ref[...]

    tm = logits.shape[0]
    lane = lax.broadcasted_iota(I32, (tm, LANES), 1)
    lane_f = lane.astype(F32)
    work = logits
    tops, picks = [], []
    for _ in range(TOP_K):
        m = jnp.max(work, axis=-1, keepdims=True)
        pick = jnp.min(jnp.where(work == m, lane_f, float(LANES)), axis=-1, keepdims=True)
        work = jnp.where(lane_f == pick, -jnp.inf, work)
        tops.append(m)
        picks.append(pick)
    exps = [jnp.exp(t - tops[0]) for t in tops]
    denom = exps[0] + exps[1] + exps[2] + exps[3]

    onehot = jnp.zeros((tm, LANES), F32)
    for k in range(TOP_K):
        onehot = onehot + jnp.where(lane_f == picks[k], 1.0, 0.0)
    ri = lax.broadcasted_iota(I32, (tm, tm), 0)
    ci = lax.broadcasted_iota(I32, (tm, tm), 1)
    before = jnp.where(ci < ri, 1.0, 0.0).astype(BF16)
    base = carry_ref[0:1, :] + _dot(before, onehot.astype(BF16))
    carry_ref[...] = carry_ref[...] + jnp.sum(onehot, axis=0, keepdims=True)
    cnt_ref[...] = carry_ref[...]

    idx_out = jnp.zeros((tm, LANES), F32)
    gate_out = jnp.zeros((tm, LANES), F32)
    rank_out = jnp.zeros((tm, LANES), F32)
    for k in range(TOP_K):
        rank_k = jnp.sum(jnp.where(lane_f == picks[k], base, 0.0), axis=-1, keepdims=True)
        idx_out = jnp.where(lane == k, picks[k], idx_out)
        gate_out = jnp.where(lane == k, exps[k] / denom, gate_out)
        rank_out = jnp.where(lane == k, rank_k, rank_out)
    idx_ref[...] = idx_out.astype(I32)
    gate_ref[...] = gate_out
    rank_ref[...] = rank_out.astype(I32)


def _mix_out(r, mix_a, mix_b, mix_c_lat, mix_c_ctx, w_out_bf, mods4, norm2_g, router_w_pad, router_b_pad,
             l, lat_len, n_batch):
    nt, d = r.shape
    tm = ROW_TILE
    n_lat_tiles = n_batch * lat_len // tm

    def grp(i):
        return jnp.minimum((i * tm) // lat_len, n_batch)

    ng3, ng_spec = _layer_rows(norm2_g, l)
    rb3, rb_spec = _layer_rows(router_b_pad, l)
    tile = lambda w: pl.BlockSpec((tm, w), lambda i: (i, 0))
    return pl.pallas_call(
        functools.partial(_mixout_kernel, n_lat_tiles=n_lat_tiles),
        out_shape=(jax.ShapeDtypeStruct((nt, d), F32), jax.ShapeDtypeStruct((nt * SUBLANES, LANES), U32),
                   jax.ShapeDtypeStruct((nt, LANES), I32), jax.ShapeDtypeStruct((nt, LANES), F32),
                   jax.ShapeDtypeStruct((nt, LANES), I32), jax.ShapeDtypeStruct((SUBLANES, LANES), F32)),
        grid=(nt // tm,),
        in_specs=[tile(d), tile(SSD_INNER), tile(CONF_DIM),
                  pl.BlockSpec((tm // GRID_W, GRID_W, S5_DIM), lambda i: (jnp.minimum(i, n_lat_tiles - 1), 0, 0)),
                  pl.BlockSpec((tm // GRID_W, GRID_W, S5_DIM), lambda i: (jnp.maximum(i - n_lat_tiles, 0), 0, 0)),
                  pl.BlockSpec((None, SSD_INNER, d), lambda i: (l, 0, 0)),
                  pl.BlockSpec((None, CONF_DIM, d), lambda i: (l, SSD_INNER // CONF_DIM, 0)),
                  pl.BlockSpec((None, S5_DIM, d), lambda i: (l, (SSD_INNER + CONF_DIM) // S5_DIM, 0)),
                  _mod_spec(l, grp, 2, d), ng_spec, _mod_spec(l, grp, 3, d), _mod_spec(l, grp, 4, d),
                  pl.BlockSpec((None, d, LANES), lambda i: (l, 0, 0)), rb_spec],
        out_specs=(tile(d), pl.BlockSpec((tm * SUBLANES, LANES), lambda i: (i, 0)), tile(LANES), tile(LANES), tile(LANES),
                   pl.BlockSpec((SUBLANES, LANES), lambda i: (0, 0))),
        scratch_shapes=[pltpu.VMEM((SUBLANES, LANES), F32)],
        compiler_params=_cparams(("arbitrary",)),
        name="mix_out_router",
    )(r, mix_a, mix_b, mix_c_lat, mix_c_ctx, w_out_bf, w_out_bf, w_out_bf, mods4, ng3, mods4, mods4,
      router_w_pad, rb3)


def _expert_kernel(be_ref, nv_ref, first_ref, nxt_ref, g_cur_ref, g_nxt_ref, s_cur_ref, s_prv_ref, h_hbm,
                   wgu_hbm, bgu_ref, wd_hbm, bd_ref, ya_hbm, xbuf, ybuf, wgu_st, wd_st, wgu_bf, wd_bf, zbuf,
                   gsem, ssem, wsem, zsem, *, layer):
    i = pl.program_id(0)
    n_steps = pl.num_programs(0)
    slot = i % 2
    nv = nv_ref[i]
    blk = MOE_BLOCK
    tile_rows = blk * SUBLANES

    def weight_copies(e):
        return (pltpu.make_async_copy(wgu_hbm.at[layer, e], wgu_st, wsem.at[0]),
                pltpu.make_async_copy(wd_hbm.at[layer, e], wd_st, wsem.at[1]))

    def token_tile(ref, row0):
        return ref.at[pl.ds(pl.multiple_of(row0, SUBLANES), SUBLANES)]

    def used(j):
        return jnp.logical_and(jnp.logical_and(j >= 0, j < n_steps),
                               nv_ref[jnp.clip(j, 0, n_steps - 1)] > 0)

    n_phases = 1
    chunk = blk // n_phases

    def gather_rows(idx_ref, dst_slot, lo_row, n):
        def body(p, carry):
            for u in range(2):
                rr = lo_row + 2 * p + u
                pltpu.make_async_copy(token_tile(h_hbm, idx_ref[0, rr]),
                                      token_tile(xbuf.at[dst_slot], rr * SUBLANES),
                                      gsem.at[dst_slot]).start(priority=u)
            return carry
        lax.fori_loop(0, n // 2, body, 0, unroll=8)

    def scatter_rows(idx_ref, src_slot, lo_row, n):
        def body(p, carry):
            for u in range(2):
                rr = lo_row + 2 * p + u
                pltpu.make_async_copy(token_tile(ybuf.at[src_slot], rr * SUBLANES),
                                      token_tile(ya_hbm, idx_ref[0, rr]), ssem.at[src_slot]).start(priority=u)
            return carry
        lax.fori_loop(0, n // 2, body, 0, unroll=8)

    def scatter_wait(src_slot):
        pltpu.make_async_copy(ybuf.at[src_slot], ya_hbm.at[pl.ds(0, tile_rows)], ssem.at[src_slot]).wait()

    def issue_phase(ph):
        @pl.when(used(i + 1))
        def _():
            gather_rows(g_nxt_ref, 1 - slot, ph * chunk, chunk)

        @pl.when(used(i - 1))
        def _():
            scatter_rows(s_prv_ref, 1 - slot, ph * chunk, chunk)

    @pl.when(i == 0)
    def _():
        zbuf[...] = jnp.zeros_like(zbuf)
        for cp in weight_copies(be_ref[0]):
            cp.start()
        gather_rows(g_cur_ref, 0, 0, blk)

    @pl.when(nv > 0)
    def _():
        @pl.when(first_ref[i] == 1)
        def _():
            for cp in weight_copies(be_ref[i]):
                cp.wait()
            wgu_bf[...] = wgu_st[...].astype(BF16)
            wd_bf[...] = wd_st[...].astype(BF16)

            @pl.when(nxt_ref[i] >= 0)
            def _():
                for cp in weight_copies(nxt_ref[i]):
                    cp.start()

        pltpu.make_async_copy(h_hbm.at[pl.ds(0, tile_rows)], xbuf.at[slot], gsem.at[slot]).wait()
        half = D_MODEL // 2
        xw = jnp.concatenate(_load_token_tiles(xbuf.at[slot], blk), axis=1)
        lo, hi = _unpack_bf16_pairs(xw)
        issue_phase(0)
        gu = (_dot(lo.astype(BF16), wgu_bf[0:half, :]) + _dot(hi.astype(BF16), wgu_bf[half:D_MODEL, :])
              + bgu_ref[...])
        gate = jnp.minimum(gu[:, :D_EXPERT], SWIGLU_LIMIT)
        lin = jnp.clip(gu[:, D_EXPERT:], -SWIGLU_LIMIT, SWIGLU_LIMIT)
        act = (gate * jax.nn.sigmoid(SWIGLU_ALPHA * gate) * (lin + 1.0)).astype(BF16)
        y = _dot(act, wd_bf[...]) + bd_ref[...]

        @pl.when(used(i - 2))
        def _():
            scatter_wait(slot)

        _store_token_tiles(ybuf.at[slot], _pack_bf16_pairs(y))

    @pl.when(nv == 0)
    def _():
        @pl.when(used(i - 2))
        def _():
            scatter_wait(slot)

        @pl.when(used(i - 1))
        def _():
            scatter_rows(s_prv_ref, 1 - slot, 0, blk)

        own = ya_hbm.at[pl.ds(pl.multiple_of(i * tile_rows, tile_rows), tile_rows)]
        zero_copy = pltpu.make_async_copy(zbuf, own, zsem)
        zero_copy.start()
        zero_copy.wait()

    @pl.when(i == n_steps - 1)
    def _():
        @pl.when(nv > 0)
        def _():
            scatter_rows(s_cur_ref, slot, 0, blk)
            scatter_wait(slot)

        @pl.when(used(i - 1))
        def _():
            scatter_wait(1 - slot)


def _experts(h_tiles, gather_row, scatter_row, block_e, n_valid, first, nxt, w_gu, b_gu, w_down, b_down,
             layer):
    n_rows = gather_row.shape[0]
    n_blocks = n_rows // MOE_BLOCK
    depth, ne = w_gu.shape[:2]
    g3 = gather_row.reshape(n_blocks, 1, MOE_BLOCK)
    s3 = scatter_row.reshape(n_blocks, 1, MOE_BLOCK)
    width = D_MODEL // 2
    tile_rows = MOE_BLOCK * SUBLANES
    idx_block = lambda imap: pl.BlockSpec((None, 1, MOE_BLOCK), imap, memory_space=pltpu.SMEM)
    grid_spec = pltpu.PrefetchScalarGridSpec(
        num_scalar_prefetch=4,
        grid=(n_blocks,),
        in_specs=[idx_block(lambda i, *_: (i, 0, 0)),
                  idx_block(lambda i, *_: (jnp.minimum(i + 1, n_blocks - 1), 0, 0)),
                  idx_block(lambda i, *_: (i, 0, 0)),
                  idx_block(lambda i, *_: (jnp.maximum(i - 1, 0), 0, 0)),
                  pl.BlockSpec(memory_space=pl.ANY),
                  pl.BlockSpec(memory_space=pl.ANY),
                  pl.BlockSpec((None, None, 1, 2 * D_EXPERT), lambda i, be, *_: (layer, be[i], 0, 0)),
                  pl.BlockSpec(memory_space=pl.ANY),
                  pl.BlockSpec((None, None, 1, D_MODEL), lambda i, be, *_: (layer, be[i], 0, 0))],
        out_specs=pl.BlockSpec(memory_space=pl.ANY),
        scratch_shapes=[pltpu.VMEM((2, tile_rows, LANES), U32), pltpu.VMEM((2, tile_rows, LANES), U32),
                        pltpu.VMEM((D_MODEL, 2 * D_EXPERT), F32), pltpu.VMEM((D_EXPERT, D_MODEL), F32),
                        pltpu.VMEM((D_MODEL, 2 * D_EXPERT), BF16), pltpu.VMEM((D_EXPERT, D_MODEL), BF16),
                        pltpu.VMEM((tile_rows, LANES), U32),
                        pltpu.SemaphoreType.DMA((2,)), pltpu.SemaphoreType.DMA((2,)),
                        pltpu.SemaphoreType.DMA((2,)), pltpu.SemaphoreType.DMA],
    )
    return pl.pallas_call(
        functools.partial(_expert_kernel, layer=layer),
        out_shape=jax.ShapeDtypeStruct((n_rows * SUBLANES, LANES), U32),
        grid_spec=grid_spec,
        compiler_params=_cparams(("arbitrary",)),
        name="moe_experts",
    )(block_e, n_valid, first, nxt, g3, g3, s3, s3, h_tiles, w_gu, b_gu.reshape(depth, ne, 1, -1), w_down,
      b_down.reshape(depth, ne, 1, -1))


def _combine_kernel(r_ref, gate_ref, y0_ref, y1_ref, y2_ref, y3_ref, g2_ref, fg_ref, o_ref, *, last_layer):
    tm = r_ref.shape[0]
    half = D_MODEL // 2
    gates = [gate_ref[:, k:k + 1] for k in range(TOP_K)]
    y_refs = (y0_ref, y1_ref, y2_ref, y3_ref)
    for c in range(SUBLANES):
        acc_lo = jnp.zeros((tm, LANES), F32)
        acc_hi = jnp.zeros((tm, LANES), F32)
        for k in range(TOP_K):
            lo, hi = _unpack_bf16_pairs(y_refs[k][pl.ds(c, tm, stride=SUBLANES), :])
            acc_lo = acc_lo + gates[k] * lo
            acc_hi = acc_hi + gates[k] * hi
        lo_cols = slice(c * LANES, (c + 1) * LANES)
        hi_cols = slice(half + c * LANES, half + (c + 1) * LANES)
        o_ref[:, lo_cols] = r_ref[:, lo_cols] + g2_ref[:, lo_cols] * acc_lo
        o_ref[:, hi_cols] = r_ref[:, hi_cols] + g2_ref[:, hi_cols] * acc_hi
    if last_layer:
        x = o_ref[...]
        ms = jnp.mean(x * x, axis=-1, keepdims=True)
        o_ref[...] = x * lax.rsqrt(ms + EPS) * fg_ref[...]


def _combine(r1, gates, ya, mods4, final_g, l, lat_len, n_batch, last_layer):
    nt, d = r1.shape
    tm = ROW_TILE
    n_out = n_batch * lat_len if last_layer else nt
    tiles_per_k = nt // tm

    def grp(i):
        return jnp.minimum((i * tm) // lat_len, n_batch)

    def choice(k):
        return pl.BlockSpec((tm * SUBLANES, LANES), lambda i: (k * tiles_per_k + i, 0))

    return pl.pallas_call(
        functools.partial(_combine_kernel, last_layer=last_layer),
        out_shape=jax.ShapeDtypeStruct((n_out, d), F32),
        grid=(n_out // tm,),
        in_specs=[pl.BlockSpec((tm, d), lambda i: (i, 0)),
                  pl.BlockSpec((tm, LANES), lambda i: (i, 0)),
                  choice(0), choice(1), choice(2), choice(3),
                  _mod_spec(l, grp, 5, d),
                  pl.BlockSpec((1, d), lambda i: (0, 0))],
        out_specs=pl.BlockSpec((tm, d), lambda i: (i, 0)),
        compiler_params=_cparams(("arbitrary",)),
        name="moe_combine",
    )(r1, gates, ya, ya, ya, ya, mods4, final_g.reshape(1, d))


def _routing_plan(top_idx, rank, counts, n_blocks):
    n_tok = top_idx.shape[0]
    n_assign = top_idx.size
    n_rows = n_blocks * MOE_BLOCK
    flat_e = top_idx.reshape(-1)
    counts = counts.astype(I32)
    padded = (counts + MOE_BLOCK - 1) // MOE_BLOCK * MOE_BLOCK
    pad_end = jnp.cumsum(padded)
    pad_start = pad_end - padded
    count_end = jnp.cumsum(counts)
    dest = pad_start[flat_e] + rank.reshape(-1)
    row_assign = jnp.full((n_rows,), -1, I32).at[dest].set(jnp.arange(n_assign, dtype=I32),
                                                           unique_indices=True, mode='drop')
    blk_start = jnp.arange(n_blocks, dtype=I32) * MOE_BLOCK

    def expert_at(pos):
        return jnp.minimum(jnp.sum((pad_end[None, :] <= pos[:, None]).astype(I32), axis=1), N_EXPERTS - 1)

    block_e = expert_at(blk_start)
    used = blk_start < pad_end[-1]
    n_valid = jnp.where(used, jnp.clip(pad_start[block_e] + counts[block_e] - blk_start, 0, MOE_BLOCK), 0)
    first = jnp.logical_and(used, blk_start == pad_start[block_e]).astype(I32)
    nxt_start = pad_end[block_e]
    nxt = jnp.where(nxt_start < pad_end[-1], expert_at(nxt_start), -1).astype(I32)

    row = jnp.arange(n_rows, dtype=I32)
    empty = row_assign < 0
    dump = n_assign + row - jnp.repeat(count_end[block_e], MOE_BLOCK)
    tok = jnp.minimum(row_assign >> 2, n_tok - 1)
    gather_row = jnp.where(empty, 0, tok) * SUBLANES
    scatter_row = jnp.where(empty, dump, (row_assign & 3) * n_tok + (row_assign >> 2)) * SUBLANES
    return gather_row.astype(I32), scatter_row.astype(I32), block_e.astype(I32), n_valid.astype(I32), first, nxt


def _forward(x, c, ctx, c_ctx, ada_w, ada_b, norm1_g, w_in, ssd_conv_w, ssd_conv_b, ssd_a_log,
             ssd_dt_bias, ssd_d, ssd_norm_g, conf_dw_w, conf_dw_b, conf_ln_g, conf_ln_b, conf_pw_w,
             conf_pw_b, conf_out_g, s5_lam_re, s5_lam_im, s5_log_step, s5_b_re, s5_b_im, s5_c_re,
             s5_c_im, s5_d, s5_glu_w, s5_glu_b, s5_out_g, w_out, norm2_g, router_w, router_b,
             w_gate_up, b_gate_up, w_down, b_down, final_norm_g):
    n_batch, lat_len, d = x.shape
    ctx_len = ctx.shape[1]
    depth = ada_w.shape[0]
    n_lat = n_batch * lat_len
    nt = n_lat + n_batch * ctx_len
    assert d == D_MODEL and ctx_len == ROW_TILE and lat_len % MM_TILE_M == 0
    assert lat_len // GRID_W == GRID_W

    r = jnp.concatenate([x.reshape(n_lat, d), ctx.reshape(n_batch * ctx_len, d)], axis=0).astype(F32)
    cond = jnp.zeros((SUBLANES, d), F32).at[:n_batch].set(c).at[n_batch].set(c_ctx)
    mods4 = _modulation(cond, ada_w, ada_b).reshape(depth, SUBLANES, 1, 6 * d)

    c_dt = SSD_INNER + SSD_CONV_DIM
    c_conf = c_dt + SSD_HEADS
    c_s5 = c_conf + 2 * CONF_DIM
    w_main = jnp.concatenate([w_in[:, :, :c_dt], w_in[:, :, c_conf:c_s5]], axis=2).astype(BF16)
    w_side = jnp.concatenate([w_in[:, :, c_s5:], w_in[:, :, c_dt:c_conf],
                              jnp.zeros((depth, d, LANES - SSD_HEADS), w_in.dtype)], axis=2).astype(BF16)
    w_out_bf = w_out.astype(BF16)
    conf_pw_bf = conf_pw_w.astype(BF16)
    s5_glu_bf = s5_glu_w.astype(BF16)
    head_pad = lambda a: jnp.pad(a.astype(F32), ((0, 0), (0, 0), (0, LANES - SSD_HEADS))).reshape(
        depth * 2, 1, LANES)
    dtb_all, alog_all = head_pad(ssd_dt_bias), head_pad(ssd_a_log)
    d_skip_all = jnp.repeat(ssd_d.astype(F32), SSD_HEAD_DIM, axis=1)
    router_w_pad = jnp.pad(router_w.astype(F32), ((0, 0), (0, 0), (0, LANES - N_EXPERTS)))
    router_b_pad = jnp.pad(router_b.astype(F32), ((0, 0), (0, LANES - N_EXPERTS)), constant_values=-1e30)
    flat2 = lambda a: a.reshape((depth * 2,) + a.shape[2:])
    s5_ops = jax.vmap(_s5_operands)(*[flat2(a) for a in (s5_lam_re, s5_lam_im, s5_log_step, s5_b_re,
                                                         s5_b_im, s5_c_re, s5_c_im)])

    n_blocks = nt * TOP_K // MOE_BLOCK + N_EXPERTS
    s5_zero = jnp.zeros((n_batch, 1, 2 * S5_NSTATE), F32)

    for l in range(depth):
        p_main = _norm_matmul(r, norm1_g, mods4, l, w_main, lat_len, n_batch, "in_proj_main", (BF16,))
        p_s5, p_dt = _norm_matmul(r, norm1_g, mods4, l, w_side, lat_len, n_batch, "in_proj_side",
                                  (F32, F32), splits=(S5_DIM, LANES), grid_rows=(0,))

        xbc = _ssd_conv(p_main, ssd_conv_w, ssd_conv_b, l, lat_len, ctx_len, n_batch)
        y_fwd = _ssd_scan(xbc, p_main, p_dt, dtb_all, alog_all, l, lat_len, ctx_len, n_batch, False)
        mix_a = _ssd_scan(xbc, p_main, p_dt, dtb_all, alog_all, l, lat_len, ctx_len, n_batch, True,
                          (y_fwd, d_skip_all, ssd_norm_g))

        mix_b = _conformer(p_main, conf_dw_w, conf_dw_b, conf_ln_g, conf_ln_b, conf_pw_bf, conf_pw_b,
                           conf_out_g, l, lat_len, ctx_len, n_batch)

        yc_f, st_f = _s5_scan(p_s5, s5_ops, 2 * l, s5_zero, n_batch, lat_len, False, False)
        yl_f, _ = _s5_scan(p_s5, s5_ops, 2 * l, st_f, n_batch, lat_len, False, True)
        fin = (l, s5_d, s5_glu_bf, s5_glu_b, s5_out_g)
        mc_c, st_r = _s5_scan(p_s5, s5_ops, 2 * l + 1, s5_zero, n_batch, lat_len, True, False, (yc_f,) + fin)
        mc_l, _ = _s5_scan(p_s5, s5_ops, 2 * l + 1, st_r, n_batch, lat_len, True, True, (yl_f,) + fin)

        r1, h_tiles, top_idx, gates, rank, counts = _mix_out(
            r, mix_a, mix_b, mc_l, mc_c, w_out_bf, mods4, norm2_g, router_w_pad,
            router_b_pad, l, lat_len, n_batch)

        gather_row, scatter_row, block_e, n_valid, first, nxt = _routing_plan(
            top_idx[:, :TOP_K], rank[:, :TOP_K], counts[0, :N_EXPERTS], n_blocks)
        ya = _experts(h_tiles, gather_row, scatter_row, block_e, n_valid, first, nxt, w_gate_up, b_gate_up,
                      w_down, b_down, l)
        r = _combine(r1, gates, ya, mods4, final_norm_g, l, lat_len, n_batch, l == depth - 1)

    return r.reshape(n_batch, lat_len, d).astype(x.dtype)


def kernel(x, c, ctx, c_ctx, ada_w, ada_b, norm1_g, w_in, ssd_conv_w, ssd_conv_b, ssd_a_log, ssd_dt_bias,
           ssd_d, ssd_norm_g, conf_dw_w, conf_dw_b, conf_ln_g, conf_ln_b, conf_pw_w, conf_pw_b, conf_out_g,
           s5_lam_re, s5_lam_im, s5_log_step, s5_b_re, s5_b_im, s5_c_re, s5_c_im, s5_d, s5_glu_w, s5_glu_b,
           s5_out_g, w_out, norm2_g, router_w, router_b, w_gate_up, b_gate_up, w_down, b_down, final_norm_g):
    return _forward(x, c, ctx, c_ctx, ada_w, ada_b, norm1_g, w_in, ssd_conv_w, ssd_conv_b, ssd_a_log,
                    ssd_dt_bias, ssd_d, ssd_norm_g, conf_dw_w, conf_dw_b, conf_ln_g, conf_ln_b, conf_pw_w,
                    conf_pw_b, conf_out_g, s5_lam_re, s5_lam_im, s5_log_step, s5_b_re, s5_b_im, s5_c_re,
                    s5_c_im, s5_d, s5_glu_w, s5_glu_b, s5_out_g, w_out, norm2_g, router_w, router_b,
                    w_gate_up, b_gate_up, w_down, b_down, final_norm_g)
```

```python
import functools
import math

import jax
import jax.numpy as jnp
from jax import lax
from jax.experimental import pallas as pl
from jax.experimental.pallas import tpu as pltpu

F32 = jnp.float32
BF16 = jnp.bfloat16
I32 = jnp.int32
U32 = jnp.uint32

D_MODEL = 2048
GRID_W = 64
SSD_INNER = 1024
SSD_HEAD_DIM = 64
SSD_HEADS = 16
SSD_GROUPS = 4
SSD_STATE = 128
SSD_CONV = 5
SSD_CHUNK = 128
SSD_CONV_DIM = SSD_INNER + 2 * SSD_GROUPS * SSD_STATE
CONF_DIM = 512
CONF_KERNEL = 31
S5_DIM = 512
S5_GROUP = 16
S5_GROUPS = 32
S5_STATE = 64
S5_NSTATE = S5_GROUPS * S5_STATE
N_EXPERTS = 32
TOP_K = 4
D_EXPERT = 768
SWIGLU_LIMIT = 7.0
SWIGLU_ALPHA = 1.702
MOE_BLOCK = 256
EPS = 1e-6

LANES = 128
SUBLANES = 8
ROW_TILE = 256
MM_TILE_M = 512
MM_TILE_N = 1024
MAIN_COLS = 4096
COL_Z, COL_X, COL_CONF = 0, 1024, 3072
SIDE_COLS = S5_DIM + LANES
CONV_HALO = 16
CONF_HALO = 16
S5_SEG = 32
S5_WIN = SUBLANES * S5_SEG
VMEM_LIMIT = 56 * 1024 * 1024


def _cparams(sem, vmem=VMEM_LIMIT):
    return pltpu.CompilerParams(dimension_semantics=sem, vmem_limit_bytes=vmem)


def _silu(v):
    return v * jax.nn.sigmoid(v)


def _split3(v):
    hi = v.astype(BF16)
    r1 = v - hi.astype(F32)
    mid = r1.astype(BF16)
    lo = (r1 - mid.astype(F32)).astype(BF16)
    return hi, mid, lo


def _dot(a, b):
    return jnp.dot(a, b, preferred_element_type=F32)


def _layer_rows(arr, l):
    depth = arr.shape[0]
    a3 = arr.reshape(depth, 1, -1)
    return a3, pl.BlockSpec((None, 1, a3.shape[2]), lambda *_: (l, 0, 0))


def _layer_mat(arr, l):
    return arr, pl.BlockSpec((None,) + arr.shape[1:], lambda *_: (l, 0, 0))


def _mod_kernel(c_ref, w_ref, b_ref, o_ref):
    s = _silu(c_ref[...])
    w = w_ref[...]
    s_hi = s.astype(BF16)
    s_lo = (s - s_hi.astype(F32)).astype(BF16)
    w_hi = w.astype(BF16)
    w_lo = (w - w_hi.astype(F32)).astype(BF16)
    acc = _dot(s_hi, w_hi) + _dot(s_lo, w_hi) + _dot(s_hi, w_lo)
    o_ref[...] = acc + b_ref[...]


def _modulation(cond, ada_w, ada_b):
    depth, d, n = ada_w.shape
    tn = 1024
    return pl.pallas_call(
        _mod_kernel,
        out_shape=jax.ShapeDtypeStruct((depth, SUBLANES, n), F32),
        grid=(depth, n // tn),
        in_specs=[pl.BlockSpec((SUBLANES, d), lambda l, j: (0, 0)),
                  pl.BlockSpec((None, d, tn), lambda l, j: (l, 0, j)),
                  pl.BlockSpec((None, 1, tn), lambda l, j: (l, 0, j))],
        out_specs=pl.BlockSpec((None, SUBLANES, tn), lambda l, j: (l, 0, j)),
        compiler_params=_cparams(("arbitrary", "arbitrary")),
        name="adaln_mod",
    )(cond, ada_w, ada_b.reshape(depth, 1, n))


def _mod_spec(l, grp, k, d):
    return pl.BlockSpec((None, None, 1, d), lambda i, *_: (l, grp(i), 0, k))


def _normmm_kernel(x_ref, g_ref, sh_ref, sc_ref, w_ref, *rest, n_tiles):
    o_refs, hn_ref = rest[:-1], rest[-1]
    s = pl.program_id(2)
    live = 2 * pl.program_id(0) + s < n_tiles

    @pl.when(jnp.logical_and(pl.program_id(1) == 0, live))
    def _():
        x = x_ref[...]
        ms = jnp.mean(x * x, axis=-1, keepdims=True)
        y = x * lax.rsqrt(ms + EPS) * g_ref[...]
        hn_ref[s] = (y * (1.0 + sc_ref[...]) + sh_ref[...]).astype(BF16)

    @pl.when(live)
    def _():
        res = _dot(hn_ref[s], w_ref[...])
        col = 0
        for o_ref in o_refs:
            wd = o_ref.shape[-1]
            o_ref[...] = res[:, col:col + wd].reshape(o_ref.shape).astype(o_ref.dtype)
            col += wd


def _norm_matmul(r, gains, mods4, l, w_all, lat_len, n_batch, name, out_dtypes, splits=None,
                 grid_rows=()):
    nt, d = r.shape
    n = w_all.shape[2]
    tm = MM_TILE_M if nt % MM_TILE_M == 0 else ROW_TILE
    tn = n if splits else MM_TILE_N
    assert n % tn == 0
    widths = splits or (tn,)
    n_tiles = nt // tm
    last = n_tiles - 1

    def tile(p, j, s):
        return jnp.minimum(2 * p + s, last)

    def x_tile(p, j, s):
        return jnp.where(j == 0, tile(p, j, s), jnp.minimum(2 * p + 1, last))

    def grp(t):
        return jnp.minimum((t * tm) // lat_len, n_batch)

    def mod_spec(k):
        return pl.BlockSpec((None, None, 1, d), lambda p, j, s: (l, grp(tile(p, j, s)), 0, k))

    g3, g_spec = _layer_rows(gains, l)
    out_shapes, out_specs = [], []
    for k, (wd, dt) in enumerate(zip(widths, out_dtypes)):
        if k in grid_rows:
            out_shapes.append(jax.ShapeDtypeStruct((nt // GRID_W, GRID_W, wd), dt))
            out_specs.append(pl.BlockSpec((tm // GRID_W, GRID_W, wd), lambda p, j, s: (tile(p, j, s), 0, 0)))
        else:
            out_shapes.append(jax.ShapeDtypeStruct((nt, n if not splits else wd), dt))
            out_specs.append(pl.BlockSpec((tm, wd), lambda p, j, s: (tile(p, j, s), j)))
    outs = pl.pallas_call(
        functools.partial(_normmm_kernel, n_tiles=n_tiles),
        out_shape=tuple(out_shapes),
        grid=(pl.cdiv(n_tiles, 2), n // tn, 2),
        in_specs=[pl.BlockSpec((tm, d), lambda p, j, s: (x_tile(p, j, s), 0)),
                  g_spec, mod_spec(0), mod_spec(1),
                  pl.BlockSpec((None, d, tn), lambda p, j, s: (l, 0, j))],
        out_specs=tuple(out_specs),
        scratch_shapes=[pltpu.VMEM((2, tm, d), BF16)],
        compiler_params=_cparams(("arbitrary", "arbitrary", "arbitrary")),
        name=name,
    )(r, g3, mods4, mods4, w_all)
    return outs if splits else outs[0]


def _seq_edges(i, tiles_per_lat_seq, n_lat_tiles):
    is_lat = i < n_lat_tiles
    first = jnp.logical_or(jnp.logical_not(is_lat), (i % tiles_per_lat_seq) == 0)
    last = jnp.logical_or(jnp.logical_not(is_lat), (i % tiles_per_lat_seq) == tiles_per_lat_seq - 1)
    return first, last


def _conv5_kernel(cur_ref, prev_ref, next_ref, w_ref, b_ref, o_ref, ext_ref, *, tiles_per_lat_seq,
                  n_lat_tiles):
    i = pl.program_id(0)
    first, last = _seq_edges(i, tiles_per_lat_seq, n_lat_tiles)
    h, tm = CONV_HALO, ROW_TILE
    ext_ref[0:h, :] = jnp.where(first, 0.0, prev_ref[...].astype(F32))
    ext_ref[h:h + tm, :] = cur_ref[...].astype(F32)
    ext_ref[h + tm:h + tm + h, :] = jnp.where(last, 0.0, next_ref[...].astype(F32))
    pad = (SSD_CONV - 1) // 2
    cw = 512
    for c in range(0, ext_ref.shape[1], cw):
        acc = jnp.broadcast_to(b_ref[:, c:c + cw], (tm, cw))
        for j in range(SSD_CONV):
            acc = acc + w_ref[j:j + 1, c:c + cw] * ext_ref[h - pad + j:h - pad + j + tm, c:c + cw]
        o_ref[:, c:c + cw] = _silu(acc).astype(BF16)


def _ssd_conv(p_main, conv_w, conv_b, l, lat_len, ctx_len, n_batch):
    nt = p_main.shape[0]
    tm, h = ROW_TILE, CONV_HALO
    assert ctx_len == tm and lat_len % tm == 0
    n_lat_tiles = n_batch * lat_len // tm
    cw = SSD_CONV_DIM // 2
    xblk = COL_X // cw
    nhb = nt // h
    depth = conv_w.shape[0]
    kern = functools.partial(_conv5_kernel, tiles_per_lat_seq=lat_len // tm, n_lat_tiles=n_lat_tiles)
    return pl.pallas_call(
        kern,
        out_shape=jax.ShapeDtypeStruct((nt, SSD_CONV_DIM), BF16),
        grid=(nt // tm, 2),
        in_specs=[pl.BlockSpec((tm, cw), lambda i, j: (i, xblk + j)),
                  pl.BlockSpec((h, cw), lambda i, j: (jnp.maximum(i * (tm // h) - 1, 0), xblk + j)),
                  pl.BlockSpec((h, cw), lambda i, j: (jnp.minimum((i + 1) * (tm // h), nhb - 1), xblk + j)),
                  pl.BlockSpec((None, SSD_CONV, cw), lambda i, j: (l, 0, j)),
                  pl.BlockSpec((None, 1, cw), lambda i, j: (l, 0, j))],
        out_specs=pl.BlockSpec((tm, cw), lambda i, j: (i, j)),
        scratch_shapes=[pltpu.VMEM((tm + 2 * h, cw), F32)],
        compiler_params=_cparams(("arbitrary", "arbitrary")),
        name="ssd_conv",
    )(p_main, p_main, p_main, conv_w, conv_b.reshape(depth, 1, -1))


def _head_cols(vals, width):
    lane = lax.broadcasted_iota(I32, (1, LANES), 1)
    halves = []
    per_half = LANES // width
    for hh in range(len(vals) // per_half):
        sel = vals[hh * per_half + per_half - 1]
        for k in range(per_half - 2, -1, -1):
            sel = jnp.where(lane < (k + 1) * width, vals[hh * per_half + k], sel)
        halves.append(sel)
    return jnp.concatenate(halves, axis=1)


def _ssd_kernel(*refs, reverse, final):
    if final:
        (x_ref, b_ref, c_ref, dt_ref, dtb_ref, alog_ref, hx_ref, yprev_ref, z_ref, dskip_ref, ng_ref,
         o_ref, state_ref, tmp_ref) = refs
    else:
        x_ref, b_ref, c_ref, dt_ref, dtb_ref, alog_ref, hx_ref, o_ref, state_ref = refs
    q = SSD_CHUNK
    r = SSD_HEADS // SSD_GROUPS
    gw = r * SSD_HEAD_DIM

    @pl.when(pl.program_id(1) == 0)
    def _():
        state_ref[...] = jnp.zeros_like(state_ref)

    lane = lax.broadcasted_iota(I32, (1, LANES), 1)
    dtp = jax.nn.softplus(dt_ref[...] + dtb_ref[...])
    a_head = -jnp.exp(alog_ref[...])
    a = jnp.where(lane < SSD_HEADS, dtp * a_head, 0.0)
    ri = lax.broadcasted_iota(I32, (q, q), 0)
    ci = lax.broadcasted_iota(I32, (q, q), 1)
    tri = (ci >= ri) if reverse else (ci <= ri)
    tri_b = jnp.where(tri, 1.0, 0.0).astype(BF16)
    a_hi, a_mid, a_lo = _split3(a)
    a_cs = _dot(tri_b, a_hi) + _dot(tri_b, a_mid) + _dot(tri_b, a_lo)
    a_cs_t = a_cs.T
    dtp_t = dtp.T
    a_end = a_cs[0:1, :] if reverse else a_cs[q - 1:q, :]
    lane_g = lax.broadcasted_iota(I32, (1, gw), 1)

    pieces = []
    for fac in (jnp.exp(a_cs), dtp * jnp.exp(a_end - a_cs)):
        hi = fac.astype(BF16)
        pieces += [hi, (fac - hi.astype(F32)).astype(BF16)]
    spread = _dot(jnp.concatenate(pieces, axis=0), hx_ref[...])
    e_all = spread[0:q] + spread[q:2 * q]
    w_all = spread[2 * q:3 * q] + spread[3 * q:4 * q]

    x = x_ref[...]
    for g in range(SSD_GROUPS):
        cg = c_ref[:, g * SSD_STATE:(g + 1) * SSD_STATE]
        bg = b_ref[:, g * SSD_STATE:(g + 1) * SSD_STATE]
        cb = lax.dot_general(cg, bg, (((1,), (1,)), ((), ())), preferred_element_type=F32)
        xg = x[:, g * gw:(g + 1) * gw]
        yg = jnp.zeros((q, gw), F32)
        dec = []
        for hl in range(r):
            h = g * r + hl
            col = a_cs[:, h:h + 1]
            row = a_cs_t[h:h + 1, :]
            lm = jnp.where(tri, jnp.exp(col - row), 0.0)
            m = (cb * lm * dtp_t[h:h + 1, :]).astype(BF16)
            in_head = jnp.logical_and(lane_g >= hl * SSD_HEAD_DIM, lane_g < (hl + 1) * SSD_HEAD_DIM)
            xm = jnp.where(in_head, xg, jnp.zeros_like(xg))
            yg = yg + _dot(m, xm)
            dec.append(jnp.exp(a_end[:, h:h + 1]))
        s_old = state_ref[g]
        gs = slice(g * gw, (g + 1) * gw)
        yg = yg + e_all[:, gs] * _dot(cg, s_old.astype(BF16))
        xw = (xg.astype(F32) * w_all[:, gs]).astype(BF16)
        upd = lax.dot_general(bg, xw, (((0,), (0,)), ((), ())), preferred_element_type=F32)
        state_ref[g] = _head_cols(dec, SSD_HEAD_DIM) * s_old + upd
        sl = slice(g * gw, (g + 1) * gw)
        if final:
            ytot = yprev_ref[:, sl] + yg + dskip_ref[:, sl] * xg.astype(F32)
            tmp_ref[:, sl] = ytot * _silu(z_ref[:, sl].astype(F32))
        else:
            o_ref[:, sl] = yg
    if final:
        gated = tmp_ref[...]
        ms = jnp.mean(gated * gated, axis=-1, keepdims=True)
        o_ref[...] = (gated * lax.rsqrt(ms + EPS) * ng_ref[...]).astype(o_ref.dtype)


def _ssd_scan(xbc, p_main, p_dt, dtb_all, alog_all, l, lat_len, ctx_len, n_batch, reverse, final_args=None):
    nt = xbc.shape[0]
    q = SSD_CHUNK
    ncl, ncc = lat_len // q, ctx_len // q
    ctx0 = n_batch * ncl
    dd = 1 if reverse else 0

    def blk(b, j):
        if reverse:
            return jnp.where(j < ncc, ctx0 + b * ncc + (ncc - 1 - j), b * ncl + (ncl - 1 - (j - ncc)))
        return jnp.where(j < ncc, ctx0 + b * ncc + j, b * ncl + (j - ncc))

    head_vec = pl.BlockSpec((None, 1, LANES), lambda b, j: (2 * l + dd, 0, 0))
    final = final_args is not None
    in_specs = [pl.BlockSpec((q, SSD_INNER), lambda b, j: (blk(b, j), 0)),
                pl.BlockSpec((q, SSD_GROUPS * SSD_STATE), lambda b, j: (blk(b, j), 2)),
                pl.BlockSpec((q, SSD_GROUPS * SSD_STATE), lambda b, j: (blk(b, j), 3)),
                pl.BlockSpec((q, LANES), lambda b, j: (blk(b, j), 0)),
                head_vec, head_vec,
                pl.BlockSpec((LANES, SSD_INNER), lambda b, j: (0, 0))]
    head_spread = (jnp.arange(SSD_INNER, dtype=I32)[None, :] // SSD_HEAD_DIM
                   == jnp.arange(LANES, dtype=I32)[:, None]).astype(BF16)
    args = [xbc, xbc, xbc, p_dt, dtb_all, alog_all, head_spread]
    scratch = [pltpu.VMEM((SSD_GROUPS, SSD_STATE, (SSD_HEADS // SSD_GROUPS) * SSD_HEAD_DIM), F32)]
    if final:
        y_prev, d_skip_all, norm_g_all = final_args
        ds3, ds_spec = _layer_rows(d_skip_all, l)
        ng3, ng_spec = _layer_rows(norm_g_all, l)
        in_specs += [pl.BlockSpec((q, SSD_INNER), lambda b, j: (blk(b, j), 0)),
                     pl.BlockSpec((q, SSD_INNER), lambda b, j: (blk(b, j), COL_Z // SSD_INNER)),
                     ds_spec, ng_spec]
        args += [y_prev, p_main, ds3, ng3]
        scratch.append(pltpu.VMEM((q, SSD_INNER), F32))
    return pl.pallas_call(
        functools.partial(_ssd_kernel, reverse=reverse, final=final),
        out_shape=jax.ShapeDtypeStruct((nt, SSD_INNER), BF16 if final else F32),
        grid=(n_batch, ncc + ncl),
        in_specs=in_specs,
        out_specs=pl.BlockSpec((q, SSD_INNER), lambda b, j: (blk(b, j), 0)),
        scratch_shapes=scratch,
        compiler_params=_cparams(("arbitrary", "arbitrary")),
        name="ssd_scan_rev" if reverse else "ssd_scan_fwd",
    )(*args)


def _glu(ref):
    v = ref[:, 0:CONF_DIM].astype(F32)
    gt = ref[:, CONF_DIM:2 * CONF_DIM].astype(F32)
    return v * jax.nn.sigmoid(gt)


def _conf_kernel(cur_ref, prev_ref, next_ref, dww_ref, dwb_ref, lng_ref, lnb_ref, pww_ref, pwb_ref,
                 og_ref, o_ref, ext_ref, acc_ref, sh_ref, *, tiles_per_lat_seq, n_lat_tiles):
    i = pl.program_id(0)
    first, last = _seq_edges(i, tiles_per_lat_seq, n_lat_tiles)
    h, tm = CONF_HALO, ROW_TILE
    ext_ref[0:h, :] = jnp.where(first, 0.0, _glu(prev_ref))
    ext_ref[h:h + tm, :] = _glu(cur_ref)
    ext_ref[h + tm:h + tm + h, :] = jnp.where(last, 0.0, _glu(next_ref))
    pad = (CONF_KERNEL - 1) // 2
    span = (CONF_KERNEL - 1) // SUBLANES * SUBLANES + tm
    for b in range(SUBLANES):
        sh_ref[b] = ext_ref[h - pad + b:h - pad + b + span, :]
    cw = 256
    for c in range(0, CONF_DIM, cw):
        acc = jnp.broadcast_to(dwb_ref[:, c:c + cw], (tm, cw))
        for j in range(CONF_KERNEL):
            a8 = j // SUBLANES * SUBLANES
            acc = acc + dww_ref[j:j + 1, c:c + cw] * sh_ref[j % SUBLANES, a8:a8 + tm, c:c + cw]
        acc_ref[:, c:c + cw] = acc
    u = acc_ref[...]
    mu = jnp.mean(u, axis=-1, keepdims=True)
    var = jnp.mean(jnp.square(u - mu), axis=-1, keepdims=True)
    y = (u - mu) * lax.rsqrt(var + EPS) * lng_ref[...] + lnb_ref[...]
    y = _silu(y)
    v = _dot(y.astype(BF16), pww_ref[...]) + pwb_ref[...]
    ms = jnp.mean(v * v, axis=-1, keepdims=True)
    o_ref[...] = (v * lax.rsqrt(ms + EPS) * og_ref[...]).astype(o_ref.dtype)


def _conformer(p_main, dw_w, dw_b, ln_g, ln_b, pw_w_bf, pw_b, out_g, l, lat_len, ctx_len, n_batch):
    nt = p_main.shape[0]
    tm, h = ROW_TILE, CONF_HALO
    n_lat_tiles = n_batch * lat_len // tm
    cblk = COL_CONF // (2 * CONF_DIM)
    nhb = nt // h
    kern = functools.partial(_conf_kernel, tiles_per_lat_seq=lat_len // tm, n_lat_tiles=n_lat_tiles)
    rows = [_layer_rows(a, l) for a in (dw_b, ln_g, ln_b)]
    rows2 = [_layer_rows(a, l) for a in (pw_b, out_g)]
    dww, dww_spec = _layer_mat(dw_w, l)
    pww, pww_spec = _layer_mat(pw_w_bf, l)
    return pl.pallas_call(
        kern,
        out_shape=jax.ShapeDtypeStruct((nt, CONF_DIM), BF16),
        grid=(nt // tm,),
        in_specs=[pl.BlockSpec((tm, 2 * CONF_DIM), lambda i: (i, cblk)),
                  pl.BlockSpec((h, 2 * CONF_DIM), lambda i: (jnp.maximum(i * (tm // h) - 1, 0), cblk)),
                  pl.BlockSpec((h, 2 * CONF_DIM), lambda i: (jnp.minimum((i + 1) * (tm // h), nhb - 1), cblk)),
                  dww_spec] + [s for _, s in rows] + [pww_spec] + [s for _, s in rows2],
        out_specs=pl.BlockSpec((tm, CONF_DIM), lambda i: (i, 0)),
        scratch_shapes=[pltpu.VMEM((tm + 2 * h, CONF_DIM), F32), pltpu.VMEM((tm, CONF_DIM), F32),
                        pltpu.VMEM((SUBLANES, (CONF_KERNEL - 1) // SUBLANES * SUBLANES + tm, CONF_DIM), F32)],
        compiler_params=_cparams(("arbitrary",)),
        name="conformer",
    )(p_main, p_main, p_main, dww, *[a for a, _ in rows], pww, *[a for a, _ in rows2])


def _gelu_tanh(v):
    return 0.5 * v * (1.0 + jnp.tanh(math.sqrt(2.0 / math.pi) * (v + 0.044715 * (v * v * v))))


def _s5_kernel(*refs, reverse, final, colmajor, seg):
    if final:
        (u_ref, yprev_ref, bblk_ref, cblk_ref, lam_ref, sin_ref, dskip_ref, gw_ref, gb_ref, og_ref,
         o_ref, sout_ref, h_ref, fin_ref, init_ref, carry_ref, *perm_refs) = refs
    else:
        (u_ref, bblk_ref, cblk_ref, lam_ref, sin_ref,
         o_ref, sout_ref, h_ref, fin_ref, init_ref, carry_ref, *perm_refs) = refs
    ns = S5_NSTATE
    nsub = SUBLANES
    win = seg * nsub
    jw = pl.program_id(1)

    @pl.when(jw == 0)
    def _():
        carry_ref[...] = sin_ref[...]

    n_lb = S5_DIM // LANES

    per_row = GRID_W // seg

    def sub_seg(s):
        return s // per_row, slice((s % per_row) * seg, (s % per_row + 1) * seg)

    def permuted(tile_ref, buf_ref):
        for s in range(nsub):
            g, rows = sub_seg(s)
            for k in range(n_lb):
                buf_ref[k, pl.ds(s, seg, stride=nsub), :] = tile_ref[g, rows, k * LANES:(k + 1) * LANES]
        return jnp.concatenate([buf_ref[k] for k in range(n_lb)], axis=1)

    if colmajor:
        u_win = u_ref[...].reshape(win, S5_DIM)
        prev_win = yprev_ref[...].reshape(win, S5_DIM) if final else None
    else:
        u_win = permuted(u_ref, perm_refs[0])
        prev_win = permuted(yprev_ref, perm_refs[1]) if final else None

    u_bf = u_win.astype(BF16)
    ch_per_tile = 2 * LANES // S5_STATE * S5_GROUP
    for j in range(2 * ns // (2 * LANES)):
        c0 = (j * ch_per_tile) % S5_DIM // LANES * LANES
        h_ref[:, j * 2 * LANES:(j + 1) * 2 * LANES] = _dot(
            u_bf[:, c0:c0 + LANES], bblk_ref[c0:c0 + LANES, j * 2 * LANES:(j + 1) * 2 * LANES])

    cw = 512
    n_chunks = ns // cw

    def lam_chunk(row, c):
        return (jnp.broadcast_to(lam_ref[row:row + 1, c * cw:(c + 1) * cw], (nsub, cw)),
                jnp.broadcast_to(lam_ref[row + 1:row + 2, c * cw:(c + 1) * cw], (nsub, cw)))

    def row0(i):
        step = (seg - 1 - i) if reverse else i
        return pl.multiple_of(step * nsub, nsub)

    for c in range(n_chunks):
        lre, lim = lam_chunk(0, c)
        cre = slice(c * cw, (c + 1) * cw)
        cim = slice(ns + c * cw, ns + (c + 1) * cw)

        def step1(i, hc, cre=cre, cim=cim, lre=lre, lim=lim):
            hre, him = hc
            r0 = row0(i)
            nre = lre * hre - lim * him + h_ref[pl.ds(r0, nsub), cre]
            nim = lre * him + lim * hre + h_ref[pl.ds(r0, nsub), cim]
            h_ref[pl.ds(r0, nsub), cre] = nre
            h_ref[pl.ds(r0, nsub), cim] = nim
            return nre, nim

        z0 = jnp.zeros((nsub, cw), F32)
        fre, fim = lax.fori_loop(0, seg, step1, (z0, z0), unroll=4)
        fin_ref[:, cre] = fre
        fin_ref[:, cim] = fim

    seg_row = {32: 2, 64: 4}[seg]
    gre, gim = lam_ref[seg_row:seg_row + 1, :], lam_ref[seg_row + 1:seg_row + 2, :]
    cur_re, cur_im = carry_ref[:, 0:ns], carry_ref[:, ns:2 * ns]
    order = range(nsub - 1, -1, -1) if reverse else range(nsub)
    for s in order:
        init_ref[s:s + 1, 0:ns] = cur_re
        init_ref[s:s + 1, ns:2 * ns] = cur_im
        f_re, f_im = fin_ref[s:s + 1, 0:ns], fin_ref[s:s + 1, ns:2 * ns]
        cur_re, cur_im = gre * cur_re - gim * cur_im + f_re, gre * cur_im + gim * cur_re + f_im
    carry_ref[:, 0:ns] = cur_re
    carry_ref[:, ns:2 * ns] = cur_im

    for c in range(n_chunks):
        lre, lim = lam_chunk(0, c)
        cre = slice(c * cw, (c + 1) * cw)
        cim = slice(ns + c * cw, ns + (c + 1) * cw)

        def step2(i, gc, cre=cre, cim=cim, lre=lre, lim=lim):
            g_re, g_im = gc
            n_re = lre * g_re - lim * g_im
            n_im = lre * g_im + lim * g_re
            r0 = row0(i)
            h_ref[pl.ds(r0, nsub), cre] = h_ref[pl.ds(r0, nsub), cre] + n_re
            h_ref[pl.ds(r0, nsub), cim] = h_ref[pl.ds(r0, nsub), cim] + n_im
            return n_re, n_im

        lax.fori_loop(0, seg, step2, (init_ref[:, cre], init_ref[:, cim]), unroll=4)

    halves = []
    st_per_tile = 2 * LANES // S5_GROUP * S5_STATE
    for n in range(S5_DIM // (2 * LANES)):
        oc = slice(n * 2 * LANES, (n + 1) * 2 * LANES)
        s_re = slice(n * st_per_tile, (n + 1) * st_per_tile)
        s_im = slice(ns + n * st_per_tile, ns + (n + 1) * st_per_tile)
        halves.append(_dot(h_ref[:, s_re].astype(BF16), cblk_ref[s_re, oc])
                      + _dot(h_ref[:, s_im].astype(BF16), cblk_ref[s_im, oc]))
    y_win = jnp.concatenate(halves, axis=1)

    if final:
        tot = prev_win + y_win + dskip_ref[...] * u_win
        gl = _gelu_tanh(tot)
        gate = jax.nn.sigmoid(_dot(gl.astype(BF16), gw_ref[...]) + gb_ref[...])
        v = gl * gate
        ms = jnp.mean(v * v, axis=-1, keepdims=True)
        y_win = v * lax.rsqrt(ms + EPS) * og_ref[...]

    if colmajor:
        o_ref[...] = y_win.reshape(o_ref.shape).astype(o_ref.dtype)
    else:
        y_ref = perm_refs[0]
        for k in range(n_lb):
            y_ref[k] = y_win[:, k * LANES:(k + 1) * LANES]
        for s in range(nsub):
            g, rows = sub_seg(s)
            for k in range(n_lb):
                o_ref[g, rows, k * LANES:(k + 1) * LANES] = (
                    y_ref[k, pl.ds(s, seg, stride=nsub), :].astype(o_ref.dtype))

    @pl.when(jw == pl.num_programs(1) - 1)
    def _():
        sout_ref[...] = carry_ref[...]


def _s5_scan(u3d, ops, ld, state_in, n_batch, lat_len, reverse, colmajor, final_args=None):
    ns2 = 2 * S5_NSTATE
    bblk, cblk, lam = ops
    if colmajor:
        seg = GRID_W
        tile = (seg, SUBLANES, S5_DIM)
        n_win = GRID_W // SUBLANES
        out_shape = (n_batch * seg, GRID_W, S5_DIM)
        imap = (lambda b, j: (b, n_win - 1 - j, 0)) if reverse else (lambda b, j: (b, j, 0))
        in_map = imap
    else:
        seg = S5_SEG
        tile = (S5_WIN // GRID_W, GRID_W, S5_DIM)
        n_win = 1
        out_shape = (n_batch * S5_WIN // GRID_W, GRID_W, S5_DIM)
        ctx_blk0 = n_batch * lat_len // S5_WIN
        imap = lambda b, j: (b, 0, 0)
        in_map = lambda b, j: (ctx_blk0 + b, 0, 0)
    win = seg * SUBLANES
    st = pl.BlockSpec((None, 1, ns2), lambda b, j: (b, 0, 0))
    stacked = lambda a: pl.BlockSpec((None,) + a.shape[1:], lambda b, j: (ld, 0, 0))
    final = final_args is not None
    in_specs = [pl.BlockSpec(tile, in_map)]
    args = [u3d]
    if final:
        in_specs.append(pl.BlockSpec(tile, imap))
        args.append(final_args[0])
    in_specs += [stacked(bblk), stacked(cblk), stacked(lam), st]
    args += [bblk, cblk, lam, state_in]
    scratch = [pltpu.VMEM((win, ns2), F32),
               pltpu.VMEM((SUBLANES, ns2), F32), pltpu.VMEM((SUBLANES, ns2), F32),
               pltpu.VMEM((1, ns2), F32)]
    if not colmajor:
        perm = pltpu.VMEM((S5_DIM // LANES, win, LANES), F32)
        scratch += [perm, perm] if final else [perm]
    if final:
        _, l, d_skip, glu_w_bf, glu_b, out_g = final_args
        for a in (d_skip,):
            a3, sp = _layer_rows(a, l)
            in_specs.append(sp)
            args.append(a3)
        gw, gw_spec = _layer_mat(glu_w_bf, l)
        in_specs.append(gw_spec)
        args.append(gw)
        for a in (glu_b, out_g):
            a3, sp = _layer_rows(a, l)
            in_specs.append(sp)
            args.append(a3)
    out, s_out = pl.pallas_call(
        functools.partial(_s5_kernel, reverse=reverse, final=final, colmajor=colmajor, seg=seg),
        out_shape=(jax.ShapeDtypeStruct(out_shape, F32),
                   jax.ShapeDtypeStruct((n_batch, 1, ns2), F32)),
        grid=(n_batch, n_win),
        in_specs=in_specs,
        out_specs=(pl.BlockSpec(tile, imap), st),
        scratch_shapes=scratch,
        compiler_params=_cparams(("arbitrary", "arbitrary")),
        name="s5_" + ("rev" if reverse else "fwd") + ("_lat" if colmajor else "_ctx"),
    )(*args)
    return out, s_out


def _s5_operands(lam_re, lam_im, log_step, b_re, b_im, c_re, c_im):
    g, p, k = S5_GROUPS, S5_STATE, S5_GROUP
    lam = lax.complex(jnp.minimum(lam_re.astype(F32), -1e-4), lam_im.astype(F32))
    step = jnp.exp(log_step.astype(F32))[:, None]
    lam_bar = jnp.exp(lam * step)
    lam_seg = jnp.exp(lam * (step * S5_SEG))
    lam_col = jnp.exp(lam * (step * GRID_W))
    b_bar = ((lam_bar - 1.0) / lam)[..., None] * lax.complex(b_re.astype(F32), b_im.astype(F32))
    eye = jnp.eye(g, dtype=F32)
    bd_in = lambda m: jnp.einsum('gkp,gh->gkhp', jnp.transpose(m, (0, 2, 1)), eye).reshape(g * k, g * p)
    bblk = jnp.concatenate([bd_in(jnp.real(b_bar)), bd_in(jnp.imag(b_bar))], axis=1)
    bd_out = lambda m: jnp.einsum('gpk,gh->gphk', jnp.transpose(m, (0, 2, 1)), eye).reshape(g * p, g * k)
    cblk = jnp.concatenate([bd_out(c_re.astype(F32)), -bd_out(c_im.astype(F32))], axis=0)
    zeros = jnp.zeros((g * p,), F32)
    lam_rows = jnp.stack([jnp.real(lam_bar).reshape(-1), jnp.imag(lam_bar).reshape(-1),
                          jnp.real(lam_seg).reshape(-1), jnp.imag(lam_seg).reshape(-1),
                          jnp.real(lam_col).reshape(-1), jnp.imag(lam_col).reshape(-1), zeros, zeros])
    return bblk.astype(BF16), cblk.astype(BF16), lam_rows


def _pack_bf16_pairs(v):
    n = v.shape[1] // 2
    bits = pltpu.bitcast(v.astype(BF16).astype(F32), U32)
    return (bits[:, :n] >> 16) | (bits[:, n:] & jnp.uint32(0xFFFF0000))


def _unpack_bf16_pairs(w):
    lo = pltpu.bitcast(w << 16, F32)
    hi = pltpu.bitcast(w & jnp.uint32(0xFFFF0000), F32)
    return lo, hi


def _store_token_tiles(ref, v):
    rows = v.shape[0]
    for c in range(SUBLANES):
        ref[pl.ds(c, rows, stride=SUBLANES), :] = v[:, c * LANES:(c + 1) * LANES]


def _load_token_tiles(ref, rows):
    return [ref[pl.ds(c, rows, stride=SUBLANES), :] for c in range(SUBLANES)]


def _mixout_kernel(r_ref, a_ref, b_ref, cl_ref, cc_ref, wa_ref, wb_ref, wc_ref, g1_ref, ng_ref, sh_ref,
                   sc_ref, rw_ref, rb_ref, r1_ref, hp_ref, idx_ref, gate_ref, rank_ref, cnt_ref,
                   carry_ref, *, n_lat_tiles):
    i = pl.program_id(0)

    @pl.when(i == 0)
    def _():
        carry_ref[...] = jnp.zeros_like(carry_ref)

    mix_c = jnp.where(i < n_lat_tiles, cl_ref[...], cc_ref[...]).reshape(a_ref.shape[0], S5_DIM).astype(BF16)
    acc = _dot(a_ref[...], wa_ref[...]) + _dot(b_ref[...], wb_ref[...]) + _dot(mix_c, wc_ref[...])
    x = r_ref[...] + g1_ref[...] * acc
    r1_ref[...] = x
    ms = jnp.mean(x * x, axis=-1, keepdims=True)
    h = x * lax.rsqrt(ms + EPS) * ng_ref[...]
    h = h * (1.0 + sc_ref[...]) + sh_ref[...]
    _store_token_tiles(hp_ref, _pack_bf16_pairs(h))

    h_hi = h.astype(BF16)
    h_lo = (h - h_hi.astype(F32)).astype(BF16)
    rw = rw_ref[...]
    w_hi = rw.astype(BF16)
    w_lo = (rw - w_hi.astype(F32)).astype(BF16)
    logits = _dot(h_hi, w_hi) + _dot(h_lo, w_hi) + _dot(h_hi, w_lo) + rb_ref[...]

    tm = logits.shape[0]
    lane = lax.broadcasted_iota(I32, (tm, LANES), 1)
    lane_f = lane.astype(F32)
    work = logits
    tops, picks = [], []
    for _ in range(TOP_K):
        m = jnp.max(work, axis=-1, keepdims=True)
        pick = jnp.min(jnp.where(work == m, lane_f, float(LANES)), axis=-1, keepdims=True)
        work = jnp.where(lane_f == pick, -jnp.inf, work)
        tops.append(m)
        picks.append(pick)
    exps = [jnp.exp(t - tops[0]) for t in tops]
    denom = exps[0] + exps[1] + exps[2] + exps[3]

    onehot = jnp.zeros((tm, LANES), F32)
    for k in range(TOP_K):
        onehot = onehot + jnp.where(lane_f == picks[k], 1.0, 0.0)
    ri = lax.broadcasted_iota(I32, (tm, tm), 0)
    ci = lax.broadcasted_iota(I32, (tm, tm), 1)
    before = jnp.where(ci < ri, 1.0, 0.0).astype(BF16)
    base = carry_ref[0:1, :] + _dot(before, onehot.astype(BF16))
    carry_ref[...] = carry_ref[...] + jnp.sum(onehot, axis=0, keepdims=True)
    cnt_ref[...] = carry_ref[...]

    idx_out = jnp.zeros((tm, LANES), F32)
    gate_out = jnp.zeros((tm, LANES), F32)
    rank_out = jnp.zeros((tm, LANES), F32)
    for k in range(TOP_K):
        rank_k = jnp.sum(jnp.where(lane_f == picks[k], base, 0.0), axis=-1, keepdims=True)
        idx_out = jnp.where(lane == k, picks[k], idx_out)
        gate_out = jnp.where(lane == k, exps[k] / denom, gate_out)
        rank_out = jnp.where(lane == k, rank_k, rank_out)
    gate_ref[...] = gate_out
    idx_t = idx_out.T.astype(I32)
    rank_t = rank_out.T.astype(I32)
    for k in range(TOP_K):
        for hh in range(tm // LANES):
            row = k * (tm // LANES) + hh
            idx_ref[row:row + 1, :] = idx_t[k:k + 1, hh * LANES:(hh + 1) * LANES]
            rank_ref[row:row + 1, :] = rank_t[k:k + 1, hh * LANES:(hh + 1) * LANES]


def _mix_out(r, mix_a, mix_b, mix_c_lat, mix_c_ctx, w_out_bf, mods4, norm2_g, router_w_pad, router_b_pad,
             l, lat_len, n_batch):
    nt, d = r.shape
    tm = ROW_TILE
    n_lat_tiles = n_batch * lat_len // tm

    def grp(i):
        return jnp.minimum((i * tm) // lat_len, n_batch)

    ng3, ng_spec = _layer_rows(norm2_g, l)
    rb3, rb_spec = _layer_rows(router_b_pad, l)
    tile = lambda w: pl.BlockSpec((tm, w), lambda i: (i, 0))
    dense_rows = tm * TOP_K // LANES
    dense = pl.BlockSpec((dense_rows, LANES), lambda i: (i, 0))
    n_dense = nt * TOP_K // LANES
    return pl.pallas_call(
        functools.partial(_mixout_kernel, n_lat_tiles=n_lat_tiles),
        out_shape=(jax.ShapeDtypeStruct((nt, d), F32), jax.ShapeDtypeStruct((nt * SUBLANES, LANES), U32),
                   jax.ShapeDtypeStruct((n_dense, LANES), I32), jax.ShapeDtypeStruct((nt, LANES), F32),
                   jax.ShapeDtypeStruct((n_dense, LANES), I32), jax.ShapeDtypeStruct((SUBLANES, LANES), F32)),
        grid=(nt // tm,),
        in_specs=[tile(d), tile(SSD_INNER), tile(CONF_DIM),
                  pl.BlockSpec((tm // GRID_W, GRID_W, S5_DIM), lambda i: (jnp.minimum(i, n_lat_tiles - 1), 0, 0)),
                  pl.BlockSpec((tm // GRID_W, GRID_W, S5_DIM), lambda i: (jnp.maximum(i - n_lat_tiles, 0), 0, 0)),
                  pl.BlockSpec((None, SSD_INNER, d), lambda i: (l, 0, 0)),
                  pl.BlockSpec((None, CONF_DIM, d), lambda i: (l, SSD_INNER // CONF_DIM, 0)),
                  pl.BlockSpec((None, S5_DIM, d), lambda i: (l, (SSD_INNER + CONF_DIM) // S5_DIM, 0)),
                  _mod_spec(l, grp, 2, d), ng_spec, _mod_spec(l, grp, 3, d), _mod_spec(l, grp, 4, d),
                  pl.BlockSpec((None, d, LANES), lambda i: (l, 0, 0)), rb_spec],
        out_specs=(tile(d), pl.BlockSpec((tm * SUBLANES, LANES), lambda i: (i, 0)), dense, tile(LANES), dense,
                   pl.BlockSpec((SUBLANES, LANES), lambda i: (0, 0))),
        scratch_shapes=[pltpu.VMEM((SUBLANES, LANES), F32)],
        compiler_params=_cparams(("arbitrary",)),
        name="mix_out_router",
    )(r, mix_a, mix_b, mix_c_lat, mix_c_ctx, w_out_bf, w_out_bf, w_out_bf, mods4, ng3, mods4, mods4,
      router_w_pad, rb3)


def _expert_kernel(be_ref, nv_ref, first_ref, nxt_ref, g_cur_ref, g_nxt_ref, s_cur_ref, s_prv_ref, h_hbm,
                   wgu_hbm, bgu_ref, wd_hbm, bd_ref, ya_hbm, xbuf, ybuf, wgu_st, wd_st, wgu_bf, wd_bf, zbuf,
                   gsem, ssem, wsem, zsem, *, layer):
    i = pl.program_id(0)
    n_steps = pl.num_programs(0)
    slot = i % 2
    nv = nv_ref[i]
    blk = MOE_BLOCK
    tile_rows = blk * SUBLANES

    def weight_copies(e):
        return (pltpu.make_async_copy(wgu_hbm.at[layer, e], wgu_st, wsem.at[0]),
                pltpu.make_async_copy(wd_hbm.at[layer, e], wd_st, wsem.at[1]))

    def token_tile(ref, row0):
        return ref.at[pl.ds(pl.multiple_of(row0, SUBLANES), SUBLANES)]

    def used(j):
        return jnp.logical_and(jnp.logical_and(j >= 0, j < n_steps),
                               nv_ref[jnp.clip(j, 0, n_steps - 1)] > 0)

    n_phases = 1
    chunk = blk // n_phases
    row_priority, weight_priority = 0, 1

    def gather_rows(idx_ref, dst_slot, lo_row, n):
        def body(rr, carry):
            pltpu.make_async_copy(token_tile(h_hbm, idx_ref[0, rr]),
                                  token_tile(xbuf.at[dst_slot], rr * SUBLANES),
                                  gsem.at[dst_slot]).start(priority=row_priority)
            return carry
        lax.fori_loop(lo_row, lo_row + n, body, 0, unroll=16)

    def scatter_rows(idx_ref, src_slot, lo_row, n):
        def body(rr, carry):
            pltpu.make_async_copy(token_tile(ybuf.at[src_slot], rr * SUBLANES),
                                  token_tile(ya_hbm, idx_ref[0, rr]),
                                  ssem.at[src_slot]).start(priority=row_priority)
            return carry
        lax.fori_loop(lo_row, lo_row + n, body, 0, unroll=16)

    def scatter_wait(src_slot):
        pltpu.make_async_copy(ybuf.at[src_slot], ya_hbm.at[pl.ds(0, tile_rows)], ssem.at[src_slot]).wait()

    def issue_phase(ph):
        @pl.when(used(i + 1))
        def _():
            gather_rows(g_nxt_ref, 1 - slot, ph * chunk, chunk)

        @pl.when(used(i - 1))
        def _():
            scatter_rows(s_prv_ref, 1 - slot, ph * chunk, chunk)

    @pl.when(i == 0)
    def _():
        zbuf[...] = jnp.zeros_like(zbuf)
        for cp in weight_copies(be_ref[0]):
            cp.start(priority=weight_priority)
        gather_rows(g_cur_ref, 0, 0, blk)

    @pl.when(nv > 0)
    def _():
        @pl.when(first_ref[i] == 1)
        def _():
            for cp in weight_copies(be_ref[i]):
                cp.wait()
            wgu_bf[...] = wgu_st[...].astype(BF16)
            wd_bf[...] = wd_st[...].astype(BF16)

            @pl.when(nxt_ref[i] >= 0)
            def _():
                for cp in weight_copies(nxt_ref[i]):
                    cp.start(priority=weight_priority)

        pltpu.make_async_copy(h_hbm.at[pl.ds(0, tile_rows)], xbuf.at[slot], gsem.at[slot]).wait()
        half = D_MODEL // 2
        xw = jnp.concatenate(_load_token_tiles(xbuf.at[slot], blk), axis=1)
        lo, hi = _unpack_bf16_pairs(xw)
        issue_phase(0)
        gu = (_dot(lo.astype(BF16), wgu_bf[0:half, :]) + _dot(hi.astype(BF16), wgu_bf[half:D_MODEL, :])
              + bgu_ref[...])
        gate = jnp.minimum(gu[:, :D_EXPERT], SWIGLU_LIMIT)
        lin = jnp.clip(gu[:, D_EXPERT:], -SWIGLU_LIMIT, SWIGLU_LIMIT)
        act = (gate * jax.nn.sigmoid(SWIGLU_ALPHA * gate) * (lin + 1.0)).astype(BF16)
        y = _dot(act, wd_bf[...]) + bd_ref[...]

        @pl.when(used(i - 2))
        def _():
            scatter_wait(slot)

        _store_token_tiles(ybuf.at[slot], _pack_bf16_pairs(y))

    @pl.when(nv == 0)
    def _():
        @pl.when(used(i - 2))
        def _():
            scatter_wait(slot)

        @pl.when(used(i - 1))
        def _():
            scatter_rows(s_prv_ref, 1 - slot, 0, blk)

        own = ya_hbm.at[pl.ds(pl.multiple_of(i * tile_rows, tile_rows), tile_rows)]
        zero_copy = pltpu.make_async_copy(zbuf, own, zsem)
        zero_copy.start()
        zero_copy.wait()

    @pl.when(i == n_steps - 1)
    def _():
        @pl.when(nv > 0)
        def _():
            scatter_rows(s_cur_ref, slot, 0, blk)
            scatter_wait(slot)

        @pl.when(used(i - 1))
        def _():
            scatter_wait(1 - slot)


def _experts(h_tiles, gather_row, scatter_row, block_e, n_valid, first, nxt, w_gu, b_gu, w_down, b_down,
             layer):
    n_rows = gather_row.shape[0]
    n_blocks = n_rows // MOE_BLOCK
    depth, ne = w_gu.shape[:2]
    g3 = gather_row.reshape(n_blocks, 1, MOE_BLOCK)
    s3 = scatter_row.reshape(n_blocks, 1, MOE_BLOCK)
    width = D_MODEL // 2
    tile_rows = MOE_BLOCK * SUBLANES
    idx_block = lambda imap: pl.BlockSpec((None, 1, MOE_BLOCK), imap, memory_space=pltpu.SMEM)
    grid_spec = pltpu.PrefetchScalarGridSpec(
        num_scalar_prefetch=4,
        grid=(n_blocks,),
        in_specs=[idx_block(lambda i, *_: (i, 0, 0)),
                  idx_block(lambda i, *_: (jnp.minimum(i + 1, n_blocks - 1), 0, 0)),
                  idx_block(lambda i, *_: (i, 0, 0)),
                  idx_block(lambda i, *_: (jnp.maximum(i - 1, 0), 0, 0)),
                  pl.BlockSpec(memory_space=pl.ANY),
                  pl.BlockSpec(memory_space=pl.ANY),
                  pl.BlockSpec((None, None, 1, 2 * D_EXPERT), lambda i, be, *_: (layer, be[i], 0, 0)),
                  pl.BlockSpec(memory_space=pl.ANY),
                  pl.BlockSpec((None, None, 1, D_MODEL), lambda i, be, *_: (layer, be[i], 0, 0))],
        out_specs=pl.BlockSpec(memory_space=pl.ANY),
        scratch_shapes=[pltpu.VMEM((2, tile_rows, LANES), U32), pltpu.VMEM((2, tile_rows, LANES), U32),
                        pltpu.VMEM((D_MODEL, 2 * D_EXPERT), F32), pltpu.VMEM((D_EXPERT, D_MODEL), F32),
                        pltpu.VMEM((D_MODEL, 2 * D_EXPERT), BF16), pltpu.VMEM((D_EXPERT, D_MODEL), BF16),
                        pltpu.VMEM((tile_rows, LANES), U32),
                        pltpu.SemaphoreType.DMA((2,)), pltpu.SemaphoreType.DMA((2,)),
                        pltpu.SemaphoreType.DMA((2,)), pltpu.SemaphoreType.DMA],
    )
    return pl.pallas_call(
        functools.partial(_expert_kernel, layer=layer),
        out_shape=jax.ShapeDtypeStruct((n_rows * SUBLANES, LANES), U32),
        grid_spec=grid_spec,
        compiler_params=_cparams(("arbitrary",)),
        name="moe_experts",
    )(block_e, n_valid, first, nxt, g3, g3, s3, s3, h_tiles, w_gu, b_gu.reshape(depth, ne, 1, -1), w_down,
      b_down.reshape(depth, ne, 1, -1))


def _combine_kernel(r_ref, gate_ref, y0_ref, y1_ref, y2_ref, y3_ref, g2_ref, fg_ref, o_ref, *, last_layer):
    tm = r_ref.shape[0]
    half = D_MODEL // 2
    gates = [gate_ref[:, k:k + 1] for k in range(TOP_K)]
    y_refs = (y0_ref, y1_ref, y2_ref, y3_ref)
    for c in range(SUBLANES):
        acc_lo = jnp.zeros((tm, LANES), F32)
        acc_hi = jnp.zeros((tm, LANES), F32)
        for k in range(TOP_K):
            lo, hi = _unpack_bf16_pairs(y_refs[k][pl.ds(c, tm, stride=SUBLANES), :])
            acc_lo = acc_lo + gates[k] * lo
            acc_hi = acc_hi + gates[k] * hi
        lo_cols = slice(c * LANES, (c + 1) * LANES)
        hi_cols = slice(half + c * LANES, half + (c + 1) * LANES)
        o_ref[:, lo_cols] = r_ref[:, lo_cols] + g2_ref[:, lo_cols] * acc_lo
        o_ref[:, hi_cols] = r_ref[:, hi_cols] + g2_ref[:, hi_cols] * acc_hi
    if last_layer:
        x = o_ref[...]
        ms = jnp.mean(x * x, axis=-1, keepdims=True)
        o_ref[...] = x * lax.rsqrt(ms + EPS) * fg_ref[...]


def _combine(r1, gates, ya, mods4, final_g, l, lat_len, n_batch, last_layer):
    nt, d = r1.shape
    tm = ROW_TILE
    n_out = n_batch * lat_len if last_layer else nt
    tiles_per_k = nt // tm

    def grp(i):
        return jnp.minimum((i * tm) // lat_len, n_batch)

    def choice(k):
        return pl.BlockSpec((tm * SUBLANES, LANES), lambda i: (k * tiles_per_k + i, 0))

    return pl.pallas_call(
        functools.partial(_combine_kernel, last_layer=last_layer),
        out_shape=jax.ShapeDtypeStruct((n_out, d), F32),
        grid=(n_out // tm,),
        in_specs=[pl.BlockSpec((tm, d), lambda i: (i, 0)),
                  pl.BlockSpec((tm, LANES), lambda i: (i, 0)),
                  choice(0), choice(1), choice(2), choice(3),
                  _mod_spec(l, grp, 5, d),
                  pl.BlockSpec((1, d), lambda i: (0, 0))],
        out_specs=pl.BlockSpec((tm, d), lambda i: (i, 0)),
        compiler_params=_cparams(("arbitrary",)),
        name="moe_combine",
    )(r1, gates, ya, ya, ya, ya, mods4, final_g.reshape(1, d))


def _routing_plan(top_idx, rank, counts, n_blocks):
    n_assign = top_idx.size
    n_tok = n_assign // TOP_K
    n_rows = n_blocks * MOE_BLOCK
    counts = counts.astype(I32)
    padded = (counts + MOE_BLOCK - 1) // MOE_BLOCK * MOE_BLOCK
    pad_end = jnp.cumsum(padded)
    pad_start = pad_end - padded
    count_end = jnp.cumsum(counts)
    per_tile = ROW_TILE * TOP_K // LANES
    q = lax.broadcasted_iota(I32, top_idx.shape, 0)
    lane = lax.broadcasted_iota(I32, top_idx.shape, 1)
    halves = ROW_TILE // LANES
    token = (q // per_tile) * ROW_TILE + (q % halves) * LANES + lane
    choice = (q % per_tile) // halves
    out_tile = choice * n_tok + token
    dest = pad_start[top_idx] + rank
    row_tile = jnp.full((n_rows,), -1, I32).at[dest].set(out_tile, unique_indices=True, mode='drop')
    blk_start = jnp.arange(n_blocks, dtype=I32) * MOE_BLOCK

    def expert_at(pos):
        return jnp.minimum(jnp.sum((pad_end[None, :] <= pos[:, None]).astype(I32), axis=1), N_EXPERTS - 1)

    block_e = expert_at(blk_start)
    used = blk_start < pad_end[-1]
    n_valid = jnp.where(used, jnp.clip(pad_start[block_e] + counts[block_e] - blk_start, 0, MOE_BLOCK), 0)
    first = jnp.logical_and(used, blk_start == pad_start[block_e]).astype(I32)
    nxt_start = pad_end[block_e]
    nxt = jnp.where(nxt_start < pad_end[-1], expert_at(nxt_start), -1).astype(I32)

    row = jnp.arange(n_rows, dtype=I32)
    empty = row_tile < 0
    dump = n_assign + row - jnp.repeat(count_end[block_e], MOE_BLOCK)
    gather_row = jnp.where(empty, 0, row_tile % n_tok) * SUBLANES
    scatter_row = jnp.where(empty, dump, row_tile) * SUBLANES
    return gather_row.astype(I32), scatter_row.astype(I32), block_e.astype(I32), n_valid.astype(I32), first, nxt


def _forward(x, c, ctx, c_ctx, ada_w, ada_b, norm1_g, w_in, ssd_conv_w, ssd_conv_b, ssd_a_log,
             ssd_dt_bias, ssd_d, ssd_norm_g, conf_dw_w, conf_dw_b, conf_ln_g, conf_ln_b, conf_pw_w,
             conf_pw_b, conf_out_g, s5_lam_re, s5_lam_im, s5_log_step, s5_b_re, s5_b_im, s5_c_re,
             s5_c_im, s5_d, s5_glu_w, s5_glu_b, s5_out_g, w_out, norm2_g, router_w, router_b,
             w_gate_up, b_gate_up, w_down, b_down, final_norm_g):
    n_batch, lat_len, d = x.shape
    ctx_len = ctx.shape[1]
    depth = ada_w.shape[0]
    n_lat = n_batch * lat_len
    nt = n_lat + n_batch * ctx_len
    assert d == D_MODEL and ctx_len == ROW_TILE and lat_len % MM_TILE_M == 0
    assert lat_len // GRID_W == GRID_W

    r = jnp.concatenate([x.reshape(n_lat, d), ctx.reshape(n_batch * ctx_len, d)], axis=0).astype(F32)
    cond = jnp.zeros((SUBLANES, d), F32).at[:n_batch].set(c).at[n_batch].set(c_ctx)
    mods4 = _modulation(cond, ada_w, ada_b).reshape(depth, SUBLANES, 1, 6 * d)

    c_dt = SSD_INNER + SSD_CONV_DIM
    c_conf = c_dt + SSD_HEADS
    c_s5 = c_conf + 2 * CONF_DIM
    w_main = jnp.concatenate([w_in[:, :, :c_dt], w_in[:, :, c_conf:c_s5]], axis=2).astype(BF16)
    w_side = jnp.concatenate([w_in[:, :, c_s5:], w_in[:, :, c_dt:c_conf],
                              jnp.zeros((depth, d, LANES - SSD_HEADS), w_in.dtype)], axis=2).astype(BF16)
    w_out_bf = w_out.astype(BF16)
    conf_pw_bf = conf_pw_w.astype(BF16)
    s5_glu_bf = s5_glu_w.astype(BF16)
    head_pad = lambda a: jnp.pad(a.astype(F32), ((0, 0), (0, 0), (0, LANES - SSD_HEADS))).reshape(
        depth * 2, 1, LANES)
    dtb_all, alog_all = head_pad(ssd_dt_bias), head_pad(ssd_a_log)
    d_skip_all = jnp.repeat(ssd_d.astype(F32), SSD_HEAD_DIM, axis=1)
    router_w_pad = jnp.pad(router_w.astype(F32), ((0, 0), (0, 0), (0, LANES - N_EXPERTS)))
    router_b_pad = jnp.pad(router_b.astype(F32), ((0, 0), (0, LANES - N_EXPERTS)), constant_values=-1e30)
    flat2 = lambda a: a.reshape((depth * 2,) + a.shape[2:])
    s5_ops = jax.vmap(_s5_operands)(*[flat2(a) for a in (s5_lam_re, s5_lam_im, s5_log_step, s5_b_re,
                                                         s5_b_im, s5_c_re, s5_c_im)])

    n_blocks = nt * TOP_K // MOE_BLOCK + N_EXPERTS
    s5_zero = jnp.zeros((n_batch, 1, 2 * S5_NSTATE), F32)

    for l in range(depth):
        p_main = _norm_matmul(r, norm1_g, mods4, l, w_main, lat_len, n_batch, "in_proj_main", (BF16,))
        p_s5, p_dt = _norm_matmul(r, norm1_g, mods4, l, w_side, lat_len, n_batch, "in_proj_side",
                                  (F32, F32), splits=(S5_DIM, LANES), grid_rows=(0,))

        xbc = _ssd_conv(p_main, ssd_conv_w, ssd_conv_b, l, lat_len, ctx_len, n_batch)
        y_fwd = _ssd_scan(xbc, p_main, p_dt, dtb_all, alog_all, l, lat_len, ctx_len, n_batch, False)
        mix_a = _ssd_scan(xbc, p_main, p_dt, dtb_all, alog_all, l, lat_len, ctx_len, n_batch, True,
                          (y_fwd, d_skip_all, ssd_norm_g))

        mix_b = _conformer(p_main, conf_dw_w, conf_dw_b, conf_ln_g, conf_ln_b, conf_pw_bf, conf_pw_b,
                           conf_out_g, l, lat_len, ctx_len, n_batch)

        yc_f, st_f = _s5_scan(p_s5, s5_ops, 2 * l, s5_zero, n_batch, lat_len, False, False)
        yl_f, _ = _s5_scan(p_s5, s5_ops, 2 * l, st_f, n_batch, lat_len, False, True)
        fin = (l, s5_d, s5_glu_bf, s5_glu_b, s5_out_g)
        mc_c, st_r = _s5_scan(p_s5, s5_ops, 2 * l + 1, s5_zero, n_batch, lat_len, True, False, (yc_f,) + fin)
        mc_l, _ = _s5_scan(p_s5, s5_ops, 2 * l + 1, st_r, n_batch, lat_len, True, True, (yl_f,) + fin)

        r1, h_tiles, top_idx, gates, rank, counts = _mix_out(
            r, mix_a, mix_b, mc_l, mc_c, w_out_bf, mods4, norm2_g, router_w_pad,
            router_b_pad, l, lat_len, n_batch)

        gather_row, scatter_row, block_e, n_valid, first, nxt = _routing_plan(
            top_idx, rank, counts[0, :N_EXPERTS], n_blocks)
        ya = _experts(h_tiles, gather_row, scatter_row, block_e, n_valid, first, nxt, w_gate_up, b_gate_up,
                      w_down, b_down, l)
        r = _combine(r1, gates, ya, mods4, final_norm_g, l, lat_len, n_batch, l == depth - 1)

    return r.reshape(n_batch, lat_len, d).astype(x.dtype)


def kernel(x, c, ctx, c_ctx, ada_w, ada_b, norm1_g, w_in, ssd_conv_w, ssd_conv_b, ssd_a_log, ssd_dt_bias,
           ssd_d, ssd_norm_g, conf_dw_w, conf_dw_b, conf_ln_g, conf_ln_b, conf_pw_w, conf_pw_b, conf_out_g,
           s5_lam_re, s5_lam_im, s5_log_step, s5_b_re, s5_b_im, s5_c_re, s5_c_im, s5_d, s5_glu_w, s5_glu_b,
           s5_out_g, w_out, norm2_g, router_w, router_b, w_gate_up, b_gate_up, w_down, b_down, final_norm_g):
    return _forward(x, c, ctx, c_ctx, ada_w, ada_b, norm1_g, w_in, ssd_conv_w, ssd_conv_b, ssd_a_log,
                    ssd_dt_bias, ssd_d, ssd_norm_g, conf_dw_w, conf_dw_b, conf_ln_g, conf_ln_b, conf_pw_w,
                    conf_pw_b, conf_out_g, s5_lam_re, s5_lam_im, s5_log_step, s5_b_re, s5_b_im, s5_c_re,
                    s5_c_im, s5_d, s5_glu_w, s5_glu_b, s5_out_g, w_out, norm2_g, router_w, router_b,
                    w_gate_up, b_gate_up, w_down, b_down, final_norm_g)
```

```python
import functools
import math

import jax
import jax.numpy as jnp
from jax import lax
from jax.experimental import pallas as pl
from jax.experimental.pallas import tpu as pltpu

F32 = jnp.float32
BF16 = jnp.bfloat16
I32 = jnp.int32
U32 = jnp.uint32

D_MODEL = 2048
GRID_W = 64
SSD_INNER = 1024
SSD_HEAD_DIM = 64
SSD_HEADS = 16
SSD_GROUPS = 4
SSD_STATE = 128
SSD_CONV = 5
SSD_CHUNK = 128
SSD_CONV_DIM = SSD_INNER + 2 * SSD_GROUPS * SSD_STATE
CONF_DIM = 512
CONF_KERNEL = 31
S5_DIM = 512
S5_GROUP = 16
S5_GROUPS = 32
S5_STATE = 64
S5_NSTATE = S5_GROUPS * S5_STATE
N_EXPERTS = 32
TOP_K = 4
D_EXPERT = 768
SWIGLU_LIMIT = 7.0
SWIGLU_ALPHA = 1.702
MOE_BLOCK = 256
EPS = 1e-6

LANES = 128
SUBLANES = 8
ROW_TILE = 256
MM_TILE_M = 512
MM_TILE_N = 1024
MAIN_COLS = 4096
COL_Z, COL_X, COL_CONF = 0, 1024, 3072
SIDE_COLS = S5_DIM + LANES
CONV_HALO = 16
CONF_HALO = 16
S5_SEG = 32
S5_WIN = SUBLANES * S5_SEG
VMEM_LIMIT = 56 * 1024 * 1024


def _cparams(sem, vmem=VMEM_LIMIT):
    return pltpu.CompilerParams(dimension_semantics=sem, vmem_limit_bytes=vmem)


def _silu(v):
    return v * jax.nn.sigmoid(v)


def _split3(v):
    hi = v.astype(BF16)
    r1 = v - hi.astype(F32)
    mid = r1.astype(BF16)
    lo = (r1 - mid.astype(F32)).astype(BF16)
    return hi, mid, lo


def _dot(a, b):
    return jnp.dot(a, b, preferred_element_type=F32)


def _layer_rows(arr, l):
    depth = arr.shape[0]
    a3 = arr.reshape(depth, 1, -1)
    return a3, pl.BlockSpec((None, 1, a3.shape[2]), lambda *_: (l, 0, 0))


def _layer_mat(arr, l):
    return arr, pl.BlockSpec((None,) + arr.shape[1:], lambda *_: (l, 0, 0))


def _mod_kernel(c_ref, w_ref, b_ref, o_ref):
    s = _silu(c_ref[...])
    w = w_ref[...]
    s_hi = s.astype(BF16)
    s_lo = (s - s_hi.astype(F32)).astype(BF16)
    w_hi = w.astype(BF16)
    w_lo = (w - w_hi.astype(F32)).astype(BF16)
    acc = _dot(s_hi, w_hi) + _dot(s_lo, w_hi) + _dot(s_hi, w_lo)
    o_ref[...] = acc + b_ref[...]


def _modulation(cond, ada_w, ada_b):
    depth, d, n = ada_w.shape
    tn = 1024
    return pl.pallas_call(
        _mod_kernel,
        out_shape=jax.ShapeDtypeStruct((depth, SUBLANES, n), F32),
        grid=(depth, n // tn),
        in_specs=[pl.BlockSpec((SUBLANES, d), lambda l, j: (0, 0)),
                  pl.BlockSpec((None, d, tn), lambda l, j: (l, 0, j)),
                  pl.BlockSpec((None, 1, tn), lambda l, j: (l, 0, j))],
        out_specs=pl.BlockSpec((None, SUBLANES, tn), lambda l, j: (l, 0, j)),
        compiler_params=_cparams(("arbitrary", "arbitrary")),
        name="adaln_mod",
    )(cond, ada_w, ada_b.reshape(depth, 1, n))


def _mod_spec(l, grp, k, d):
    return pl.BlockSpec((None, None, 1, d), lambda i, *_: (l, grp(i), 0, k))


def _inproj_kernel(x_ref, g_ref, sh_ref, sc_ref, w_ref, main_ref, s5_ref, dt_ref, hn_ref, *, n_tiles,
                   n_main):
    j = pl.program_id(1)
    s = pl.program_id(2)
    live = 2 * pl.program_id(0) + s < n_tiles

    @pl.when(jnp.logical_and(j == 0, live))
    def _():
        x = x_ref[...]
        ms = jnp.mean(x * x, axis=-1, keepdims=True)
        y = x * lax.rsqrt(ms + EPS) * g_ref[...]
        hn_ref[s] = (y * (1.0 + sc_ref[...]) + sh_ref[...]).astype(BF16)

    @pl.when(jnp.logical_and(live, j < n_main))
    def _():
        main_ref[...] = _dot(hn_ref[s], w_ref[...]).astype(main_ref.dtype)

    @pl.when(jnp.logical_and(live, j == n_main))
    def _():
        res = _dot(hn_ref[s], w_ref[...])
        s5_ref[...] = res[:, 0:S5_DIM].reshape(s5_ref.shape)
        dt_ref[...] = res[:, S5_DIM:S5_DIM + LANES]


def _in_proj(r, gains, mods4, l, w_all, lat_len, n_batch):
    nt, d = r.shape
    tm = MM_TILE_M if nt % MM_TILE_M == 0 else ROW_TILE
    tn = MM_TILE_N
    n_main = MAIN_COLS // tn
    assert w_all.shape[2] == (n_main + 1) * tn and SIDE_COLS <= tn
    n_tiles = nt // tm
    last = n_tiles - 1

    def tile(p, s):
        return jnp.minimum(2 * p + s, last)

    def pair_end(p):
        return jnp.minimum(2 * p + 1, last)

    def x_tile(p, j, s):
        return jnp.where(j == 0, tile(p, s), pair_end(p))

    def main_tile(p, j, s):
        return jnp.where(j < n_main, tile(p, s), pair_end(p))

    def side_tile(p, j, s):
        return jnp.where(j == n_main, tile(p, s), jnp.maximum(2 * p - 1, 0))

    def grp(t):
        return jnp.minimum((t * tm) // lat_len, n_batch)

    def mod_spec(k):
        return pl.BlockSpec((None, None, 1, d), lambda p, j, s: (l, grp(tile(p, s)), 0, k))

    g3, g_spec = _layer_rows(gains, l)
    return pl.pallas_call(
        functools.partial(_inproj_kernel, n_tiles=n_tiles, n_main=n_main),
        out_shape=(jax.ShapeDtypeStruct((nt, MAIN_COLS), BF16),
                   jax.ShapeDtypeStruct((nt // GRID_W, GRID_W, S5_DIM), F32),
                   jax.ShapeDtypeStruct((nt, LANES), F32)),
        grid=(pl.cdiv(n_tiles, 2), n_main + 1, 2),
        in_specs=[pl.BlockSpec((tm, d), lambda p, j, s: (x_tile(p, j, s), 0)),
                  g_spec, mod_spec(0), mod_spec(1),
                  pl.BlockSpec((None, d, tn), lambda p, j, s: (l, 0, j))],
        out_specs=(pl.BlockSpec((tm, tn), lambda p, j, s: (main_tile(p, j, s), jnp.minimum(j, n_main - 1))),
                   pl.BlockSpec((tm // GRID_W, GRID_W, S5_DIM), lambda p, j, s: (side_tile(p, j, s), 0, 0)),
                   pl.BlockSpec((tm, LANES), lambda p, j, s: (side_tile(p, j, s), 0))),
        scratch_shapes=[pltpu.VMEM((2, tm, d), BF16)],
        compiler_params=_cparams(("arbitrary", "arbitrary", "arbitrary")),
        name="in_proj",
    )(r, g3, mods4, mods4, w_all)


def _seq_edges(i, tiles_per_lat_seq, n_lat_tiles):
    is_lat = i < n_lat_tiles
    first = jnp.logical_or(jnp.logical_not(is_lat), (i % tiles_per_lat_seq) == 0)
    last = jnp.logical_or(jnp.logical_not(is_lat), (i % tiles_per_lat_seq) == tiles_per_lat_seq - 1)
    return first, last


def _conv5_kernel(cur_ref, prev_ref, next_ref, w_ref, b_ref, o_ref, ext_ref, *, tiles_per_lat_seq,
                  n_lat_tiles):
    i = pl.program_id(0)
    first, last = _seq_edges(i, tiles_per_lat_seq, n_lat_tiles)
    h, tm = CONV_HALO, ROW_TILE
    ext_ref[0:h, :] = jnp.where(first, 0.0, prev_ref[...].astype(F32))
    ext_ref[h:h + tm, :] = cur_ref[...].astype(F32)
    ext_ref[h + tm:h + tm + h, :] = jnp.where(last, 0.0, next_ref[...].astype(F32))
    pad = (SSD_CONV - 1) // 2
    cw = 512
    for c in range(0, ext_ref.shape[1], cw):
        acc = jnp.broadcast_to(b_ref[:, c:c + cw], (tm, cw))
        for j in range(SSD_CONV):
            acc = acc + w_ref[j:j + 1, c:c + cw] * ext_ref[h - pad + j:h - pad + j + tm, c:c + cw]
        o_ref[:, c:c + cw] = _silu(acc).astype(BF16)


def _ssd_conv(p_main, conv_w, conv_b, l, lat_len, ctx_len, n_batch):
    nt = p_main.shape[0]
    tm, h = ROW_TILE, CONV_HALO
    assert ctx_len == tm and lat_len % tm == 0
    n_lat_tiles = n_batch * lat_len // tm
    cw = SSD_CONV_DIM // 2
    xblk = COL_X // cw
    nhb = nt // h
    depth = conv_w.shape[0]
    kern = functools.partial(_conv5_kernel, tiles_per_lat_seq=lat_len // tm, n_lat_tiles=n_lat_tiles)
    return pl.pallas_call(
        kern,
        out_shape=jax.ShapeDtypeStruct((nt, SSD_CONV_DIM), BF16),
        grid=(nt // tm, 2),
        in_specs=[pl.BlockSpec((tm, cw), lambda i, j: (i, xblk + j)),
                  pl.BlockSpec((h, cw), lambda i, j: (jnp.maximum(i * (tm // h) - 1, 0), xblk + j)),
                  pl.BlockSpec((h, cw), lambda i, j: (jnp.minimum((i + 1) * (tm // h), nhb - 1), xblk + j)),
                  pl.BlockSpec((None, SSD_CONV, cw), lambda i, j: (l, 0, j)),
                  pl.BlockSpec((None, 1, cw), lambda i, j: (l, 0, j))],
        out_specs=pl.BlockSpec((tm, cw), lambda i, j: (i, j)),
        scratch_shapes=[pltpu.VMEM((tm + 2 * h, cw), F32)],
        compiler_params=_cparams(("arbitrary", "arbitrary")),
        name="ssd_conv",
    )(p_main, p_main, p_main, conv_w, conv_b.reshape(depth, 1, -1))


def _head_cols(vals, width):
    lane = lax.broadcasted_iota(I32, (1, LANES), 1)
    halves = []
    per_half = LANES // width
    for hh in range(len(vals) // per_half):
        sel = vals[hh * per_half + per_half - 1]
        for k in range(per_half - 2, -1, -1):
            sel = jnp.where(lane < (k + 1) * width, vals[hh * per_half + k], sel)
        halves.append(sel)
    return jnp.concatenate(halves, axis=1)


def _ssd_kernel(*refs, reverse, final):
    if final:
        (x_ref, b_ref, c_ref, dt_ref, dtb_ref, alog_ref, hx_ref, yprev_ref, z_ref, dskip_ref, ng_ref,
         o_ref, state_ref, tmp_ref) = refs
    else:
        x_ref, b_ref, c_ref, dt_ref, dtb_ref, alog_ref, hx_ref, o_ref, state_ref = refs
    q = SSD_CHUNK
    r = SSD_HEADS // SSD_GROUPS
    gw = r * SSD_HEAD_DIM

    @pl.when(pl.program_id(1) == 0)
    def _():
        state_ref[...] = jnp.zeros_like(state_ref)

    lane = lax.broadcasted_iota(I32, (1, LANES), 1)
    dtp = jax.nn.softplus(dt_ref[...] + dtb_ref[...])
    a_head = -jnp.exp(alog_ref[...])
    a = jnp.where(lane < SSD_HEADS, dtp * a_head, 0.0)
    ri = lax.broadcasted_iota(I32, (q, q), 0)
    ci = lax.broadcasted_iota(I32, (q, q), 1)
    tri = (ci >= ri) if reverse else (ci <= ri)
    tri_b = jnp.where(tri, 1.0, 0.0).astype(BF16)
    a_hi, a_mid, a_lo = _split3(a)
    a_cs = _dot(tri_b, a_hi) + _dot(tri_b, a_mid) + _dot(tri_b, a_lo)
    a_cs_t = a_cs.T
    dtp_t = dtp.T
    a_end = a_cs[0:1, :] if reverse else a_cs[q - 1:q, :]
    lane_g = lax.broadcasted_iota(I32, (1, gw), 1)

    pieces = []
    for fac in (jnp.exp(a_cs), dtp * jnp.exp(a_end - a_cs)):
        hi = fac.astype(BF16)
        pieces += [hi, (fac - hi.astype(F32)).astype(BF16)]
    spread = _dot(jnp.concatenate(pieces, axis=0), hx_ref[...])
    e_all = spread[0:q] + spread[q:2 * q]
    w_all = spread[2 * q:3 * q] + spread[3 * q:4 * q]

    x = x_ref[...]
    for g in range(SSD_GROUPS):
        cg = c_ref[:, g * SSD_STATE:(g + 1) * SSD_STATE]
        bg = b_ref[:, g * SSD_STATE:(g + 1) * SSD_STATE]
        cb = lax.dot_general(cg, bg, (((1,), (1,)), ((), ())), preferred_element_type=F32)
        xg = x[:, g * gw:(g + 1) * gw]
        yg = jnp.zeros((q, gw), F32)
        dec = []
        for hl in range(r):
            h = g * r + hl
            col = a_cs[:, h:h + 1]
            row = a_cs_t[h:h + 1, :]
            lm = jnp.where(tri, jnp.exp(col - row), 0.0)
            m = (cb * lm * dtp_t[h:h + 1, :]).astype(BF16)
            in_head = jnp.logical_and(lane_g >= hl * SSD_HEAD_DIM, lane_g < (hl + 1) * SSD_HEAD_DIM)
            xm = jnp.where(in_head, xg, jnp.zeros_like(xg))
            yg = yg + _dot(m, xm)
            dec.append(jnp.exp(a_end[:, h:h + 1]))
        s_old = state_ref[g]
        gs = slice(g * gw, (g + 1) * gw)
        yg = yg + e_all[:, gs] * _dot(cg, s_old.astype(BF16))
        xw = (xg.astype(F32) * w_all[:, gs]).astype(BF16)
        upd = lax.dot_general(bg, xw, (((0,), (0,)), ((), ())), preferred_element_type=F32)
        state_ref[g] = _head_cols(dec, SSD_HEAD_DIM) * s_old + upd
        sl = slice(g * gw, (g + 1) * gw)
        if final:
            ytot = yprev_ref[:, sl] + yg + dskip_ref[:, sl] * xg.astype(F32)
            tmp_ref[:, sl] = ytot * _silu(z_ref[:, sl].astype(F32))
        else:
            o_ref[:, sl] = yg
    if final:
        gated = tmp_ref[...]
        ms = jnp.mean(gated * gated, axis=-1, keepdims=True)
        o_ref[...] = (gated * lax.rsqrt(ms + EPS) * ng_ref[...]).astype(o_ref.dtype)


def _ssd_scan(xbc, p_main, p_dt, dtb_all, alog_all, l, lat_len, ctx_len, n_batch, reverse, final_args=None):
    nt = xbc.shape[0]
    q = SSD_CHUNK
    ncl, ncc = lat_len // q, ctx_len // q
    ctx0 = n_batch * ncl
    dd = 1 if reverse else 0

    def blk(b, j):
        if reverse:
            return jnp.where(j < ncc, ctx0 + b * ncc + (ncc - 1 - j), b * ncl + (ncl - 1 - (j - ncc)))
        return jnp.where(j < ncc, ctx0 + b * ncc + j, b * ncl + (j - ncc))

    head_vec = pl.BlockSpec((None, 1, LANES), lambda b, j: (2 * l + dd, 0, 0))
    final = final_args is not None
    in_specs = [pl.BlockSpec((q, SSD_INNER), lambda b, j: (blk(b, j), 0)),
                pl.BlockSpec((q, SSD_GROUPS * SSD_STATE), lambda b, j: (blk(b, j), 2)),
                pl.BlockSpec((q, SSD_GROUPS * SSD_STATE), lambda b, j: (blk(b, j), 3)),
                pl.BlockSpec((q, LANES), lambda b, j: (blk(b, j), 0)),
                head_vec, head_vec,
                pl.BlockSpec((LANES, SSD_INNER), lambda b, j: (0, 0))]
    head_spread = (jnp.arange(SSD_INNER, dtype=I32)[None, :] // SSD_HEAD_DIM
                   == jnp.arange(LANES, dtype=I32)[:, None]).astype(BF16)
    args = [xbc, xbc, xbc, p_dt, dtb_all, alog_all, head_spread]
    scratch = [pltpu.VMEM((SSD_GROUPS, SSD_STATE, (SSD_HEADS // SSD_GROUPS) * SSD_HEAD_DIM), F32)]
    if final:
        y_prev, d_skip_all, norm_g_all = final_args
        ds3, ds_spec = _layer_rows(d_skip_all, l)
        ng3, ng_spec = _layer_rows(norm_g_all, l)
        in_specs += [pl.BlockSpec((q, SSD_INNER), lambda b, j: (blk(b, j), 0)),
                     pl.BlockSpec((q, SSD_INNER), lambda b, j: (blk(b, j), COL_Z // SSD_INNER)),
                     ds_spec, ng_spec]
        args += [y_prev, p_main, ds3, ng3]
        scratch.append(pltpu.VMEM((q, SSD_INNER), F32))
    return pl.pallas_call(
        functools.partial(_ssd_kernel, reverse=reverse, final=final),
        out_shape=jax.ShapeDtypeStruct((nt, SSD_INNER), BF16 if final else F32),
        grid=(n_batch, ncc + ncl),
        in_specs=in_specs,
        out_specs=pl.BlockSpec((q, SSD_INNER), lambda b, j: (blk(b, j), 0)),
        scratch_shapes=scratch,
        compiler_params=_cparams(("arbitrary", "arbitrary")),
        name="ssd_scan_rev" if reverse else "ssd_scan_fwd",
    )(*args)


def _glu(ref):
    v = ref[:, 0:CONF_DIM].astype(F32)
    gt = ref[:, CONF_DIM:2 * CONF_DIM].astype(F32)
    return v * jax.nn.sigmoid(gt)


def _conf_kernel(cur_ref, prev_ref, next_ref, dww_ref, dwb_ref, lng_ref, lnb_ref, pww_ref, pwb_ref,
                 og_ref, o_ref, ext_ref, acc_ref, sh_ref, *, tiles_per_lat_seq, n_lat_tiles):
    i = pl.program_id(0)
    first, last = _seq_edges(i, tiles_per_lat_seq, n_lat_tiles)
    h, tm = CONF_HALO, ROW_TILE
    ext_ref[0:h, :] = jnp.where(first, 0.0, _glu(prev_ref))
    ext_ref[h:h + tm, :] = _glu(cur_ref)
    ext_ref[h + tm:h + tm + h, :] = jnp.where(last, 0.0, _glu(next_ref))
    pad = (CONF_KERNEL - 1) // 2
    span = (CONF_KERNEL - 1) // SUBLANES * SUBLANES + tm
    for b in range(SUBLANES):
        sh_ref[b] = ext_ref[h - pad + b:h - pad + b + span, :]
    cw = 256
    for c in range(0, CONF_DIM, cw):
        acc = jnp.broadcast_to(dwb_ref[:, c:c + cw], (tm, cw))
        for j in range(CONF_KERNEL):
            a8 = j // SUBLANES * SUBLANES
            acc = acc + dww_ref[j:j + 1, c:c + cw] * sh_ref[j % SUBLANES, a8:a8 + tm, c:c + cw]
        acc_ref[:, c:c + cw] = acc
    u = acc_ref[...]
    mu = jnp.mean(u, axis=-1, keepdims=True)
    var = jnp.mean(jnp.square(u - mu), axis=-1, keepdims=True)
    y = (u - mu) * lax.rsqrt(var + EPS) * lng_ref[...] + lnb_ref[...]
    y = _silu(y)
    v = _dot(y.astype(BF16), pww_ref[...]) + pwb_ref[...]
    ms = jnp.mean(v * v, axis=-1, keepdims=True)
    o_ref[...] = (v * lax.rsqrt(ms + EPS) * og_ref[...]).astype(o_ref.dtype)


def _conformer(p_main, dw_w, dw_b, ln_g, ln_b, pw_w_bf, pw_b, out_g, l, lat_len, ctx_len, n_batch):
    nt = p_main.shape[0]
    tm, h = ROW_TILE, CONF_HALO
    n_lat_tiles = n_batch * lat_len // tm
    cblk = COL_CONF // (2 * CONF_DIM)
    nhb = nt // h
    kern = functools.partial(_conf_kernel, tiles_per_lat_seq=lat_len // tm, n_lat_tiles=n_lat_tiles)
    rows = [_layer_rows(a, l) for a in (dw_b, ln_g, ln_b)]
    rows2 = [_layer_rows(a, l) for a in (pw_b, out_g)]
    dww, dww_spec = _layer_mat(dw_w, l)
    pww, pww_spec = _layer_mat(pw_w_bf, l)
    return pl.pallas_call(
        kern,
        out_shape=jax.ShapeDtypeStruct((nt, CONF_DIM), BF16),
        grid=(nt // tm,),
        in_specs=[pl.BlockSpec((tm, 2 * CONF_DIM), lambda i: (i, cblk)),
                  pl.BlockSpec((h, 2 * CONF_DIM), lambda i: (jnp.maximum(i * (tm // h) - 1, 0), cblk)),
                  pl.BlockSpec((h, 2 * CONF_DIM), lambda i: (jnp.minimum((i + 1) * (tm // h), nhb - 1), cblk)),
                  dww_spec] + [s for _, s in rows] + [pww_spec] + [s for _, s in rows2],
        out_specs=pl.BlockSpec((tm, CONF_DIM), lambda i: (i, 0)),
        scratch_shapes=[pltpu.VMEM((tm + 2 * h, CONF_DIM), F32), pltpu.VMEM((tm, CONF_DIM), F32),
                        pltpu.VMEM((SUBLANES, (CONF_KERNEL - 1) // SUBLANES * SUBLANES + tm, CONF_DIM), F32)],
        compiler_params=_cparams(("arbitrary",)),
        name="conformer",
    )(p_main, p_main, p_main, dww, *[a for a, _ in rows], pww, *[a for a, _ in rows2])


def _gelu_tanh(v):
    return 0.5 * v * (1.0 + jnp.tanh(math.sqrt(2.0 / math.pi) * (v + 0.044715 * (v * v * v))))


def _s5_kernel(*refs, reverse, final, colmajor, seg):
    if final:
        (u_ref, yprev_ref, bblk_ref, cblk_ref, lam_ref, sin_ref, dskip_ref, gw_ref, gb_ref, og_ref,
         o_ref, sout_ref, h_ref, fin_ref, init_ref, carry_ref, *perm_refs) = refs
    else:
        (u_ref, bblk_ref, cblk_ref, lam_ref, sin_ref,
         o_ref, sout_ref, h_ref, fin_ref, init_ref, carry_ref, *perm_refs) = refs
    ns = S5_NSTATE
    nsub = SUBLANES
    win = seg * nsub
    jw = pl.program_id(1)

    @pl.when(jw == 0)
    def _():
        carry_ref[...] = sin_ref[...]

    n_lb = S5_DIM // LANES

    per_row = GRID_W // seg

    def sub_seg(s):
        return s // per_row, slice((s % per_row) * seg, (s % per_row + 1) * seg)

    def permuted(tile_ref, buf_ref):
        for s in range(nsub):
            g, rows = sub_seg(s)
            for k in range(n_lb):
                buf_ref[k, pl.ds(s, seg, stride=nsub), :] = tile_ref[g, rows, k * LANES:(k + 1) * LANES]
        return jnp.concatenate([buf_ref[k] for k in range(n_lb)], axis=1)

    if colmajor:
        u_win = u_ref[...].reshape(win, S5_DIM)
        prev_win = yprev_ref[...].reshape(win, S5_DIM) if final else None
    else:
        u_win = permuted(u_ref, perm_refs[0])
        prev_win = permuted(yprev_ref, perm_refs[1]) if final else None

    u_bf = u_win.astype(BF16)
    ch_per_tile = 2 * LANES // S5_STATE * S5_GROUP
    for j in range(2 * ns // (2 * LANES)):
        c0 = (j * ch_per_tile) % S5_DIM // LANES * LANES
        h_ref[:, j * 2 * LANES:(j + 1) * 2 * LANES] = _dot(
            u_bf[:, c0:c0 + LANES], bblk_ref[c0:c0 + LANES, j * 2 * LANES:(j + 1) * 2 * LANES])

    cw = 512
    n_chunks = ns // cw

    def lam_chunk(row, c):
        return (jnp.broadcast_to(lam_ref[row:row + 1, c * cw:(c + 1) * cw], (nsub, cw)),
                jnp.broadcast_to(lam_ref[row + 1:row + 2, c * cw:(c + 1) * cw], (nsub, cw)))

    def row0(i):
        step = (seg - 1 - i) if reverse else i
        return pl.multiple_of(step * nsub, nsub)

    for c in range(n_chunks):
        lre, lim = lam_chunk(0, c)
        cre = slice(c * cw, (c + 1) * cw)
        cim = slice(ns + c * cw, ns + (c + 1) * cw)

        def step1(i, hc, cre=cre, cim=cim, lre=lre, lim=lim):
            hre, him = hc
            r0 = row0(i)
            nre = lre * hre - lim * him + h_ref[pl.ds(r0, nsub), cre]
            nim = lre * him + lim * hre + h_ref[pl.ds(r0, nsub), cim]
            h_ref[pl.ds(r0, nsub), cre] = nre
            h_ref[pl.ds(r0, nsub), cim] = nim
            return nre, nim

        z0 = jnp.zeros((nsub, cw), F32)
        fre, fim = lax.fori_loop(0, seg, step1, (z0, z0), unroll=4)
        fin_ref[:, cre] = fre
        fin_ref[:, cim] = fim

    seg_row = {32: 2, 64: 4}[seg]
    gre, gim = lam_ref[seg_row:seg_row + 1, :], lam_ref[seg_row + 1:seg_row + 2, :]
    cur_re, cur_im = carry_ref[:, 0:ns], carry_ref[:, ns:2 * ns]
    order = range(nsub - 1, -1, -1) if reverse else range(nsub)
    for s in order:
        init_ref[s:s + 1, 0:ns] = cur_re
        init_ref[s:s + 1, ns:2 * ns] = cur_im
        f_re, f_im = fin_ref[s:s + 1, 0:ns], fin_ref[s:s + 1, ns:2 * ns]
        cur_re, cur_im = gre * cur_re - gim * cur_im + f_re, gre * cur_im + gim * cur_re + f_im
    carry_ref[:, 0:ns] = cur_re
    carry_ref[:, ns:2 * ns] = cur_im

    for c in range(n_chunks):
        lre, lim = lam_chunk(0, c)
        cre = slice(c * cw, (c + 1) * cw)
        cim = slice(ns + c * cw, ns + (c + 1) * cw)

        def step2(i, gc, cre=cre, cim=cim, lre=lre, lim=lim):
            g_re, g_im = gc
            n_re = lre * g_re - lim * g_im
            n_im = lre * g_im + lim * g_re
            r0 = row0(i)
            h_ref[pl.ds(r0, nsub), cre] = h_ref[pl.ds(r0, nsub), cre] + n_re
            h_ref[pl.ds(r0, nsub), cim] = h_ref[pl.ds(r0, nsub), cim] + n_im
            return n_re, n_im

        lax.fori_loop(0, seg, step2, (init_ref[:, cre], init_ref[:, cim]), unroll=4)

    halves = []
    st_per_tile = 2 * LANES // S5_GROUP * S5_STATE
    for n in range(S5_DIM // (2 * LANES)):
        oc = slice(n * 2 * LANES, (n + 1) * 2 * LANES)
        s_re = slice(n * st_per_tile, (n + 1) * st_per_tile)
        s_im = slice(ns + n * st_per_tile, ns + (n + 1) * st_per_tile)
        halves.append(_dot(h_ref[:, s_re].astype(BF16), cblk_ref[s_re, oc])
                      + _dot(h_ref[:, s_im].astype(BF16), cblk_ref[s_im, oc]))
    y_win = jnp.concatenate(halves, axis=1)

    if final:
        tot = prev_win + y_win + dskip_ref[...] * u_win
        gl = _gelu_tanh(tot)
        gate = jax.nn.sigmoid(_dot(gl.astype(BF16), gw_ref[...]) + gb_ref[...])
        v = gl * gate
        ms = jnp.mean(v * v, axis=-1, keepdims=True)
        y_win = v * lax.rsqrt(ms + EPS) * og_ref[...]

    if colmajor:
        o_ref[...] = y_win.reshape(o_ref.shape).astype(o_ref.dtype)
    else:
        y_ref = perm_refs[0]
        for k in range(n_lb):
            y_ref[k] = y_win[:, k * LANES:(k + 1) * LANES]
        for s in range(nsub):
            g, rows = sub_seg(s)
            for k in range(n_lb):
                o_ref[g, rows, k * LANES:(k + 1) * LANES] = (
                    y_ref[k, pl.ds(s, seg, stride=nsub), :].astype(o_ref.dtype))

    @pl.when(jw == pl.num_programs(1) - 1)
    def _():
        sout_ref[...] = carry_ref[...]


def _s5_scan(u3d, ops, ld, state_in, n_batch, lat_len, reverse, colmajor, final_args=None):
    ns2 = 2 * S5_NSTATE
    bblk, cblk, lam = ops
    if colmajor:
        seg = GRID_W
        tile = (seg, SUBLANES, S5_DIM)
        n_win = GRID_W // SUBLANES
        out_shape = (n_batch * seg, GRID_W, S5_DIM)
        imap = (lambda b, j: (b, n_win - 1 - j, 0)) if reverse else (lambda b, j: (b, j, 0))
        in_map = imap
    else:
        seg = S5_SEG
        tile = (S5_WIN // GRID_W, GRID_W, S5_DIM)
        n_win = 1
        out_shape = (n_batch * S5_WIN // GRID_W, GRID_W, S5_DIM)
        ctx_blk0 = n_batch * lat_len // S5_WIN
        imap = lambda b, j: (b, 0, 0)
        in_map = lambda b, j: (ctx_blk0 + b, 0, 0)
    win = seg * SUBLANES
    st = pl.BlockSpec((None, 1, ns2), lambda b, j: (b, 0, 0))
    stacked = lambda a: pl.BlockSpec((None,) + a.shape[1:], lambda b, j: (ld, 0, 0))
    final = final_args is not None
    in_specs = [pl.BlockSpec(tile, in_map)]
    args = [u3d]
    if final:
        in_specs.append(pl.BlockSpec(tile, imap))
        args.append(final_args[0])
    in_specs += [stacked(bblk), stacked(cblk), stacked(lam), st]
    args += [bblk, cblk, lam, state_in]
    scratch = [pltpu.VMEM((win, ns2), F32),
               pltpu.VMEM((SUBLANES, ns2), F32), pltpu.VMEM((SUBLANES, ns2), F32),
               pltpu.VMEM((1, ns2), F32)]
    if not colmajor:
        perm = pltpu.VMEM((S5_DIM // LANES, win, LANES), F32)
        scratch += [perm, perm] if final else [perm]
    if final:
        _, l, d_skip, glu_w_bf, glu_b, out_g = final_args
        for a in (d_skip,):
            a3, sp = _layer_rows(a, l)
            in_specs.append(sp)
            args.append(a3)
        gw, gw_spec = _layer_mat(glu_w_bf, l)
        in_specs.append(gw_spec)
        args.append(gw)
        for a in (glu_b, out_g):
            a3, sp = _layer_rows(a, l)
            in_specs.append(sp)
            args.append(a3)
    out, s_out = pl.pallas_call(
        functools.partial(_s5_kernel, reverse=reverse, final=final, colmajor=colmajor, seg=seg),
        out_shape=(jax.ShapeDtypeStruct(out_shape, F32),
                   jax.ShapeDtypeStruct((n_batch, 1, ns2), F32)),
        grid=(n_batch, n_win),
        in_specs=in_specs,
        out_specs=(pl.BlockSpec(tile, imap), st),
        scratch_shapes=scratch,
        compiler_params=_cparams(("arbitrary", "arbitrary")),
        name="s5_" + ("rev" if reverse else "fwd") + ("_lat" if colmajor else "_ctx"),
    )(*args)
    return out, s_out


def _s5_operands(lam_re, lam_im, log_step, b_re, b_im, c_re, c_im):
    g, p, k = S5_GROUPS, S5_STATE, S5_GROUP
    lam = lax.complex(jnp.minimum(lam_re.astype(F32), -1e-4), lam_im.astype(F32))
    step = jnp.exp(log_step.astype(F32))[:, None]
    lam_bar = jnp.exp(lam * step)
    lam_seg = jnp.exp(lam * (step * S5_SEG))
    lam_col = jnp.exp(lam * (step * GRID_W))
    b_bar = ((lam_bar - 1.0) / lam)[..., None] * lax.complex(b_re.astype(F32), b_im.astype(F32))
    def block_diag(m):
        a, b = m.shape[1], m.shape[2]
        tiled = jnp.tile(m.reshape(g * a, b), (1, g))
        own = (lax.broadcasted_iota(I32, (g * a, g * b), 0) // a
               == lax.broadcasted_iota(I32, (g * a, g * b), 1) // b)
        return jnp.where(own, tiled, 0.0)

    bd_in = lambda m: block_diag(jnp.transpose(m, (0, 2, 1)))
    bblk = jnp.concatenate([bd_in(jnp.real(b_bar)), bd_in(jnp.imag(b_bar))], axis=1)
    bd_out = lambda m: block_diag(jnp.transpose(m, (0, 2, 1)))
    cblk = jnp.concatenate([bd_out(c_re.astype(F32)), -bd_out(c_im.astype(F32))], axis=0)
    zeros = jnp.zeros((g * p,), F32)
    lam_rows = jnp.stack([jnp.real(lam_bar).reshape(-1), jnp.imag(lam_bar).reshape(-1),
                          jnp.real(lam_seg).reshape(-1), jnp.imag(lam_seg).reshape(-1),
                          jnp.real(lam_col).reshape(-1), jnp.imag(lam_col).reshape(-1), zeros, zeros])
    return bblk.astype(BF16), cblk.astype(BF16), lam_rows


def _pack_bf16_pairs(v):
    n = v.shape[1] // 2
    bits = pltpu.bitcast(v.astype(BF16).astype(F32), U32)
    return (bits[:, :n] >> 16) | (bits[:, n:] & jnp.uint32(0xFFFF0000))


def _unpack_bf16_pairs(w):
    lo = pltpu.bitcast(w << 16, F32)
    hi = pltpu.bitcast(w & jnp.uint32(0xFFFF0000), F32)
    return lo, hi


def _store_token_tiles(ref, v):
    rows = v.shape[0]
    for c in range(SUBLANES):
        ref[pl.ds(c, rows, stride=SUBLANES), :] = v[:, c * LANES:(c + 1) * LANES]


def _load_token_tiles(ref, rows):
    return [ref[pl.ds(c, rows, stride=SUBLANES), :] for c in range(SUBLANES)]


def _mixout_kernel(r_ref, a_ref, b_ref, cl_ref, cc_ref, wa_ref, wb_ref, wc_ref, g1_ref, ng_ref, sh_ref,
                   sc_ref, rw_ref, rb_ref, r1_ref, hp_ref, idx_ref, gate_ref, rank_ref, cnt_ref,
                   carry_ref, *, n_lat_tiles):
    i = pl.program_id(0)

    @pl.when(i == 0)
    def _():
        carry_ref[...] = jnp.zeros_like(carry_ref)

    mix_c = jnp.where(i < n_lat_tiles, cl_ref[...], cc_ref[...]).reshape(a_ref.shape[0], S5_DIM).astype(BF16)
    acc = _dot(a_ref[...], wa_ref[...]) + _dot(b_ref[...], wb_ref[...]) + _dot(mix_c, wc_ref[...])
    x = r_ref[...] + g1_ref[...] * acc
    r1_ref[...] = x
    ms = jnp.mean(x * x, axis=-1, keepdims=True)
    h = x * lax.rsqrt(ms + EPS) * ng_ref[...]
    h = h * (1.0 + sc_ref[...]) + sh_ref[...]
    _store_token_tiles(hp_ref, _pack_bf16_pairs(h))

    h_hi = h.astype(BF16)
    h_lo = (h - h_hi.astype(F32)).astype(BF16)
    rw = rw_ref[...]
    w_hi = rw.astype(BF16)
    w_lo = (rw - w_hi.astype(F32)).astype(BF16)
    logits = _dot(h_hi, w_hi) + _dot(h_lo, w_hi) + _dot(h_hi, w_lo) + rb_ref[...]

    tm = logits.shape[0]
    lane = lax.broadcasted_iota(I32, (tm, LANES), 1)
    lane_f = lane.astype(F32)
    work = logits
    tops, picks = [], []
    for _ in range(TOP_K):
        m = jnp.max(work, axis=-1, keepdims=True)
        pick = jnp.min(jnp.where(work == m, lane_f, float(LANES)), axis=-1, keepdims=True)
        work = jnp.where(lane_f == pick, -jnp.inf, work)
        tops.append(m)
        picks.append(pick)
    exps = [jnp.exp(t - tops[0]) for t in tops]
    denom = exps[0] + exps[1] + exps[2] + exps[3]

    onehot = jnp.zeros((tm, LANES), F32)
    for k in range(TOP_K):
        onehot = onehot + jnp.where(lane_f == picks[k], 1.0, 0.0)
    ri = lax.broadcasted_iota(I32, (tm, tm), 0)
    ci = lax.broadcasted_iota(I32, (tm, tm), 1)
    before = jnp.where(ci < ri, 1.0, 0.0).astype(BF16)
    base = carry_ref[0:1, :] + _dot(before, onehot.astype(BF16))
    carry_ref[...] = carry_ref[...] + jnp.sum(onehot, axis=0, keepdims=True)
    cnt_ref[...] = carry_ref[...]

    idx_out = jnp.zeros((tm, LANES), F32)
    gate_out = jnp.zeros((tm, LANES), F32)
    rank_out = jnp.zeros((tm, LANES), F32)
    for k in range(TOP_K):
        rank_k = jnp.sum(jnp.where(lane_f == picks[k], base, 0.0), axis=-1, keepdims=True)
        idx_out = jnp.where(lane == k, picks[k], idx_out)
        gate_out = jnp.where(lane == k, exps[k] / denom, gate_out)
        rank_out = jnp.where(lane == k, rank_k, rank_out)
    gate_ref[...] = gate_out
    idx_t = idx_out.T.astype(I32)
    rank_t = rank_out.T.astype(I32)
    for k in range(TOP_K):
        for hh in range(tm // LANES):
            row = k * (tm // LANES) + hh
            idx_ref[row:row + 1, :] = idx_t[k:k + 1, hh * LANES:(hh + 1) * LANES]
            rank_ref[row:row + 1, :] = rank_t[k:k + 1, hh * LANES:(hh + 1) * LANES]


def _mix_out(r, mix_a, mix_b, mix_c_lat, mix_c_ctx, w_out_bf, mods4, norm2_g, router_w_pad, router_b_pad,
             l, lat_len, n_batch):
    nt, d = r.shape
    tm = ROW_TILE
    n_lat_tiles = n_batch * lat_len // tm

    def grp(i):
        return jnp.minimum((i * tm) // lat_len, n_batch)

    ng3, ng_spec = _layer_rows(norm2_g, l)
    rb3, rb_spec = _layer_rows(router_b_pad, l)
    tile = lambda w: pl.BlockSpec((tm, w), lambda i: (i, 0))
    dense_rows = tm * TOP_K // LANES
    dense = pl.BlockSpec((dense_rows, LANES), lambda i: (i, 0))
    n_dense = nt * TOP_K // LANES
    return pl.pallas_call(
        functools.partial(_mixout_kernel, n_lat_tiles=n_lat_tiles),
        out_shape=(jax.ShapeDtypeStruct((nt, d), F32), jax.ShapeDtypeStruct((nt * SUBLANES, LANES), U32),
                   jax.ShapeDtypeStruct((n_dense, LANES), I32), jax.ShapeDtypeStruct((nt, LANES), F32),
                   jax.ShapeDtypeStruct((n_dense, LANES), I32), jax.ShapeDtypeStruct((SUBLANES, LANES), F32)),
        grid=(nt // tm,),
        in_specs=[tile(d), tile(SSD_INNER), tile(CONF_DIM),
                  pl.BlockSpec((tm // GRID_W, GRID_W, S5_DIM), lambda i: (jnp.minimum(i, n_lat_tiles - 1), 0, 0)),
                  pl.BlockSpec((tm // GRID_W, GRID_W, S5_DIM), lambda i: (jnp.maximum(i - n_lat_tiles, 0), 0, 0)),
                  pl.BlockSpec((None, SSD_INNER, d), lambda i: (l, 0, 0)),
                  pl.BlockSpec((None, CONF_DIM, d), lambda i: (l, SSD_INNER // CONF_DIM, 0)),
                  pl.BlockSpec((None, S5_DIM, d), lambda i: (l, (SSD_INNER + CONF_DIM) // S5_DIM, 0)),
                  _mod_spec(l, grp, 2, d), ng_spec, _mod_spec(l, grp, 3, d), _mod_spec(l, grp, 4, d),
                  pl.BlockSpec((None, d, LANES), lambda i: (l, 0, 0)), rb_spec],
        out_specs=(tile(d), pl.BlockSpec((tm * SUBLANES, LANES), lambda i: (i, 0)), dense, tile(LANES), dense,
                   pl.BlockSpec((SUBLANES, LANES), lambda i: (0, 0))),
        scratch_shapes=[pltpu.VMEM((SUBLANES, LANES), F32)],
        compiler_params=_cparams(("arbitrary",)),
        name="mix_out_router",
    )(r, mix_a, mix_b, mix_c_lat, mix_c_ctx, w_out_bf, w_out_bf, w_out_bf, mods4, ng3, mods4, mods4,
      router_w_pad, rb3)


def _expert_kernel(be_ref, nv_ref, first_ref, nxt_ref, g_cur_ref, g_nxt_ref, s_cur_ref, s_prv_ref, h_hbm,
                   wgu_hbm, bgu_ref, wd_hbm, bd_ref, ya_hbm, xbuf, ybuf, wgu_st, wd_st, wgu_bf, wd_bf, zbuf,
                   gsem, ssem, wsem, zsem, *, layer):
    i = pl.program_id(0)
    n_steps = pl.num_programs(0)
    slot = i % 2
    nv = nv_ref[i]
    blk = MOE_BLOCK
    tile_rows = blk * SUBLANES

    def weight_copies(e):
        return (pltpu.make_async_copy(wgu_hbm.at[layer, e], wgu_st, wsem.at[0]),
                pltpu.make_async_copy(wd_hbm.at[layer, e], wd_st, wsem.at[1]))

    def token_tile(ref, row0):
        return ref.at[pl.ds(pl.multiple_of(row0, SUBLANES), SUBLANES)]

    def used(j):
        return jnp.logical_and(jnp.logical_and(j >= 0, j < n_steps),
                               nv_ref[jnp.clip(j, 0, n_steps - 1)] > 0)

    n_phases = 1
    chunk = blk // n_phases
    row_priority, weight_priority = 0, 1

    def gather_rows(idx_ref, dst_slot, lo_row, n):
        def body(rr, carry):
            pltpu.make_async_copy(token_tile(h_hbm, idx_ref[0, rr]),
                                  token_tile(xbuf.at[dst_slot], rr * SUBLANES),
                                  gsem.at[dst_slot]).start(priority=row_priority)
            return carry
        lax.fori_loop(lo_row, lo_row + n, body, 0, unroll=16)

    def scatter_rows(idx_ref, src_slot, lo_row, n):
        def body(rr, carry):
            pltpu.make_async_copy(token_tile(ybuf.at[src_slot], rr * SUBLANES),
                                  token_tile(ya_hbm, idx_ref[0, rr]),
                                  ssem.at[src_slot]).start(priority=row_priority)
            return carry
        lax.fori_loop(lo_row, lo_row + n, body, 0, unroll=16)

    def scatter_wait(src_slot):
        pltpu.make_async_copy(ybuf.at[src_slot], ya_hbm.at[pl.ds(0, tile_rows)], ssem.at[src_slot]).wait()

    def issue_phase(ph):
        @pl.when(used(i + 1))
        def _():
            gather_rows(g_nxt_ref, 1 - slot, ph * chunk, chunk)

        @pl.when(used(i - 1))
        def _():
            scatter_rows(s_prv_ref, 1 - slot, ph * chunk, chunk)

    @pl.when(i == 0)
    def _():
        zbuf[...] = jnp.zeros_like(zbuf)
        for cp in weight_copies(be_ref[0]):
            cp.start(priority=weight_priority)
        gather_rows(g_cur_ref, 0, 0, blk)

    @pl.when(nv > 0)
    def _():
        @pl.when(first_ref[i] == 1)
        def _():
            for cp in weight_copies(be_ref[i]):
                cp.wait()
            wgu_bf[...] = wgu_st[...].astype(BF16)
            wd_bf[...] = wd_st[...].astype(BF16)

            @pl.when(nxt_ref[i] >= 0)
            def _():
                for cp in weight_copies(nxt_ref[i]):
                    cp.start(priority=weight_priority)

        pltpu.make_async_copy(h_hbm.at[pl.ds(0, tile_rows)], xbuf.at[slot], gsem.at[slot]).wait()
        half = D_MODEL // 2
        xw = jnp.concatenate(_load_token_tiles(xbuf.at[slot], blk), axis=1)
        lo, hi = _unpack_bf16_pairs(xw)
        issue_phase(0)
        gu = (_dot(lo.astype(BF16), wgu_bf[0:half, :]) + _dot(hi.astype(BF16), wgu_bf[half:D_MODEL, :])
              + bgu_ref[...])
        gate = jnp.minimum(gu[:, :D_EXPERT], SWIGLU_LIMIT)
        lin = jnp.clip(gu[:, D_EXPERT:], -SWIGLU_LIMIT, SWIGLU_LIMIT)
        act = (gate * jax.nn.sigmoid(SWIGLU_ALPHA * gate) * (lin + 1.0)).astype(BF16)
        y = _dot(act, wd_bf[...]) + bd_ref[...]

        @pl.when(used(i - 2))
        def _():
            scatter_wait(slot)

        _store_token_tiles(ybuf.at[slot], _pack_bf16_pairs(y))

    @pl.when(nv == 0)
    def _():
        @pl.when(used(i - 2))
        def _():
            scatter_wait(slot)

        @pl.when(used(i - 1))
        def _():
            scatter_rows(s_prv_ref, 1 - slot, 0, blk)

        own = ya_hbm.at[pl.ds(pl.multiple_of(i * tile_rows, tile_rows), tile_rows)]
        zero_copy = pltpu.make_async_copy(zbuf, own, zsem)
        zero_copy.start()
        zero_copy.wait()

    @pl.when(i == n_steps - 1)
    def _():
        @pl.when(nv > 0)
        def _():
            scatter_rows(s_cur_ref, slot, 0, blk)
            scatter_wait(slot)

        @pl.when(used(i - 1))
        def _():
            scatter_wait(1 - slot)


def _experts(h_tiles, gather_row, scatter_row, block_e, n_valid, first, nxt, w_gu, b_gu, w_down, b_down,
             layer):
    n_rows = gather_row.shape[0]
    n_blocks = n_rows // MOE_BLOCK
    depth, ne = w_gu.shape[:2]
    g3 = gather_row.reshape(n_blocks, 1, MOE_BLOCK)
    s3 = scatter_row.reshape(n_blocks, 1, MOE_BLOCK)
    width = D_MODEL // 2
    tile_rows = MOE_BLOCK * SUBLANES
    idx_block = lambda imap: pl.BlockSpec((None, 1, MOE_BLOCK), imap, memory_space=pltpu.SMEM)
    grid_spec = pltpu.PrefetchScalarGridSpec(
        num_scalar_prefetch=4,
        grid=(n_blocks,),
        in_specs=[idx_block(lambda i, *_: (i, 0, 0)),
                  idx_block(lambda i, *_: (jnp.minimum(i + 1, n_blocks - 1), 0, 0)),
                  idx_block(lambda i, *_: (i, 0, 0)),
                  idx_block(lambda i, *_: (jnp.maximum(i - 1, 0), 0, 0)),
                  pl.BlockSpec(memory_space=pl.ANY),
                  pl.BlockSpec(memory_space=pl.ANY),
                  pl.BlockSpec((None, None, 1, 2 * D_EXPERT), lambda i, be, *_: (layer, be[i], 0, 0)),
                  pl.BlockSpec(memory_space=pl.ANY),
                  pl.BlockSpec((None, None, 1, D_MODEL), lambda i, be, *_: (layer, be[i], 0, 0))],
        out_specs=pl.BlockSpec(memory_space=pl.ANY),
        scratch_shapes=[pltpu.VMEM((2, tile_rows, LANES), U32), pltpu.VMEM((2, tile_rows, LANES), U32),
                        pltpu.VMEM((D_MODEL, 2 * D_EXPERT), F32), pltpu.VMEM((D_EXPERT, D_MODEL), F32),
                        pltpu.VMEM((D_MODEL, 2 * D_EXPERT), BF16), pltpu.VMEM((D_EXPERT, D_MODEL), BF16),
                        pltpu.VMEM((tile_rows, LANES), U32),
                        pltpu.SemaphoreType.DMA((2,)), pltpu.SemaphoreType.DMA((2,)),
                        pltpu.SemaphoreType.DMA((2,)), pltpu.SemaphoreType.DMA],
    )
    return pl.pallas_call(
        functools.partial(_expert_kernel, layer=layer),
        out_shape=jax.ShapeDtypeStruct((n_rows * SUBLANES, LANES), U32),
        grid_spec=grid_spec,
        compiler_params=_cparams(("arbitrary",)),
        name="moe_experts",
    )(block_e, n_valid, first, nxt, g3, g3, s3, s3, h_tiles, w_gu, b_gu.reshape(depth, ne, 1, -1), w_down,
      b_down.reshape(depth, ne, 1, -1))


def _combine_kernel(r_ref, gate_ref, y0_ref, y1_ref, y2_ref, y3_ref, g2_ref, fg_ref, o_ref, *, last_layer):
    tm = r_ref.shape[0]
    half = D_MODEL // 2
    gates = [gate_ref[:, k:k + 1] for k in range(TOP_K)]
    y_refs = (y0_ref, y1_ref, y2_ref, y3_ref)
    for c in range(SUBLANES):
        acc_lo = jnp.zeros((tm, LANES), F32)
        acc_hi = jnp.zeros((tm, LANES), F32)
        for k in range(TOP_K):
            lo, hi = _unpack_bf16_pairs(y_refs[k][pl.ds(c, tm, stride=SUBLANES), :])
            acc_lo = acc_lo + gates[k] * lo
            acc_hi = acc_hi + gates[k] * hi
        lo_cols = slice(c * LANES, (c + 1) * LANES)
        hi_cols = slice(half + c * LANES, half + (c + 1) * LANES)
        o_ref[:, lo_cols] = r_ref[:, lo_cols] + g2_ref[:, lo_cols] * acc_lo
        o_ref[:, hi_cols] = r_ref[:, hi_cols] + g2_ref[:, hi_cols] * acc_hi
    if last_layer:
        x = o_ref[...]
        ms = jnp.mean(x * x, axis=-1, keepdims=True)
        o_ref[...] = x * lax.rsqrt(ms + EPS) * fg_ref[...]


def _combine(r1, gates, ya, mods4, final_g, l, lat_len, n_batch, last_layer):
    nt, d = r1.shape
    tm = ROW_TILE
    n_out = n_batch * lat_len if last_layer else nt
    tiles_per_k = nt // tm

    def grp(i):
        return jnp.minimum((i * tm) // lat_len, n_batch)

    def choice(k):
        return pl.BlockSpec((tm * SUBLANES, LANES), lambda i: (k * tiles_per_k + i, 0))

    return pl.pallas_call(
        functools.partial(_combine_kernel, last_layer=last_layer),
        out_shape=jax.ShapeDtypeStruct((n_out, d), F32),
        grid=(n_out // tm,),
        in_specs=[pl.BlockSpec((tm, d), lambda i: (i, 0)),
                  pl.BlockSpec((tm, LANES), lambda i: (i, 0)),
                  choice(0), choice(1), choice(2), choice(3),
                  _mod_spec(l, grp, 5, d),
                  pl.BlockSpec((1, d), lambda i: (0, 0))],
        out_specs=pl.BlockSpec((tm, d), lambda i: (i, 0)),
        compiler_params=_cparams(("arbitrary",)),
        name="moe_combine",
    )(r1, gates, ya, ya, ya, ya, mods4, final_g.reshape(1, d))


def _routing_plan(top_idx, rank, counts, n_blocks):
    n_assign = top_idx.size
    n_tok = n_assign // TOP_K
    n_rows = n_blocks * MOE_BLOCK
    counts = counts.astype(I32)
    padded = (counts + MOE_BLOCK - 1) // MOE_BLOCK * MOE_BLOCK
    pad_end = jnp.cumsum(padded)
    pad_start = pad_end - padded
    count_end = jnp.cumsum(counts)
    per_tile = ROW_TILE * TOP_K // LANES
    q = lax.broadcasted_iota(I32, top_idx.shape, 0)
    lane = lax.broadcasted_iota(I32, top_idx.shape, 1)
    halves = ROW_TILE // LANES
    token = (q // per_tile) * ROW_TILE + (q % halves) * LANES + lane
    choice = (q % per_tile) // halves
    out_tile = choice * n_tok + token
    dest = pad_start[top_idx] + rank
    blk_start = jnp.arange(n_blocks, dtype=I32) * MOE_BLOCK

    def expert_at(pos):
        return jnp.minimum(jnp.sum((pad_end[None, :] <= pos[:, None]).astype(I32), axis=1), N_EXPERTS - 1)

    block_e = expert_at(blk_start)
    used = blk_start < pad_end[-1]
    n_valid = jnp.where(used, jnp.clip(pad_start[block_e] + counts[block_e] - blk_start, 0, MOE_BLOCK), 0)
    first = jnp.logical_and(used, blk_start == pad_start[block_e]).astype(I32)
    nxt_start = pad_end[block_e]
    nxt = jnp.where(nxt_start < pad_end[-1], expert_at(nxt_start), -1).astype(I32)

    row = jnp.arange(n_rows, dtype=I32)
    dump = n_assign + row - jnp.repeat(count_end[block_e], MOE_BLOCK)
    row_tile = dump.at[dest].set(out_tile, unique_indices=True, mode='drop')
    gather_row = (row_tile % n_tok) * SUBLANES
    scatter_row = row_tile * SUBLANES
    return gather_row.astype(I32), scatter_row.astype(I32), block_e.astype(I32), n_valid.astype(I32), first, nxt


def _forward(x, c, ctx, c_ctx, ada_w, ada_b, norm1_g, w_in, ssd_conv_w, ssd_conv_b, ssd_a_log,
             ssd_dt_bias, ssd_d, ssd_norm_g, conf_dw_w, conf_dw_b, conf_ln_g, conf_ln_b, conf_pw_w,
             conf_pw_b, conf_out_g, s5_lam_re, s5_lam_im, s5_log_step, s5_b_re, s5_b_im, s5_c_re,
             s5_c_im, s5_d, s5_glu_w, s5_glu_b, s5_out_g, w_out, norm2_g, router_w, router_b,
             w_gate_up, b_gate_up, w_down, b_down, final_norm_g):
    n_batch, lat_len, d = x.shape
    ctx_len = ctx.shape[1]
    depth = ada_w.shape[0]
    n_lat = n_batch * lat_len
    nt = n_lat + n_batch * ctx_len
    assert d == D_MODEL and ctx_len == ROW_TILE and lat_len % MM_TILE_M == 0
    assert lat_len // GRID_W == GRID_W

    r = jnp.concatenate([x.reshape(n_lat, d), ctx.reshape(n_batch * ctx_len, d)], axis=0).astype(F32)
    cond = jnp.zeros((SUBLANES, d), F32).at[:n_batch].set(c).at[n_batch].set(c_ctx)
    mods4 = _modulation(cond, ada_w, ada_b).reshape(depth, SUBLANES, 1, 6 * d)

    c_dt = SSD_INNER + SSD_CONV_DIM
    c_conf = c_dt + SSD_HEADS
    c_s5 = c_conf + 2 * CONF_DIM
    w_proj = jnp.concatenate(
        [w_in[:, :, :c_dt], w_in[:, :, c_conf:c_s5], w_in[:, :, c_s5:], w_in[:, :, c_dt:c_conf],
         jnp.zeros((depth, d, MM_TILE_N - S5_DIM - SSD_HEADS), w_in.dtype)], axis=2).astype(BF16)
    w_out_bf = w_out.astype(BF16)
    conf_pw_bf = conf_pw_w.astype(BF16)
    s5_glu_bf = s5_glu_w.astype(BF16)
    head_pad = lambda a: jnp.pad(a.astype(F32), ((0, 0), (0, 0), (0, LANES - SSD_HEADS))).reshape(
        depth * 2, 1, LANES)
    dtb_all, alog_all = head_pad(ssd_dt_bias), head_pad(ssd_a_log)
    d_skip_all = jnp.repeat(ssd_d.astype(F32), SSD_HEAD_DIM, axis=1)
    router_w_pad = jnp.pad(router_w.astype(F32), ((0, 0), (0, 0), (0, LANES - N_EXPERTS)))
    router_b_pad = jnp.pad(router_b.astype(F32), ((0, 0), (0, LANES - N_EXPERTS)), constant_values=-1e30)
    flat2 = lambda a: a.reshape((depth * 2,) + a.shape[2:])
    s5_ops = jax.vmap(_s5_operands)(*[flat2(a) for a in (s5_lam_re, s5_lam_im, s5_log_step, s5_b_re,
                                                         s5_b_im, s5_c_re, s5_c_im)])

    n_blocks = nt * TOP_K // MOE_BLOCK + N_EXPERTS
    s5_zero = jnp.zeros((n_batch, 1, 2 * S5_NSTATE), F32)

    for l in range(depth):
        p_main, p_s5, p_dt = _in_proj(r, norm1_g, mods4, l, w_proj, lat_len, n_batch)

        xbc = _ssd_conv(p_main, ssd_conv_w, ssd_conv_b, l, lat_len, ctx_len, n_batch)
        y_fwd = _ssd_scan(xbc, p_main, p_dt, dtb_all, alog_all, l, lat_len, ctx_len, n_batch, False)
        mix_a = _ssd_scan(xbc, p_main, p_dt, dtb_all, alog_all, l, lat_len, ctx_len, n_batch, True,
                          (y_fwd, d_skip_all, ssd_norm_g))

        mix_b = _conformer(p_main, conf_dw_w, conf_dw_b, conf_ln_g, conf_ln_b, conf_pw_bf, conf_pw_b,
                           conf_out_g, l, lat_len, ctx_len, n_batch)

        yc_f, st_f = _s5_scan(p_s5, s5_ops, 2 * l, s5_zero, n_batch, lat_len, False, False)
        yl_f, _ = _s5_scan(p_s5, s5_ops, 2 * l, st_f, n_batch, lat_len, False, True)
        fin = (l, s5_d, s5_glu_bf, s5_glu_b, s5_out_g)
        mc_c, st_r = _s5_scan(p_s5, s5_ops, 2 * l + 1, s5_zero, n_batch, lat_len, True, False, (yc_f,) + fin)
        mc_l, _ = _s5_scan(p_s5, s5_ops, 2 * l + 1, st_r, n_batch, lat_len, True, True, (yl_f,) + fin)

        r1, h_tiles, top_idx, gates, rank, counts = _mix_out(
            r, mix_a, mix_b, mc_l, mc_c, w_out_bf, mods4, norm2_g, router_w_pad,
            router_b_pad, l, lat_len, n_batch)

        gather_row, scatter_row, block_e, n_valid, first, nxt = _routing_plan(
            top_idx, rank, counts[0, :N_EXPERTS], n_blocks)
        ya = _experts(h_tiles, gather_row, scatter_row, block_e, n_valid, first, nxt, w_gate_up, b_gate_up,
                      w_down, b_down, l)
        r = _combine(r1, gates, ya, mods4, final_norm_g, l, lat_len, n_batch, l == depth - 1)

    return r.reshape(n_batch, lat_len, d).astype(x.dtype)


def kernel(x, c, ctx, c_ctx, ada_w, ada_b, norm1_g, w_in, ssd_conv_w, ssd_conv_b, ssd_a_log, ssd_dt_bias,
           ssd_d, ssd_norm_g, conf_dw_w, conf_dw_b, conf_ln_g, conf_ln_b, conf_pw_w, conf_pw_b, conf_out_g,
           s5_lam_re, s5_lam_im, s5_log_step, s5_b_re, s5_b_im, s5_c_re, s5_c_im, s5_d, s5_glu_w, s5_glu_b,
           s5_out_g, w_out, norm2_g, router_w, router_b, w_gate_up, b_gate_up, w_down, b_down, final_norm_g):
    return _forward(x, c, ctx, c_ctx, ada_w, ada_b, norm1_g, w_in, ssd_conv_w, ssd_conv_b, ssd_a_log,
                    ssd_dt_bias, ssd_d, ssd_norm_g, conf_dw_w, conf_dw_b, conf_ln_g, conf_ln_b, conf_pw_w,
                    conf_pw_b, conf_out_g, s5_lam_re, s5_lam_im, s5_log_step, s5_b_re, s5_b_im, s5_c_re,
                    s5_c_im, s5_d, s5_glu_w, s5_glu_b, s5_out_g, w_out, norm2_g, router_w, router_b,
                    w_gate_up, b_gate_up, w_down, b_down, final_norm_g)
```

```python
import functools
import math

import jax
import jax.numpy as jnp
from jax import lax
from jax.experimental import pallas as pl
from jax.experimental.pallas import tpu as pltpu

F32 = jnp.float32
BF16 = jnp.bfloat16
I32 = jnp.int32
U32 = jnp.uint32

D_MODEL = 2048
GRID_W = 64
SSD_INNER = 1024
SSD_HEAD_DIM = 64
SSD_HEADS = 16
SSD_GROUPS = 4
SSD_STATE = 128
SSD_CONV = 5
SSD_CHUNK = 128
SSD_CONV_DIM = SSD_INNER + 2 * SSD_GROUPS * SSD_STATE
CONF_DIM = 512
CONF_KERNEL = 31
S5_DIM = 512
S5_GROUP = 16
S5_GROUPS = 32
S5_STATE = 64
S5_NSTATE = S5_GROUPS * S5_STATE
N_EXPERTS = 32
TOP_K = 4
D_EXPERT = 768
SWIGLU_LIMIT = 7.0
SWIGLU_ALPHA = 1.702
MOE_BLOCK = 256
EPS = 1e-6

LANES = 128
SUBLANES = 8
ROW_TILE = 256
MM_TILE_M = 512
MM_TILE_N = 1024
MAIN_COLS = 4096
COL_Z, COL_X, COL_CONF = 0, 1024, 3072
SIDE_COLS = S5_DIM + LANES
CONV_HALO = 16
CONF_HALO = 16
S5_SEG = 32
S5_WIN = SUBLANES * S5_SEG
VMEM_LIMIT = 56 * 1024 * 1024


def _cparams(sem, vmem=VMEM_LIMIT):
    return pltpu.CompilerParams(dimension_semantics=sem, vmem_limit_bytes=vmem)


def _silu(v):
    return v * jax.nn.sigmoid(v)


def _split3(v):
    hi = v.astype(BF16)
    r1 = v - hi.astype(F32)
    mid = r1.astype(BF16)
    lo = (r1 - mid.astype(F32)).astype(BF16)
    return hi, mid, lo


def _dot(a, b):
    return jnp.dot(a, b, preferred_element_type=F32)


def _layer_rows(arr, l):
    depth = arr.shape[0]
    a3 = arr.reshape(depth, 1, -1)
    return a3, pl.BlockSpec((None, 1, a3.shape[2]), lambda *_: (l, 0, 0))


def _layer_mat(arr, l):
    return arr, pl.BlockSpec((None,) + arr.shape[1:], lambda *_: (l, 0, 0))


def _mod_kernel(c_ref, w_ref, b_ref, o_ref):
    s = _silu(c_ref[...])
    w = w_ref[...]
    s_hi = s.astype(BF16)
    s_lo = (s - s_hi.astype(F32)).astype(BF16)
    w_hi = w.astype(BF16)
    w_lo = (w - w_hi.astype(F32)).astype(BF16)
    acc = _dot(s_hi, w_hi) + _dot(s_lo, w_hi) + _dot(s_hi, w_lo)
    o_ref[...] = acc + b_ref[...]


def _modulation(cond, ada_w, ada_b):
    depth, d, n = ada_w.shape
    tn = 1024
    return pl.pallas_call(
        _mod_kernel,
        out_shape=jax.ShapeDtypeStruct((depth, SUBLANES, n), F32),
        grid=(depth, n // tn),
        in_specs=[pl.BlockSpec((SUBLANES, d), lambda l, j: (0, 0)),
                  pl.BlockSpec((None, d, tn), lambda l, j: (l, 0, j)),
                  pl.BlockSpec((None, 1, tn), lambda l, j: (l, 0, j))],
        out_specs=pl.BlockSpec((None, SUBLANES, tn), lambda l, j: (l, 0, j)),
        compiler_params=_cparams(("arbitrary", "arbitrary")),
        name="adaln_mod",
    )(cond, ada_w, ada_b.reshape(depth, 1, n))


def _mod_spec(l, grp, k, d):
    return pl.BlockSpec((None, None, 1, d), lambda i, *_: (l, grp(i), 0, k))


def _inproj_kernel(x_ref, g_ref, sh_ref, sc_ref, w_ref, main_ref, s5_ref, dt_ref, hn_ref, *, n_tiles,
                   n_main):
    j = pl.program_id(1)
    s = pl.program_id(2)
    live = 2 * pl.program_id(0) + s < n_tiles

    @pl.when(jnp.logical_and(j == 0, live))
    def _():
        x = x_ref[...]
        ms = jnp.mean(x * x, axis=-1, keepdims=True)
        y = x * lax.rsqrt(ms + EPS) * g_ref[...]
        hn_ref[s] = (y * (1.0 + sc_ref[...]) + sh_ref[...]).astype(BF16)

    @pl.when(jnp.logical_and(live, j < n_main))
    def _():
        main_ref[...] = _dot(hn_ref[s], w_ref[...]).astype(main_ref.dtype)

    @pl.when(jnp.logical_and(live, j == n_main))
    def _():
        res = _dot(hn_ref[s], w_ref[...])
        s5_ref[...] = res[:, 0:S5_DIM].reshape(s5_ref.shape)
        dt_ref[...] = res[:, S5_DIM:S5_DIM + LANES]


def _in_proj(r, gains, mods4, l, w_all, lat_len, n_batch):
    nt, d = r.shape
    tm = MM_TILE_M if nt % MM_TILE_M == 0 else ROW_TILE
    tn = MM_TILE_N
    n_main = MAIN_COLS // tn
    assert w_all.shape[2] == (n_main + 1) * tn and SIDE_COLS <= tn
    n_tiles = nt // tm
    last = n_tiles - 1

    def tile(p, s):
        return jnp.minimum(2 * p + s, last)

    def pair_end(p):
        return jnp.minimum(2 * p + 1, last)

    def x_tile(p, j, s):
        return jnp.where(j == 0, tile(p, s), pair_end(p))

    def main_tile(p, j, s):
        return jnp.where(j < n_main, tile(p, s), pair_end(p))

    def side_tile(p, j, s):
        return jnp.where(j == n_main, tile(p, s), jnp.maximum(2 * p - 1, 0))

    def grp(t):
        return jnp.minimum((t * tm) // lat_len, n_batch)

    def mod_spec(k):
        return pl.BlockSpec((None, None, 1, d), lambda p, j, s: (l, grp(tile(p, s)), 0, k))

    g3, g_spec = _layer_rows(gains, l)
    return pl.pallas_call(
        functools.partial(_inproj_kernel, n_tiles=n_tiles, n_main=n_main),
        out_shape=(jax.ShapeDtypeStruct((nt, MAIN_COLS), BF16),
                   jax.ShapeDtypeStruct((nt // GRID_W, GRID_W, S5_DIM), F32),
                   jax.ShapeDtypeStruct((nt, LANES), F32)),
        grid=(pl.cdiv(n_tiles, 2), n_main + 1, 2),
        in_specs=[pl.BlockSpec((tm, d), lambda p, j, s: (x_tile(p, j, s), 0)),
                  g_spec, mod_spec(0), mod_spec(1),
                  pl.BlockSpec((None, d, tn), lambda p, j, s: (l, 0, j))],
        out_specs=(pl.BlockSpec((tm, tn), lambda p, j, s: (main_tile(p, j, s), jnp.minimum(j, n_main - 1))),
                   pl.BlockSpec((tm // GRID_W, GRID_W, S5_DIM), lambda p, j, s: (side_tile(p, j, s), 0, 0)),
                   pl.BlockSpec((tm, LANES), lambda p, j, s: (side_tile(p, j, s), 0))),
        scratch_shapes=[pltpu.VMEM((2, tm, d), BF16)],
        compiler_params=_cparams(("arbitrary", "arbitrary", "arbitrary")),
        name="in_proj",
    )(r, g3, mods4, mods4, w_all)


def _seq_edges(i, tiles_per_lat_seq, n_lat_tiles):
    is_lat = i < n_lat_tiles
    first = jnp.logical_or(jnp.logical_not(is_lat), (i % tiles_per_lat_seq) == 0)
    last = jnp.logical_or(jnp.logical_not(is_lat), (i % tiles_per_lat_seq) == tiles_per_lat_seq - 1)
    return first, last


def _conv5_kernel(cur_ref, prev_ref, next_ref, w_ref, b_ref, o_ref, ext_ref, *, tiles_per_lat_seq,
                  n_lat_tiles):
    i = pl.program_id(0)
    first, last = _seq_edges(i, tiles_per_lat_seq, n_lat_tiles)
    h, tm = CONV_HALO, ROW_TILE
    ext_ref[0:h, :] = jnp.where(first, 0.0, prev_ref[...].astype(F32))
    ext_ref[h:h + tm, :] = cur_ref[...].astype(F32)
    ext_ref[h + tm:h + tm + h, :] = jnp.where(last, 0.0, next_ref[...].astype(F32))
    pad = (SSD_CONV - 1) // 2
    cw = 512
    for c in range(0, ext_ref.shape[1], cw):
        acc = jnp.broadcast_to(b_ref[:, c:c + cw], (tm, cw))
        for j in range(SSD_CONV):
            acc = acc + w_ref[j:j + 1, c:c + cw] * ext_ref[h - pad + j:h - pad + j + tm, c:c + cw]
        o_ref[:, c:c + cw] = _silu(acc).astype(BF16)


def _ssd_conv(p_main, conv_w, conv_b, l, lat_len, ctx_len, n_batch):
    nt = p_main.shape[0]
    tm, h = ROW_TILE, CONV_HALO
    assert ctx_len == tm and lat_len % tm == 0
    n_lat_tiles = n_batch * lat_len // tm
    cw = SSD_CONV_DIM // 2
    xblk = COL_X // cw
    nhb = nt // h
    depth = conv_w.shape[0]
    kern = functools.partial(_conv5_kernel, tiles_per_lat_seq=lat_len // tm, n_lat_tiles=n_lat_tiles)
    return pl.pallas_call(
        kern,
        out_shape=jax.ShapeDtypeStruct((nt, SSD_CONV_DIM), BF16),
        grid=(nt // tm, 2),
        in_specs=[pl.BlockSpec((tm, cw), lambda i, j: (i, xblk + j)),
                  pl.BlockSpec((h, cw), lambda i, j: (jnp.maximum(i * (tm // h) - 1, 0), xblk + j)),
                  pl.BlockSpec((h, cw), lambda i, j: (jnp.minimum((i + 1) * (tm // h), nhb - 1), xblk + j)),
                  pl.BlockSpec((None, SSD_CONV, cw), lambda i, j: (l, 0, j)),
                  pl.BlockSpec((None, 1, cw), lambda i, j: (l, 0, j))],
        out_specs=pl.BlockSpec((tm, cw), lambda i, j: (i, j)),
        scratch_shapes=[pltpu.VMEM((tm + 2 * h, cw), F32)],
        compiler_params=_cparams(("arbitrary", "arbitrary")),
        name="ssd_conv",
    )(p_main, p_main, p_main, conv_w, conv_b.reshape(depth, 1, -1))


def _head_cols(vals, width):
    lane = lax.broadcasted_iota(I32, (1, LANES), 1)
    halves = []
    per_half = LANES // width
    for hh in range(len(vals) // per_half):
        sel = vals[hh * per_half + per_half - 1]
        for k in range(per_half - 2, -1, -1):
            sel = jnp.where(lane < (k + 1) * width, vals[hh * per_half + k], sel)
        halves.append(sel)
    return jnp.concatenate(halves, axis=1)


def _ssd_kernel(*refs, reverse, final):
    if final:
        (x_ref, b_ref, c_ref, dt_ref, dtb_ref, alog_ref, hx_ref, yprev_ref, z_ref, dskip_ref, ng_ref,
         o_ref, state_ref, tmp_ref) = refs
    else:
        x_ref, b_ref, c_ref, dt_ref, dtb_ref, alog_ref, hx_ref, o_ref, state_ref = refs
    q = SSD_CHUNK
    r = SSD_HEADS // SSD_GROUPS
    gw = r * SSD_HEAD_DIM

    @pl.when(pl.program_id(1) == 0)
    def _():
        state_ref[...] = jnp.zeros_like(state_ref)

    lane = lax.broadcasted_iota(I32, (1, LANES), 1)
    dtp = jax.nn.softplus(dt_ref[...] + dtb_ref[...])
    a_head = -jnp.exp(alog_ref[...])
    a = jnp.where(lane < SSD_HEADS, dtp * a_head, 0.0)
    ri = lax.broadcasted_iota(I32, (q, q), 0)
    ci = lax.broadcasted_iota(I32, (q, q), 1)
    tri = (ci >= ri) if reverse else (ci <= ri)
    tri_b = jnp.where(tri, 1.0, 0.0).astype(BF16)
    a_hi, a_mid, a_lo = _split3(a)
    a_cs = _dot(tri_b, a_hi) + _dot(tri_b, a_mid) + _dot(tri_b, a_lo)
    a_cs_t = a_cs.T
    dtp_t = dtp.T
    a_end = a_cs[0:1, :] if reverse else a_cs[q - 1:q, :]
    lane_g = lax.broadcasted_iota(I32, (1, gw), 1)

    pieces = []
    for fac in (jnp.exp(a_cs), dtp * jnp.exp(a_end - a_cs)):
        hi = fac.astype(BF16)
        pieces += [hi, (fac - hi.astype(F32)).astype(BF16)]
    spread = _dot(jnp.concatenate(pieces, axis=0), hx_ref[...])
    e_all = spread[0:q] + spread[q:2 * q]
    w_all = spread[2 * q:3 * q] + spread[3 * q:4 * q]

    x = x_ref[...]
    for g in range(SSD_GROUPS):
        cg = c_ref[:, g * SSD_STATE:(g + 1) * SSD_STATE]
        bg = b_ref[:, g * SSD_STATE:(g + 1) * SSD_STATE]
        cb = lax.dot_general(cg, bg, (((1,), (1,)), ((), ())), preferred_element_type=F32)
        xg = x[:, g * gw:(g + 1) * gw]
        yg = jnp.zeros((q, gw), F32)
        dec = []
        for hl in range(r):
            h = g * r + hl
            col = a_cs[:, h:h + 1]
            row = a_cs_t[h:h + 1, :]
            lm = jnp.where(tri, jnp.exp(col - row), 0.0)
            m = (cb * lm * dtp_t[h:h + 1, :]).astype(BF16)
            in_head = jnp.logical_and(lane_g >= hl * SSD_HEAD_DIM, lane_g < (hl + 1) * SSD_HEAD_DIM)
            xm = jnp.where(in_head, xg, jnp.zeros_like(xg))
            yg = yg + _dot(m, xm)
            dec.append(jnp.exp(a_end[:, h:h + 1]))
        s_old = state_ref[g]
        gs = slice(g * gw, (g + 1) * gw)
        yg = yg + e_all[:, gs] * _dot(cg, s_old.astype(BF16))
        xw = (xg.astype(F32) * w_all[:, gs]).astype(BF16)
        upd = lax.dot_general(bg, xw, (((0,), (0,)), ((), ())), preferred_element_type=F32)
        state_ref[g] = _head_cols(dec, SSD_HEAD_DIM) * s_old + upd
        sl = slice(g * gw, (g + 1) * gw)
        if final:
            ytot = yprev_ref[:, sl] + yg + dskip_ref[:, sl] * xg.astype(F32)
            tmp_ref[:, sl] = ytot * _silu(z_ref[:, sl].astype(F32))
        else:
            o_ref[:, sl] = yg
    if final:
        gated = tmp_ref[...]
        ms = jnp.mean(gated * gated, axis=-1, keepdims=True)
        o_ref[...] = (gated * lax.rsqrt(ms + EPS) * ng_ref[...]).astype(o_ref.dtype)


def _ssd_scan(xbc, p_main, p_dt, dtb_all, alog_all, l, lat_len, ctx_len, n_batch, reverse, final_args=None):
    nt = xbc.shape[0]
    q = SSD_CHUNK
    ncl, ncc = lat_len // q, ctx_len // q
    ctx0 = n_batch * ncl
    dd = 1 if reverse else 0

    def blk(b, j):
        if reverse:
            return jnp.where(j < ncc, ctx0 + b * ncc + (ncc - 1 - j), b * ncl + (ncl - 1 - (j - ncc)))
        return jnp.where(j < ncc, ctx0 + b * ncc + j, b * ncl + (j - ncc))

    head_vec = pl.BlockSpec((None, 1, LANES), lambda b, j: (2 * l + dd, 0, 0))
    final = final_args is not None
    in_specs = [pl.BlockSpec((q, SSD_INNER), lambda b, j: (blk(b, j), 0)),
                pl.BlockSpec((q, SSD_GROUPS * SSD_STATE), lambda b, j: (blk(b, j), 2)),
                pl.BlockSpec((q, SSD_GROUPS * SSD_STATE), lambda b, j: (blk(b, j), 3)),
                pl.BlockSpec((q, LANES), lambda b, j: (blk(b, j), 0)),
                head_vec, head_vec,
                pl.BlockSpec((LANES, SSD_INNER), lambda b, j: (0, 0))]
    head_spread = (jnp.arange(SSD_INNER, dtype=I32)[None, :] // SSD_HEAD_DIM
                   == jnp.arange(LANES, dtype=I32)[:, None]).astype(BF16)
    args = [xbc, xbc, xbc, p_dt, dtb_all, alog_all, head_spread]
    scratch = [pltpu.VMEM((SSD_GROUPS, SSD_STATE, (SSD_HEADS // SSD_GROUPS) * SSD_HEAD_DIM), F32)]
    if final:
        y_prev, d_skip_all, norm_g_all = final_args
        ds3, ds_spec = _layer_rows(d_skip_all, l)
        ng3, ng_spec = _layer_rows(norm_g_all, l)
        in_specs += [pl.BlockSpec((q, SSD_INNER), lambda b, j: (blk(b, j), 0)),
                     pl.BlockSpec((q, SSD_INNER), lambda b, j: (blk(b, j), COL_Z // SSD_INNER)),
                     ds_spec, ng_spec]
        args += [y_prev, p_main, ds3, ng3]
        scratch.append(pltpu.VMEM((q, SSD_INNER), F32))
    return pl.pallas_call(
        functools.partial(_ssd_kernel, reverse=reverse, final=final),
        out_shape=jax.ShapeDtypeStruct((nt, SSD_INNER), BF16 if final else F32),
        grid=(n_batch, ncc + ncl),
        in_specs=in_specs,
        out_specs=pl.BlockSpec((q, SSD_INNER), lambda b, j: (blk(b, j), 0)),
        scratch_shapes=scratch,
        compiler_params=_cparams(("arbitrary", "arbitrary")),
        name="ssd_scan_rev" if reverse else "ssd_scan_fwd",
    )(*args)


def _glu(ref):
    v = ref[:, 0:CONF_DIM].astype(F32)
    gt = ref[:, CONF_DIM:2 * CONF_DIM].astype(F32)
    return v * jax.nn.sigmoid(gt)


def _conf_kernel(cur_ref, prev_ref, next_ref, dww_ref, dwb_ref, lng_ref, lnb_ref, pww_ref, pwb_ref,
                 og_ref, o_ref, ext_ref, acc_ref, sh_ref, *, tiles_per_lat_seq, n_lat_tiles):
    i = pl.program_id(0)
    first, last = _seq_edges(i, tiles_per_lat_seq, n_lat_tiles)
    h, tm = CONF_HALO, ROW_TILE
    ext_ref[0:h, :] = jnp.where(first, 0.0, _glu(prev_ref))
    ext_ref[h:h + tm, :] = _glu(cur_ref)
    ext_ref[h + tm:h + tm + h, :] = jnp.where(last, 0.0, _glu(next_ref))
    pad = (CONF_KERNEL - 1) // 2
    span = (CONF_KERNEL - 1) // SUBLANES * SUBLANES + tm
    for b in range(SUBLANES):
        sh_ref[b] = ext_ref[h - pad + b:h - pad + b + span, :]
    cw = 256
    for c in range(0, CONF_DIM, cw):
        acc = jnp.broadcast_to(dwb_ref[:, c:c + cw], (tm, cw))
        for j in range(CONF_KERNEL):
            a8 = j // SUBLANES * SUBLANES
            acc = acc + dww_ref[j:j + 1, c:c + cw] * sh_ref[j % SUBLANES, a8:a8 + tm, c:c + cw]
        acc_ref[:, c:c + cw] = acc
    u = acc_ref[...]
    mu = jnp.mean(u, axis=-1, keepdims=True)
    var = jnp.mean(jnp.square(u - mu), axis=-1, keepdims=True)
    y = (u - mu) * lax.rsqrt(var + EPS) * lng_ref[...] + lnb_ref[...]
    y = _silu(y)
    v = _dot(y.astype(BF16), pww_ref[...]) + pwb_ref[...]
    ms = jnp.mean(v * v, axis=-1, keepdims=True)
    o_ref[...] = (v * lax.rsqrt(ms + EPS) * og_ref[...]).astype(o_ref.dtype)


def _conformer(p_main, dw_w, dw_b, ln_g, ln_b, pw_w_bf, pw_b, out_g, l, lat_len, ctx_len, n_batch):
    nt = p_main.shape[0]
    tm, h = ROW_TILE, CONF_HALO
    n_lat_tiles = n_batch * lat_len // tm
    cblk = COL_CONF // (2 * CONF_DIM)
    nhb = nt // h
    kern = functools.partial(_conf_kernel, tiles_per_lat_seq=lat_len // tm, n_lat_tiles=n_lat_tiles)
    rows = [_layer_rows(a, l) for a in (dw_b, ln_g, ln_b)]
    rows2 = [_layer_rows(a, l) for a in (pw_b, out_g)]
    dww, dww_spec = _layer_mat(dw_w, l)
    pww, pww_spec = _layer_mat(pw_w_bf, l)
    return pl.pallas_call(
        kern,
        out_shape=jax.ShapeDtypeStruct((nt, CONF_DIM), BF16),
        grid=(nt // tm,),
        in_specs=[pl.BlockSpec((tm, 2 * CONF_DIM), lambda i: (i, cblk)),
                  pl.BlockSpec((h, 2 * CONF_DIM), lambda i: (jnp.maximum(i * (tm // h) - 1, 0), cblk)),
                  pl.BlockSpec((h, 2 * CONF_DIM), lambda i: (jnp.minimum((i + 1) * (tm // h), nhb - 1), cblk)),
                  dww_spec] + [s for _, s in rows] + [pww_spec] + [s for _, s in rows2],
        out_specs=pl.BlockSpec((tm, CONF_DIM), lambda i: (i, 0)),
        scratch_shapes=[pltpu.VMEM((tm + 2 * h, CONF_DIM), F32), pltpu.VMEM((tm, CONF_DIM), F32),
                        pltpu.VMEM((SUBLANES, (CONF_KERNEL - 1) // SUBLANES * SUBLANES + tm, CONF_DIM), F32)],
        compiler_params=_cparams(("arbitrary",)),
        name="conformer",
    )(p_main, p_main, p_main, dww, *[a for a, _ in rows], pww, *[a for a, _ in rows2])


def _gelu_tanh(v):
    return 0.5 * v * (1.0 + jnp.tanh(math.sqrt(2.0 / math.pi) * (v + 0.044715 * (v * v * v))))


def _s5_kernel(*refs, reverse, final, colmajor, seg):
    if final:
        (u_ref, yprev_ref, bblk_ref, cblk_ref, lam_ref, sin_ref, dskip_ref, gw_ref, gb_ref, og_ref,
         o_ref, sout_ref, h_ref, fin_ref, init_ref, carry_ref, *perm_refs) = refs
    else:
        (u_ref, bblk_ref, cblk_ref, lam_ref, sin_ref,
         o_ref, sout_ref, h_ref, fin_ref, init_ref, carry_ref, *perm_refs) = refs
    ns = S5_NSTATE
    nsub = SUBLANES
    win = seg * nsub
    jw = pl.program_id(1)

    @pl.when(jw == 0)
    def _():
        carry_ref[...] = sin_ref[...]

    n_lb = S5_DIM // LANES

    per_row = GRID_W // seg

    def sub_seg(s):
        return s // per_row, slice((s % per_row) * seg, (s % per_row + 1) * seg)

    def permuted(tile_ref, buf_ref):
        for s in range(nsub):
            g, rows = sub_seg(s)
            for k in range(n_lb):
                buf_ref[k, pl.ds(s, seg, stride=nsub), :] = tile_ref[g, rows, k * LANES:(k + 1) * LANES]
        return jnp.concatenate([buf_ref[k] for k in range(n_lb)], axis=1)

    if colmajor:
        u_win = u_ref[...].reshape(win, S5_DIM)
        prev_win = yprev_ref[...].reshape(win, S5_DIM) if final else None
    else:
        u_win = permuted(u_ref, perm_refs[0])
        prev_win = permuted(yprev_ref, perm_refs[1]) if final else None

    u_bf = u_win.astype(BF16)
    ch_per_tile = 2 * LANES // S5_STATE * S5_GROUP
    for j in range(2 * ns // (2 * LANES)):
        c0 = (j * ch_per_tile) % S5_DIM // LANES * LANES
        h_ref[:, j * 2 * LANES:(j + 1) * 2 * LANES] = _dot(
            u_bf[:, c0:c0 + LANES], bblk_ref[c0:c0 + LANES, j * 2 * LANES:(j + 1) * 2 * LANES])

    cw = 512
    n_chunks = ns // cw

    def lam_chunk(row, c):
        return (jnp.broadcast_to(lam_ref[row:row + 1, c * cw:(c + 1) * cw], (nsub, cw)),
                jnp.broadcast_to(lam_ref[row + 1:row + 2, c * cw:(c + 1) * cw], (nsub, cw)))

    def row0(i):
        step = (seg - 1 - i) if reverse else i
        return pl.multiple_of(step * nsub, nsub)

    for c in range(n_chunks):
        lre, lim = lam_chunk(0, c)
        cre = slice(c * cw, (c + 1) * cw)
        cim = slice(ns + c * cw, ns + (c + 1) * cw)

        def step1(i, hc, cre=cre, cim=cim, lre=lre, lim=lim):
            hre, him = hc
            r0 = row0(i)
            nre = lre * hre - lim * him + h_ref[pl.ds(r0, nsub), cre]
            nim = lre * him + lim * hre + h_ref[pl.ds(r0, nsub), cim]
            h_ref[pl.ds(r0, nsub), cre] = nre
            h_ref[pl.ds(r0, nsub), cim] = nim
            return nre, nim

        z0 = jnp.zeros((nsub, cw), F32)
        fre, fim = lax.fori_loop(0, seg, step1, (z0, z0), unroll=4)
        fin_ref[:, cre] = fre
        fin_ref[:, cim] = fim

    seg_row = {32: 2, 64: 4}[seg]
    gre, gim = lam_ref[seg_row:seg_row + 1, :], lam_ref[seg_row + 1:seg_row + 2, :]
    cur_re, cur_im = carry_ref[:, 0:ns], carry_ref[:, ns:2 * ns]
    order = range(nsub - 1, -1, -1) if reverse else range(nsub)
    for s in order:
        init_ref[s:s + 1, 0:ns] = cur_re
        init_ref[s:s + 1, ns:2 * ns] = cur_im
        f_re, f_im = fin_ref[s:s + 1, 0:ns], fin_ref[s:s + 1, ns:2 * ns]
        cur_re, cur_im = gre * cur_re - gim * cur_im + f_re, gre * cur_im + gim * cur_re + f_im
    carry_ref[:, 0:ns] = cur_re
    carry_ref[:, ns:2 * ns] = cur_im

    for c in range(n_chunks):
        lre, lim = lam_chunk(0, c)
        cre = slice(c * cw, (c + 1) * cw)
        cim = slice(ns + c * cw, ns + (c + 1) * cw)

        def step2(i, gc, cre=cre, cim=cim, lre=lre, lim=lim):
            g_re, g_im = gc
            n_re = lre * g_re - lim * g_im
            n_im = lre * g_im + lim * g_re
            r0 = row0(i)
            h_ref[pl.ds(r0, nsub), cre] = h_ref[pl.ds(r0, nsub), cre] + n_re
            h_ref[pl.ds(r0, nsub), cim] = h_ref[pl.ds(r0, nsub), cim] + n_im
            return n_re, n_im

        lax.fori_loop(0, seg, step2, (init_ref[:, cre], init_ref[:, cim]), unroll=4)

    halves = []
    st_per_tile = 2 * LANES // S5_GROUP * S5_STATE
    for n in range(S5_DIM // (2 * LANES)):
        oc = slice(n * 2 * LANES, (n + 1) * 2 * LANES)
        s_re = slice(n * st_per_tile, (n + 1) * st_per_tile)
        s_im = slice(ns + n * st_per_tile, ns + (n + 1) * st_per_tile)
        halves.append(_dot(h_ref[:, s_re].astype(BF16), cblk_ref[s_re, oc])
                      + _dot(h_ref[:, s_im].astype(BF16), cblk_ref[s_im, oc]))
    y_win = jnp.concatenate(halves, axis=1)

    if final:
        tot = prev_win + y_win + dskip_ref[...] * u_win
        gl = _gelu_tanh(tot)
        gate = jax.nn.sigmoid(_dot(gl.astype(BF16), gw_ref[...]) + gb_ref[...])
        v = gl * gate
        ms = jnp.mean(v * v, axis=-1, keepdims=True)
        y_win = v * lax.rsqrt(ms + EPS) * og_ref[...]

    if colmajor:
        o_ref[...] = y_win.reshape(o_ref.shape).astype(o_ref.dtype)
    else:
        y_ref = perm_refs[0]
        for k in range(n_lb):
            y_ref[k] = y_win[:, k * LANES:(k + 1) * LANES]
        for s in range(nsub):
            g, rows = sub_seg(s)
            for k in range(n_lb):
                o_ref[g, rows, k * LANES:(k + 1) * LANES] = (
                    y_ref[k, pl.ds(s, seg, stride=nsub), :].astype(o_ref.dtype))

    @pl.when(jw == pl.num_programs(1) - 1)
    def _():
        sout_ref[...] = carry_ref[...]


def _s5_scan(u3d, ops, ld, state_in, n_batch, lat_len, reverse, colmajor, final_args=None):
    ns2 = 2 * S5_NSTATE
    bblk, cblk, lam = ops
    if colmajor:
        seg = GRID_W
        tile = (seg, SUBLANES, S5_DIM)
        n_win = GRID_W // SUBLANES
        out_shape = (n_batch * seg, GRID_W, S5_DIM)
        imap = (lambda b, j: (b, n_win - 1 - j, 0)) if reverse else (lambda b, j: (b, j, 0))
        in_map = imap
    else:
        seg = S5_SEG
        tile = (S5_WIN // GRID_W, GRID_W, S5_DIM)
        n_win = 1
        out_shape = (n_batch * S5_WIN // GRID_W, GRID_W, S5_DIM)
        ctx_blk0 = n_batch * lat_len // S5_WIN
        imap = lambda b, j: (b, 0, 0)
        in_map = lambda b, j: (ctx_blk0 + b, 0, 0)
    win = seg * SUBLANES
    st = pl.BlockSpec((None, 1, ns2), lambda b, j: (b, 0, 0))
    stacked = lambda a: pl.BlockSpec((None,) + a.shape[1:], lambda b, j: (ld, 0, 0))
    final = final_args is not None
    in_specs = [pl.BlockSpec(tile, in_map)]
    args = [u3d]
    if final:
        in_specs.append(pl.BlockSpec(tile, imap))
        args.append(final_args[0])
    in_specs += [stacked(bblk), stacked(cblk), stacked(lam), st]
    args += [bblk, cblk, lam, state_in]
    scratch = [pltpu.VMEM((win, ns2), F32),
               pltpu.VMEM((SUBLANES, ns2), F32), pltpu.VMEM((SUBLANES, ns2), F32),
               pltpu.VMEM((1, ns2), F32)]
    if not colmajor:
        perm = pltpu.VMEM((S5_DIM // LANES, win, LANES), F32)
        scratch += [perm, perm] if final else [perm]
    if final:
        _, l, d_skip, glu_w_bf, glu_b, out_g = final_args
        for a in (d_skip,):
            a3, sp = _layer_rows(a, l)
            in_specs.append(sp)
            args.append(a3)
        gw, gw_spec = _layer_mat(glu_w_bf, l)
        in_specs.append(gw_spec)
        args.append(gw)
        for a in (glu_b, out_g):
            a3, sp = _layer_rows(a, l)
            in_specs.append(sp)
            args.append(a3)
    out, s_out = pl.pallas_call(
        functools.partial(_s5_kernel, reverse=reverse, final=final, colmajor=colmajor, seg=seg),
        out_shape=(jax.ShapeDtypeStruct(out_shape, F32),
                   jax.ShapeDtypeStruct((n_batch, 1, ns2), F32)),
        grid=(n_batch, n_win),
        in_specs=in_specs,
        out_specs=(pl.BlockSpec(tile, imap), st),
        scratch_shapes=scratch,
        compiler_params=_cparams(("arbitrary", "arbitrary")),
        name="s5_" + ("rev" if reverse else "fwd") + ("_lat" if colmajor else "_ctx"),
    )(*args)
    return out, s_out


def _s5_operands(lam_re, lam_im, log_step, b_re, b_im, c_re, c_im):
    g, p, k = S5_GROUPS, S5_STATE, S5_GROUP
    lam = lax.complex(jnp.minimum(lam_re.astype(F32), -1e-4), lam_im.astype(F32))
    step = jnp.exp(log_step.astype(F32))[:, None]
    lam_bar = jnp.exp(lam * step)
    lam_seg = jnp.exp(lam * (step * S5_SEG))
    lam_col = jnp.exp(lam * (step * GRID_W))
    b_bar = ((lam_bar - 1.0) / lam)[..., None] * lax.complex(b_re.astype(F32), b_im.astype(F32))
    def block_diag(m):
        a, b = m.shape[1], m.shape[2]
        tiled = jnp.tile(m.reshape(g * a, b), (1, g))
        own = (lax.broadcasted_iota(I32, (g * a, g * b), 0) // a
               == lax.broadcasted_iota(I32, (g * a, g * b), 1) // b)
        return jnp.where(own, tiled, 0.0)

    bd_in = lambda m: block_diag(jnp.transpose(m, (0, 2, 1)))
    bblk = jnp.concatenate([bd_in(jnp.real(b_bar)), bd_in(jnp.imag(b_bar))], axis=1)
    bd_out = lambda m: block_diag(jnp.transpose(m, (0, 2, 1)))
    cblk = jnp.concatenate([bd_out(c_re.astype(F32)), -bd_out(c_im.astype(F32))], axis=0)
    zeros = jnp.zeros((g * p,), F32)
    lam_rows = jnp.stack([jnp.real(lam_bar).reshape(-1), jnp.imag(lam_bar).reshape(-1),
                          jnp.real(lam_seg).reshape(-1), jnp.imag(lam_seg).reshape(-1),
                          jnp.real(lam_col).reshape(-1), jnp.imag(lam_col).reshape(-1), zeros, zeros])
    return bblk.astype(BF16), cblk.astype(BF16), lam_rows


def _pack_bf16_pairs(v):
    n = v.shape[1] // 2
    bits = pltpu.bitcast(v.astype(BF16).astype(F32), U32)
    return (bits[:, :n] >> 16) | (bits[:, n:] & jnp.uint32(0xFFFF0000))


def _unpack_bf16_pairs(w):
    lo = pltpu.bitcast(w << 16, F32)
    hi = pltpu.bitcast(w & jnp.uint32(0xFFFF0000), F32)
    return lo, hi


def _store_token_tiles(ref, v):
    rows = v.shape[0]
    for c in range(SUBLANES):
        ref[pl.ds(c, rows, stride=SUBLANES), :] = v[:, c * LANES:(c + 1) * LANES]


def _load_token_tiles(ref, rows):
    return [ref[pl.ds(c, rows, stride=SUBLANES), :] for c in range(SUBLANES)]


def _mixout_kernel(r_ref, a_ref, b_ref, cl_ref, cc_ref, wa_ref, wb_ref, wc_ref, g1_ref, ng_ref, sh_ref,
                   sc_ref, rw_ref, rb_ref, r1_ref, hp_ref, idx_ref, gate_ref, rank_ref, cnt_ref,
                   carry_ref, *, n_lat_tiles):
    i = pl.program_id(0)

    @pl.when(i == 0)
    def _():
        carry_ref[...] = jnp.zeros_like(carry_ref)

    mix_c = jnp.where(i < n_lat_tiles, cl_ref[...], cc_ref[...]).reshape(a_ref.shape[0], S5_DIM).astype(BF16)
    acc = _dot(a_ref[...], wa_ref[...]) + _dot(b_ref[...], wb_ref[...]) + _dot(mix_c, wc_ref[...])
    x = r_ref[...] + g1_ref[...] * acc
    r1_ref[...] = x
    ms = jnp.mean(x * x, axis=-1, keepdims=True)
    h = x * lax.rsqrt(ms + EPS) * ng_ref[...]
    h = h * (1.0 + sc_ref[...]) + sh_ref[...]
    _store_token_tiles(hp_ref, _pack_bf16_pairs(h))

    h_hi = h.astype(BF16)
    h_lo = (h - h_hi.astype(F32)).astype(BF16)
    rw = rw_ref[...]
    w_hi = rw.astype(BF16)
    w_lo = (rw - w_hi.astype(F32)).astype(BF16)
    logits = _dot(h_hi, w_hi) + _dot(h_lo, w_hi) + _dot(h_hi, w_lo) + rb_ref[...]

    tm = logits.shape[0]
    lane = lax.broadcasted_iota(I32, (tm, LANES), 1)
    lane_f = lane.astype(F32)
    work = logits
    tops, picks = [], []
    for _ in range(TOP_K):
        m = jnp.max(work, axis=-1, keepdims=True)
        pick = jnp.min(jnp.where(work == m, lane_f, float(LANES)), axis=-1, keepdims=True)
        work = jnp.where(lane_f == pick, -jnp.inf, work)
        tops.append(m)
        picks.append(pick)
    exps = [jnp.exp(t - tops[0]) for t in tops]
    denom = exps[0] + exps[1] + exps[2] + exps[3]

    onehot = jnp.zeros((tm, LANES), F32)
    for k in range(TOP_K):
        onehot = onehot + jnp.where(lane_f == picks[k], 1.0, 0.0)
    ri = lax.broadcasted_iota(I32, (tm, tm), 0)
    ci = lax.broadcasted_iota(I32, (tm, tm), 1)
    before = jnp.where(ci < ri, 1.0, 0.0).astype(BF16)
    base = carry_ref[0:1, :] + _dot(before, onehot.astype(BF16))
    carry_ref[...] = carry_ref[...] + jnp.sum(onehot, axis=0, keepdims=True)
    cnt_ref[...] = carry_ref[...]

    idx_out = jnp.zeros((tm, LANES), F32)
    gate_out = jnp.zeros((tm, LANES), F32)
    rank_out = jnp.zeros((tm, LANES), F32)
    for k in range(TOP_K):
        rank_k = jnp.sum(jnp.where(lane_f == picks[k], base, 0.0), axis=-1, keepdims=True)
        idx_out = jnp.where(lane == k, picks[k], idx_out)
        gate_out = jnp.where(lane == k, exps[k] / denom, gate_out)
        rank_out = jnp.where(lane == k, rank_k, rank_out)
    gate_ref[...] = gate_out
    idx_t = idx_out.T.astype(I32)
    rank_t = rank_out.T.astype(I32)
    for k in range(TOP_K):
        for hh in range(tm // LANES):
            row = k * (tm // LANES) + hh
            idx_ref[row:row + 1, :] = idx_t[k:k + 1, hh * LANES:(hh + 1) * LANES]
            rank_ref[row:row + 1, :] = rank_t[k:k + 1, hh * LANES:(hh + 1) * LANES]


def _mix_out(r, mix_a, mix_b, mix_c_lat, mix_c_ctx, w_out_bf, mods4, norm2_g, router_w_pad, router_b_pad,
             l, lat_len, n_batch):
    nt, d = r.shape
    tm = ROW_TILE
    n_lat_tiles = n_batch * lat_len // tm

    def grp(i):
        return jnp.minimum((i * tm) // lat_len, n_batch)

    ng3, ng_spec = _layer_rows(norm2_g, l)
    rb3, rb_spec = _layer_rows(router_b_pad, l)
    tile = lambda w: pl.BlockSpec((tm, w), lambda i: (i, 0))
    dense_rows = tm * TOP_K // LANES
    dense = pl.BlockSpec((dense_rows, LANES), lambda i: (i, 0))
    n_dense = nt * TOP_K // LANES
    return pl.pallas_call(
        functools.partial(_mixout_kernel, n_lat_tiles=n_lat_tiles),
        out_shape=(jax.ShapeDtypeStruct((nt, d), F32), jax.ShapeDtypeStruct((nt * SUBLANES, LANES), U32),
                   jax.ShapeDtypeStruct((n_dense, LANES), I32), jax.ShapeDtypeStruct((nt, LANES), F32),
                   jax.ShapeDtypeStruct((n_dense, LANES), I32), jax.ShapeDtypeStruct((SUBLANES, LANES), F32)),
        grid=(nt // tm,),
        in_specs=[tile(d), tile(SSD_INNER), tile(CONF_DIM),
                  pl.BlockSpec((tm // GRID_W, GRID_W, S5_DIM), lambda i: (jnp.minimum(i, n_lat_tiles - 1), 0, 0)),
                  pl.BlockSpec((tm // GRID_W, GRID_W, S5_DIM), lambda i: (jnp.maximum(i - n_lat_tiles, 0), 0, 0)),
                  pl.BlockSpec((None, SSD_INNER, d), lambda i: (l, 0, 0)),
                  pl.BlockSpec((None, CONF_DIM, d), lambda i: (l, SSD_INNER // CONF_DIM, 0)),
                  pl.BlockSpec((None, S5_DIM, d), lambda i: (l, (SSD_INNER + CONF_DIM) // S5_DIM, 0)),
                  _mod_spec(l, grp, 2, d), ng_spec, _mod_spec(l, grp, 3, d), _mod_spec(l, grp, 4, d),
                  pl.BlockSpec((None, d, LANES), lambda i: (l, 0, 0)), rb_spec],
        out_specs=(tile(d), pl.BlockSpec((tm * SUBLANES, LANES), lambda i: (i, 0)), dense, tile(LANES), dense,
                   pl.BlockSpec((SUBLANES, LANES), lambda i: (0, 0))),
        scratch_shapes=[pltpu.VMEM((SUBLANES, LANES), F32)],
        compiler_params=_cparams(("arbitrary",)),
        name="mix_out_router",
    )(r, mix_a, mix_b, mix_c_lat, mix_c_ctx, w_out_bf, w_out_bf, w_out_bf, mods4, ng3, mods4, mods4,
      router_w_pad, rb3)


def _expert_kernel(be_ref, nv_ref, first_ref, nxt_ref, g_cur_ref, g_nxt_ref, s_cur_ref, s_prv_ref, h_hbm,
                   wgu_hbm, bgu_ref, wd_hbm, bd_ref, ya_hbm, xbuf, ybuf, wgu_st, wd_st, wgu_bf, wd_bf, zbuf,
                   gu_ref, gsem, ssem, wsem, zsem, *, layer):
    i = pl.program_id(0)
    n_steps = pl.num_programs(0)
    slot = i % 2
    nv = nv_ref[i]
    blk = MOE_BLOCK
    tile_rows = blk * SUBLANES

    def weight_copies(e):
        return (pltpu.make_async_copy(wgu_hbm.at[layer, e], wgu_st, wsem.at[0]),
                pltpu.make_async_copy(wd_hbm.at[layer, e], wd_st, wsem.at[1]))

    def token_tile(ref, row0):
        return ref.at[pl.ds(pl.multiple_of(row0, SUBLANES), SUBLANES)]

    def used(j):
        return jnp.logical_and(jnp.logical_and(j >= 0, j < n_steps),
                               nv_ref[jnp.clip(j, 0, n_steps - 1)] > 0)

    row_priority, weight_priority = 0, 1

    def gather_rows(idx_ref, dst_slot, lo_row, n):
        def body(rr, carry):
            pltpu.make_async_copy(token_tile(h_hbm, idx_ref[0, rr]),
                                  token_tile(xbuf.at[dst_slot], rr * SUBLANES),
                                  gsem.at[dst_slot]).start(priority=row_priority)
            return carry
        lax.fori_loop(lo_row, lo_row + n, body, 0, unroll=16)

    def scatter_rows(idx_ref, src_slot, lo_row, n):
        def body(rr, carry):
            pltpu.make_async_copy(token_tile(ybuf.at[src_slot], rr * SUBLANES),
                                  token_tile(ya_hbm, idx_ref[0, rr]),
                                  ssem.at[src_slot]).start(priority=row_priority)
            return carry
        lax.fori_loop(lo_row, lo_row + n, body, 0, unroll=16)

    def scatter_wait(src_slot):
        pltpu.make_async_copy(ybuf.at[src_slot], ya_hbm.at[pl.ds(0, tile_rows)], ssem.at[src_slot]).wait()

    def gather_wait(dst_slot):
        pltpu.make_async_copy(h_hbm.at[pl.ds(0, tile_rows)], xbuf.at[dst_slot], gsem.at[dst_slot]).wait()

    @pl.when(i == 0)
    def _():
        zbuf[...] = jnp.zeros_like(zbuf)
        for cp in weight_copies(be_ref[0]):
            cp.start(priority=weight_priority)
        gather_rows(g_cur_ref, 0, 0, blk)

    def compute_block(with_scatter):
        @pl.when(first_ref[i] == 1)
        def _():
            for cp in weight_copies(be_ref[i]):
                cp.wait()
            wgu_bf[...] = wgu_st[...].astype(BF16)
            wd_bf[...] = wd_st[...].astype(BF16)

            @pl.when(nxt_ref[i] >= 0)
            def _():
                for cp in weight_copies(nxt_ref[i]):
                    cp.start(priority=weight_priority)

        gather_wait(slot)
        half = D_MODEL // 2
        xw = jnp.concatenate(_load_token_tiles(xbuf.at[slot], blk), axis=1)
        lo, hi = _unpack_bf16_pairs(xw)
        lo, hi = lo.astype(BF16), hi.astype(BF16)
        n_col = 2 * D_EXPERT // (2 * LANES)
        bounds = [blk * c // n_col for c in range(n_col + 1)]
        for c in range(n_col):
            cols = slice(c * 2 * LANES, (c + 1) * 2 * LANES)
            gu_ref[:, cols] = (_dot(lo, wgu_bf[0:half, cols]) + _dot(hi, wgu_bf[half:D_MODEL, cols])
                               + bgu_ref[:, cols])
            for rr in range(bounds[c], bounds[c + 1]):
                pltpu.make_async_copy(token_tile(h_hbm, g_nxt_ref[0, rr]),
                                      xbuf.at[1 - slot, pl.ds(rr * SUBLANES, SUBLANES)],
                                      gsem.at[1 - slot]).start(priority=row_priority)
                if with_scatter:
                    pltpu.make_async_copy(ybuf.at[1 - slot, pl.ds(rr * SUBLANES, SUBLANES)],
                                          token_tile(ya_hbm, s_prv_ref[0, rr]),
                                          ssem.at[1 - slot]).start(priority=row_priority)
        gu = gu_ref[...]
        gate = jnp.minimum(gu[:, :D_EXPERT], SWIGLU_LIMIT)
        lin = jnp.clip(gu[:, D_EXPERT:], -SWIGLU_LIMIT, SWIGLU_LIMIT)
        act = (gate * jax.nn.sigmoid(SWIGLU_ALPHA * gate) * (lin + 1.0)).astype(BF16)
        y = _dot(act, wd_bf[...]) + bd_ref[...]

        @pl.when(used(i - 2))
        def _():
            scatter_wait(slot)

        _store_token_tiles(ybuf.at[slot], _pack_bf16_pairs(y))

    @pl.when(jnp.logical_and(nv > 0, i == 0))
    def _():
        compute_block(False)

    @pl.when(jnp.logical_and(nv > 0, i > 0))
    def _():
        compute_block(True)

    @pl.when(nv == 0)
    def _():
        gather_wait(slot)
        gather_rows(g_nxt_ref, 1 - slot, 0, blk)

        @pl.when(used(i - 2))
        def _():
            scatter_wait(slot)

        @pl.when(used(i - 1))
        def _():
            scatter_rows(s_prv_ref, 1 - slot, 0, blk)

        own = ya_hbm.at[pl.ds(pl.multiple_of(i * tile_rows, tile_rows), tile_rows)]
        zero_copy = pltpu.make_async_copy(zbuf, own, zsem)
        zero_copy.start()
        zero_copy.wait()

    @pl.when(i == n_steps - 1)
    def _():
        gather_wait(1 - slot)

        @pl.when(nv > 0)
        def _():
            scatter_rows(s_cur_ref, slot, 0, blk)
            scatter_wait(slot)

        @pl.when(used(i - 1))
        def _():
            scatter_wait(1 - slot)


def _experts(h_tiles, gather_row, scatter_row, block_e, n_valid, first, nxt, w_gu, b_gu, w_down, b_down,
             layer):
    n_rows = gather_row.shape[0]
    n_blocks = n_rows // MOE_BLOCK
    depth, ne = w_gu.shape[:2]
    g3 = gather_row.reshape(n_blocks, 1, MOE_BLOCK)
    s3 = scatter_row.reshape(n_blocks, 1, MOE_BLOCK)
    width = D_MODEL // 2
    tile_rows = MOE_BLOCK * SUBLANES
    idx_block = lambda imap: pl.BlockSpec((None, 1, MOE_BLOCK), imap, memory_space=pltpu.SMEM)
    grid_spec = pltpu.PrefetchScalarGridSpec(
        num_scalar_prefetch=4,
        grid=(n_blocks,),
        in_specs=[idx_block(lambda i, *_: (i, 0, 0)),
                  idx_block(lambda i, *_: (jnp.minimum(i + 1, n_blocks - 1), 0, 0)),
                  idx_block(lambda i, *_: (i, 0, 0)),
                  idx_block(lambda i, *_: (jnp.maximum(i - 1, 0), 0, 0)),
                  pl.BlockSpec(memory_space=pl.ANY),
                  pl.BlockSpec(memory_space=pl.ANY),
                  pl.BlockSpec((None, None, 1, 2 * D_EXPERT), lambda i, be, *_: (layer, be[i], 0, 0)),
                  pl.BlockSpec(memory_space=pl.ANY),
                  pl.BlockSpec((None, None, 1, D_MODEL), lambda i, be, *_: (layer, be[i], 0, 0))],
        out_specs=pl.BlockSpec(memory_space=pl.ANY),
        scratch_shapes=[pltpu.VMEM((2, tile_rows, LANES), U32), pltpu.VMEM((2, tile_rows, LANES), U32),
                        pltpu.VMEM((D_MODEL, 2 * D_EXPERT), F32), pltpu.VMEM((D_EXPERT, D_MODEL), F32),
                        pltpu.VMEM((D_MODEL, 2 * D_EXPERT), BF16), pltpu.VMEM((D_EXPERT, D_MODEL), BF16),
                        pltpu.VMEM((tile_rows, LANES), U32), pltpu.VMEM((MOE_BLOCK, 2 * D_EXPERT), F32),
                        pltpu.SemaphoreType.DMA((2,)), pltpu.SemaphoreType.DMA((2,)),
                        pltpu.SemaphoreType.DMA((2,)), pltpu.SemaphoreType.DMA],
    )
    return pl.pallas_call(
        functools.partial(_expert_kernel, layer=layer),
        out_shape=jax.ShapeDtypeStruct((n_rows * SUBLANES, LANES), U32),
        grid_spec=grid_spec,
        compiler_params=_cparams(("arbitrary",)),
        name="moe_experts",
    )(block_e, n_valid, first, nxt, g3, g3, s3, s3, h_tiles, w_gu, b_gu.reshape(depth, ne, 1, -1), w_down,
      b_down.reshape(depth, ne, 1, -1))


def _combine_kernel(r_ref, gate_ref, y0_ref, y1_ref, y2_ref, y3_ref, g2_ref, fg_ref, o_ref, *, last_layer):
    tm = r_ref.shape[0]
    half = D_MODEL // 2
    gates = [gate_ref[:, k:k + 1] for k in range(TOP_K)]
    y_refs = (y0_ref, y1_ref, y2_ref, y3_ref)
    for c in range(SUBLANES):
        acc_lo = jnp.zeros((tm, LANES), F32)
        acc_hi = jnp.zeros((tm, LANES), F32)
        for k in range(TOP_K):
            lo, hi = _unpack_bf16_pairs(y_refs[k][pl.ds(c, tm, stride=SUBLANES), :])
            acc_lo = acc_lo + gates[k] * lo
            acc_hi = acc_hi + gates[k] * hi
        lo_cols = slice(c * LANES, (c + 1) * LANES)
        hi_cols = slice(half + c * LANES, half + (c + 1) * LANES)
        o_ref[:, lo_cols] = r_ref[:, lo_cols] + g2_ref[:, lo_cols] * acc_lo
        o_ref[:, hi_cols] = r_ref[:, hi_cols] + g2_ref[:, hi_cols] * acc_hi
    if last_layer:
        x = o_ref[...]
        ms = jnp.mean(x * x, axis=-1, keepdims=True)
        o_ref[...] = x * lax.rsqrt(ms + EPS) * fg_ref[...]


def _combine(r1, gates, ya, mods4, final_g, l, lat_len, n_batch, last_layer):
    nt, d = r1.shape
    tm = ROW_TILE
    n_out = n_batch * lat_len if last_layer else nt
    tiles_per_k = nt // tm

    def grp(i):
        return jnp.minimum((i * tm) // lat_len, n_batch)

    def choice(k):
        return pl.BlockSpec((tm * SUBLANES, LANES), lambda i: (k * tiles_per_k + i, 0))

    return pl.pallas_call(
        functools.partial(_combine_kernel, last_layer=last_layer),
        out_shape=jax.ShapeDtypeStruct((n_out, d), F32),
        grid=(n_out // tm,),
        in_specs=[pl.BlockSpec((tm, d), lambda i: (i, 0)),
                  pl.BlockSpec((tm, LANES), lambda i: (i, 0)),
                  choice(0), choice(1), choice(2), choice(3),
                  _mod_spec(l, grp, 5, d),
                  pl.BlockSpec((1, d), lambda i: (0, 0))],
        out_specs=pl.BlockSpec((tm, d), lambda i: (i, 0)),
        compiler_params=_cparams(("arbitrary",)),
        name="moe_combine",
    )(r1, gates, ya, ya, ya, ya, mods4, final_g.reshape(1, d))


def _routing_plan(top_idx, rank, counts, n_blocks):
    n_assign = top_idx.size
    n_tok = n_assign // TOP_K
    n_rows = n_blocks * MOE_BLOCK
    counts = counts.astype(I32)
    padded = (counts + MOE_BLOCK - 1) // MOE_BLOCK * MOE_BLOCK
    pad_end = jnp.cumsum(padded)
    pad_start = pad_end - padded
    count_end = jnp.cumsum(counts)
    per_tile = ROW_TILE * TOP_K // LANES
    q = lax.broadcasted_iota(I32, top_idx.shape, 0)
    lane = lax.broadcasted_iota(I32, top_idx.shape, 1)
    halves = ROW_TILE // LANES
    token = (q // per_tile) * ROW_TILE + (q % halves) * LANES + lane
    choice = (q % per_tile) // halves
    out_tile = choice * n_tok + token
    dest = pad_start[top_idx] + rank
    blk_start = jnp.arange(n_blocks, dtype=I32) * MOE_BLOCK

    def expert_at(pos):
        return jnp.minimum(jnp.sum((pad_end[None, :] <= pos[:, None]).astype(I32), axis=1), N_EXPERTS - 1)

    block_e = expert_at(blk_start)
    used = blk_start < pad_end[-1]
    n_valid = jnp.where(used, jnp.clip(pad_start[block_e] + counts[block_e] - blk_start, 0, MOE_BLOCK), 0)
    first = jnp.logical_and(used, blk_start == pad_start[block_e]).astype(I32)
    nxt_start = pad_end[block_e]
    nxt = jnp.where(nxt_start < pad_end[-1], expert_at(nxt_start), -1).astype(I32)

    row = jnp.arange(n_rows, dtype=I32)
    dump = n_assign + row - jnp.repeat(count_end[block_e], MOE_BLOCK)
    row_tile = dump.at[dest].set(out_tile, unique_indices=True, mode='drop')
    gather_row = (row_tile % n_tok) * SUBLANES
    scatter_row = row_tile * SUBLANES
    return gather_row.astype(I32), scatter_row.astype(I32), block_e.astype(I32), n_valid.astype(I32), first, nxt


def _forward(x, c, ctx, c_ctx, ada_w, ada_b, norm1_g, w_in, ssd_conv_w, ssd_conv_b, ssd_a_log,
             ssd_dt_bias, ssd_d, ssd_norm_g, conf_dw_w, conf_dw_b, conf_ln_g, conf_ln_b, conf_pw_w,
             conf_pw_b, conf_out_g, s5_lam_re, s5_lam_im, s5_log_step, s5_b_re, s5_b_im, s5_c_re,
             s5_c_im, s5_d, s5_glu_w, s5_glu_b, s5_out_g, w_out, norm2_g, router_w, router_b,
             w_gate_up, b_gate_up, w_down, b_down, final_norm_g):
    n_batch, lat_len, d = x.shape
    ctx_len = ctx.shape[1]
    depth = ada_w.shape[0]
    n_lat = n_batch * lat_len
    nt = n_lat + n_batch * ctx_len
    assert d == D_MODEL and ctx_len == ROW_TILE and lat_len % MM_TILE_M == 0
    assert lat_len // GRID_W == GRID_W

    r = jnp.concatenate([x.reshape(n_lat, d), ctx.reshape(n_batch * ctx_len, d)], axis=0).astype(F32)
    cond = jnp.zeros((SUBLANES, d), F32).at[:n_batch].set(c).at[n_batch].set(c_ctx)
    mods4 = _modulation(cond, ada_w, ada_b).reshape(depth, SUBLANES, 1, 6 * d)

    c_dt = SSD_INNER + SSD_CONV_DIM
    c_conf = c_dt + SSD_HEADS
    c_s5 = c_conf + 2 * CONF_DIM
    w_proj = jnp.concatenate(
        [w_in[:, :, :c_dt], w_in[:, :, c_conf:c_s5], w_in[:, :, c_s5:], w_in[:, :, c_dt:c_conf],
         jnp.zeros((depth, d, MM_TILE_N - S5_DIM - SSD_HEADS), w_in.dtype)], axis=2).astype(BF16)
    w_out_bf = w_out.astype(BF16)
    conf_pw_bf = conf_pw_w.astype(BF16)
    s5_glu_bf = s5_glu_w.astype(BF16)
    head_pad = lambda a: jnp.pad(a.astype(F32), ((0, 0), (0, 0), (0, LANES - SSD_HEADS))).reshape(
        depth * 2, 1, LANES)
    dtb_all, alog_all = head_pad(ssd_dt_bias), head_pad(ssd_a_log)
    d_skip_all = jnp.repeat(ssd_d.astype(F32), SSD_HEAD_DIM, axis=1)
    router_w_pad = jnp.pad(router_w.astype(F32), ((0, 0), (0, 0), (0, LANES - N_EXPERTS)))
    router_b_pad = jnp.pad(router_b.astype(F32), ((0, 0), (0, LANES - N_EXPERTS)), constant_values=-1e30)
    flat2 = lambda a: a.reshape((depth * 2,) + a.shape[2:])
    s5_ops = jax.vmap(_s5_operands)(*[flat2(a) for a in (s5_lam_re, s5_lam_im, s5_log_step, s5_b_re,
                                                         s5_b_im, s5_c_re, s5_c_im)])

    n_blocks = nt * TOP_K // MOE_BLOCK + N_EXPERTS
    s5_zero = jnp.zeros((n_batch, 1, 2 * S5_NSTATE), F32)

    for l in range(depth):
        p_main, p_s5, p_dt = _in_proj(r, norm1_g, mods4, l, w_proj, lat_len, n_batch)

        xbc = _ssd_conv(p_main, ssd_conv_w, ssd_conv_b, l, lat_len, ctx_len, n_batch)
        y_fwd = _ssd_scan(xbc, p_main, p_dt, dtb_all, alog_all, l, lat_len, ctx_len, n_batch, False)
        mix_a = _ssd_scan(xbc, p_main, p_dt, dtb_all, alog_all, l, lat_len, ctx_len, n_batch, True,
                          (y_fwd, d_skip_all, ssd_norm_g))

        mix_b = _conformer(p_main, conf_dw_w, conf_dw_b, conf_ln_g, conf_ln_b, conf_pw_bf, conf_pw_b,
                           conf_out_g, l, lat_len, ctx_len, n_batch)

        yc_f, st_f = _s5_scan(p_s5, s5_ops, 2 * l, s5_zero, n_batch, lat_len, False, False)
        yl_f, _ = _s5_scan(p_s5, s5_ops, 2 * l, st_f, n_batch, lat_len, False, True)
        fin = (l, s5_d, s5_glu_bf, s5_glu_b, s5_out_g)
        mc_c, st_r = _s5_scan(p_s5, s5_ops, 2 * l + 1, s5_zero, n_batch, lat_len, True, False, (yc_f,) + fin)
        mc_l, _ = _s5_scan(p_s5, s5_ops, 2 * l + 1, st_r, n_batch, lat_len, True, True, (yl_f,) + fin)

        r1, h_tiles, top_idx, gates, rank, counts = _mix_out(
            r, mix_a, mix_b, mc_l, mc_c, w_out_bf, mods4, norm2_g, router_w_pad,
            router_b_pad, l, lat_len, n_batch)

        gather_row, scatter_row, block_e, n_valid, first, nxt = _routing_plan(
            top_idx, rank, counts[0, :N_EXPERTS], n_blocks)
        ya = _experts(h_tiles, gather_row, scatter_row, block_e, n_valid, first, nxt, w_gate_up, b_gate_up,
                      w_down, b_down, l)
        r = _combine(r1, gates, ya, mods4, final_norm_g, l, lat_len, n_batch, l == depth - 1)

    return r.reshape(n_batch, lat_len, d).astype(x.dtype)


def kernel(x, c, ctx, c_ctx, ada_w, ada_b, norm1_g, w_in, ssd_conv_w, ssd_conv_b, ssd_a_log, ssd_dt_bias,
           ssd_d, ssd_norm_g, conf_dw_w, conf_dw_b, conf_ln_g, conf_ln_b, conf_pw_w, conf_pw_b, conf_out_g,
           s5_lam_re, s5_lam_im, s5_log_step, s5_b_re, s5_b_im, s5_c_re, s5_c_im, s5_d, s5_glu_w, s5_glu_b,
           s5_out_g, w_out, norm2_g, router_w, router_b, w_gate_up, b_gate_up, w_down, b_down, final_norm_g):
    return _forward(x, c, ctx, c_ctx, ada_w, ada_b, norm1_g, w_in, ssd_conv_w, ssd_conv_b, ssd_a_log,
                    ssd_dt_bias, ssd_d, ssd_norm_g, conf_dw_w, conf_dw_b, conf_ln_g, conf_ln_b, conf_pw_w,
                    conf_pw_b, conf_out_g, s5_lam_re, s5_lam_im, s5_log_step, s5_b_re, s5_b_im, s5_c_re,
                    s5_c_im, s5_d, s5_glu_w, s5_glu_b, s5_out_g, w_out, norm2_g, router_w, router_b,
                    w_gate_up, b_gate_up, w_down, b_down, final_norm_g)
```

```python
import functools
import math

import jax
import jax.numpy as jnp
from jax import lax
from jax.experimental import pallas as pl
from jax.experimental.pallas import tpu as pltpu

F32 = jnp.float32
BF16 = jnp.bfloat16
I32 = jnp.int32
U32 = jnp.uint32

D_MODEL = 2048
GRID_W = 64
SSD_INNER = 1024
SSD_HEAD_DIM = 64
SSD_HEADS = 16
SSD_GROUPS = 4
SSD_STATE = 128
SSD_CONV = 5
SSD_CHUNK = 128
SSD_CONV_DIM = SSD_INNER + 2 * SSD_GROUPS * SSD_STATE
CONF_DIM = 512
CONF_KERNEL = 31
S5_DIM = 512
S5_GROUP = 16
S5_GROUPS = 32
S5_STATE = 64
S5_NSTATE = S5_GROUPS * S5_STATE
N_EXPERTS = 32
TOP_K = 4
D_EXPERT = 768
SWIGLU_LIMIT = 7.0
SWIGLU_ALPHA = 1.702
MOE_BLOCK = 256
EPS = 1e-6

LANES = 128
SUBLANES = 8
ROW_TILE = 256
MM_TILE_M = 512
MM_TILE_N = 1024
MAIN_COLS = 4096
COL_Z, COL_X, COL_CONF = 0, 1024, 3072
SIDE_COLS = S5_DIM + LANES
CONV_HALO = 16
CONF_HALO = 16
S5_SEG = 32
S5_WIN = SUBLANES * S5_SEG
VMEM_LIMIT = 56 * 1024 * 1024


def _cparams(sem, vmem=VMEM_LIMIT):
    return pltpu.CompilerParams(dimension_semantics=sem, vmem_limit_bytes=vmem)


def _silu(v):
    return v * jax.nn.sigmoid(v)


def _split3(v):
    hi = v.astype(BF16)
    r1 = v - hi.astype(F32)
    mid = r1.astype(BF16)
    lo = (r1 - mid.astype(F32)).astype(BF16)
    return hi, mid, lo


def _dot(a, b):
    return jnp.dot(a, b, preferred_element_type=F32)


def _layer_rows(arr, l):
    depth = arr.shape[0]
    a3 = arr.reshape(depth, 1, -1)
    return a3, pl.BlockSpec((None, 1, a3.shape[2]), lambda *_: (l, 0, 0))


def _layer_mat(arr, l):
    return arr, pl.BlockSpec((None,) + arr.shape[1:], lambda *_: (l, 0, 0))


def _mod_kernel(c_ref, w_ref, b_ref, o_ref):
    s = _silu(c_ref[...])
    w = w_ref[...]
    s_hi = s.astype(BF16)
    s_lo = (s - s_hi.astype(F32)).astype(BF16)
    w_hi = w.astype(BF16)
    w_lo = (w - w_hi.astype(F32)).astype(BF16)
    acc = _dot(s_hi, w_hi) + _dot(s_lo, w_hi) + _dot(s_hi, w_lo)
    o_ref[...] = acc + b_ref[...]


def _modulation(cond, ada_w, ada_b):
    depth, d, n = ada_w.shape
    tn = 1024
    return pl.pallas_call(
        _mod_kernel,
        out_shape=jax.ShapeDtypeStruct((depth, SUBLANES, n), F32),
        grid=(depth, n // tn),
        in_specs=[pl.BlockSpec((SUBLANES, d), lambda l, j: (0, 0)),
                  pl.BlockSpec((None, d, tn), lambda l, j: (l, 0, j)),
                  pl.BlockSpec((None, 1, tn), lambda l, j: (l, 0, j))],
        out_specs=pl.BlockSpec((None, SUBLANES, tn), lambda l, j: (l, 0, j)),
        compiler_params=_cparams(("arbitrary", "arbitrary")),
        name="adaln_mod",
    )(cond, ada_w, ada_b.reshape(depth, 1, n))


def _mod_spec(l, grp, k, d):
    return pl.BlockSpec((None, None, 1, d), lambda i, *_: (l, grp(i), 0, k))


def _inproj_kernel(x_ref, g_ref, sh_ref, sc_ref, w_ref, main_ref, s5_ref, dt_ref, hn_ref, *, n_tiles,
                   n_main):
    j = pl.program_id(1)
    s = pl.program_id(2)
    live = 2 * pl.program_id(0) + s < n_tiles

    @pl.when(jnp.logical_and(j == 0, live))
    def _():
        x = x_ref[...]
        ms = jnp.mean(x * x, axis=-1, keepdims=True)
        y = x * lax.rsqrt(ms + EPS) * g_ref[...]
        hn_ref[s] = (y * (1.0 + sc_ref[...]) + sh_ref[...]).astype(BF16)

    @pl.when(jnp.logical_and(live, j < n_main))
    def _():
        main_ref[...] = _dot(hn_ref[s], w_ref[...]).astype(main_ref.dtype)

    @pl.when(jnp.logical_and(live, j == n_main))
    def _():
        res = _dot(hn_ref[s], w_ref[...])
        s5_ref[...] = res[:, 0:S5_DIM].reshape(s5_ref.shape)
        dt_ref[...] = res[:, S5_DIM:S5_DIM + LANES]


def _in_proj(r, gains, mods4, l, w_all, lat_len, n_batch):
    nt, d = r.shape
    tm = MM_TILE_M if nt % MM_TILE_M == 0 else ROW_TILE
    tn = MM_TILE_N
    n_main = MAIN_COLS // tn
    assert w_all.shape[2] == (n_main + 1) * tn and SIDE_COLS <= tn
    n_tiles = nt // tm
    last = n_tiles - 1

    def tile(p, s):
        return jnp.minimum(2 * p + s, last)

    def pair_end(p):
        return jnp.minimum(2 * p + 1, last)

    def x_tile(p, j, s):
        return jnp.where(j == 0, tile(p, s), pair_end(p))

    def main_tile(p, j, s):
        return jnp.where(j < n_main, tile(p, s), pair_end(p))

    def side_tile(p, j, s):
        return jnp.where(j == n_main, tile(p, s), jnp.maximum(2 * p - 1, 0))

    def grp(t):
        return jnp.minimum((t * tm) // lat_len, n_batch)

    def mod_spec(k):
        return pl.BlockSpec((None, None, 1, d), lambda p, j, s: (l, grp(tile(p, s)), 0, k))

    g3, g_spec = _layer_rows(gains, l)
    return pl.pallas_call(
        functools.partial(_inproj_kernel, n_tiles=n_tiles, n_main=n_main),
        out_shape=(jax.ShapeDtypeStruct((nt, MAIN_COLS), BF16),
                   jax.ShapeDtypeStruct((nt // GRID_W, GRID_W, S5_DIM), F32),
                   jax.ShapeDtypeStruct((nt, LANES), F32)),
        grid=(pl.cdiv(n_tiles, 2), n_main + 1, 2),
        in_specs=[pl.BlockSpec((tm, d), lambda p, j, s: (x_tile(p, j, s), 0)),
                  g_spec, mod_spec(0), mod_spec(1),
                  pl.BlockSpec((None, d, tn), lambda p, j, s: (l, 0, j))],
        out_specs=(pl.BlockSpec((tm, tn), lambda p, j, s: (main_tile(p, j, s), jnp.minimum(j, n_main - 1))),
                   pl.BlockSpec((tm // GRID_W, GRID_W, S5_DIM), lambda p, j, s: (side_tile(p, j, s), 0, 0)),
                   pl.BlockSpec((tm, LANES), lambda p, j, s: (side_tile(p, j, s), 0))),
        scratch_shapes=[pltpu.VMEM((2, tm, d), BF16)],
        compiler_params=_cparams(("arbitrary", "arbitrary", "arbitrary")),
        name="in_proj",
    )(r, g3, mods4, mods4, w_all)


def _seq_edges(i, tiles_per_lat_seq, n_lat_tiles):
    is_lat = i < n_lat_tiles
    first = jnp.logical_or(jnp.logical_not(is_lat), (i % tiles_per_lat_seq) == 0)
    last = jnp.logical_or(jnp.logical_not(is_lat), (i % tiles_per_lat_seq) == tiles_per_lat_seq - 1)
    return first, last


def _conv5_kernel(cur_ref, prev_ref, next_ref, w_ref, b_ref, o_ref, ext_ref, *, tiles_per_lat_seq,
                  n_lat_tiles):
    i = pl.program_id(0)
    first, last = _seq_edges(i, tiles_per_lat_seq, n_lat_tiles)
    h, tm = CONV_HALO, ROW_TILE
    ext_ref[0:h, :] = jnp.where(first, 0.0, prev_ref[...].astype(F32))
    ext_ref[h:h + tm, :] = cur_ref[...].astype(F32)
    ext_ref[h + tm:h + tm + h, :] = jnp.where(last, 0.0, next_ref[...].astype(F32))
    pad = (SSD_CONV - 1) // 2
    cw = 512
    for c in range(0, ext_ref.shape[1], cw):
        acc = jnp.broadcast_to(b_ref[:, c:c + cw], (tm, cw))
        for j in range(SSD_CONV):
            acc = acc + w_ref[j:j + 1, c:c + cw] * ext_ref[h - pad + j:h - pad + j + tm, c:c + cw]
        o_ref[:, c:c + cw] = _silu(acc).astype(BF16)


def _ssd_conv(p_main, conv_w, conv_b, l, lat_len, ctx_len, n_batch):
    nt = p_main.shape[0]
    tm, h = ROW_TILE, CONV_HALO
    assert ctx_len == tm and lat_len % tm == 0
    n_lat_tiles = n_batch * lat_len // tm
    cw = SSD_CONV_DIM // 2
    xblk = COL_X // cw
    nhb = nt // h
    depth = conv_w.shape[0]
    kern = functools.partial(_conv5_kernel, tiles_per_lat_seq=lat_len // tm, n_lat_tiles=n_lat_tiles)
    return pl.pallas_call(
        kern,
        out_shape=jax.ShapeDtypeStruct((nt, SSD_CONV_DIM), BF16),
        grid=(nt // tm, 2),
        in_specs=[pl.BlockSpec((tm, cw), lambda i, j: (i, xblk + j)),
                  pl.BlockSpec((h, cw), lambda i, j: (jnp.maximum(i * (tm // h) - 1, 0), xblk + j)),
                  pl.BlockSpec((h, cw), lambda i, j: (jnp.minimum((i + 1) * (tm // h), nhb - 1), xblk + j)),
                  pl.BlockSpec((None, SSD_CONV, cw), lambda i, j: (l, 0, j)),
                  pl.BlockSpec((None, 1, cw), lambda i, j: (l, 0, j))],
        out_specs=pl.BlockSpec((tm, cw), lambda i, j: (i, j)),
        scratch_shapes=[pltpu.VMEM((tm + 2 * h, cw), F32)],
        compiler_params=_cparams(("arbitrary", "arbitrary")),
        name="ssd_conv",
    )(p_main, p_main, p_main, conv_w, conv_b.reshape(depth, 1, -1))


def _head_cols(vals, width):
    lane = lax.broadcasted_iota(I32, (1, LANES), 1)
    halves = []
    per_half = LANES // width
    for hh in range(len(vals) // per_half):
        sel = vals[hh * per_half + per_half - 1]
        for k in range(per_half - 2, -1, -1):
            sel = jnp.where(lane < (k + 1) * width, vals[hh * per_half + k], sel)
        halves.append(sel)
    return jnp.concatenate(halves, axis=1)


def _ssd_kernel(*refs, reverse, final):
    if final:
        (x_ref, b_ref, c_ref, dt_ref, dtb_ref, alog_ref, hx_ref, yprev_ref, z_ref, dskip_ref, ng_ref,
         o_ref, state_ref, tmp_ref) = refs
    else:
        x_ref, b_ref, c_ref, dt_ref, dtb_ref, alog_ref, hx_ref, o_ref, state_ref = refs
    q = SSD_CHUNK
    r = SSD_HEADS // SSD_GROUPS
    gw = r * SSD_HEAD_DIM

    @pl.when(pl.program_id(1) == 0)
    def _():
        state_ref[...] = jnp.zeros_like(state_ref)

    lane = lax.broadcasted_iota(I32, (1, LANES), 1)
    dtp = jax.nn.softplus(dt_ref[...] + dtb_ref[...])
    a_head = -jnp.exp(alog_ref[...])
    a = jnp.where(lane < SSD_HEADS, dtp * a_head, 0.0)
    ri = lax.broadcasted_iota(I32, (q, q), 0)
    ci = lax.broadcasted_iota(I32, (q, q), 1)
    tri = (ci >= ri) if reverse else (ci <= ri)
    tri_b = jnp.where(tri, 1.0, 0.0).astype(BF16)
    a_hi, a_mid, a_lo = _split3(a)
    a_cs = _dot(tri_b, a_hi) + _dot(tri_b, a_mid) + _dot(tri_b, a_lo)
    a_cs_t = a_cs.T
    dtp_t = dtp.T
    a_end = a_cs[0:1, :] if reverse else a_cs[q - 1:q, :]
    lane_g = lax.broadcasted_iota(I32, (1, gw), 1)

    pieces = []
    for fac in (jnp.exp(a_cs), dtp * jnp.exp(a_end - a_cs)):
        hi = fac.astype(BF16)
        pieces += [hi, (fac - hi.astype(F32)).astype(BF16)]
    spread = _dot(jnp.concatenate(pieces, axis=0), hx_ref[...])
    e_all = spread[0:q] + spread[q:2 * q]
    w_all = spread[2 * q:3 * q] + spread[3 * q:4 * q]

    x = x_ref[...]
    for g in range(SSD_GROUPS):
        cg = c_ref[:, g * SSD_STATE:(g + 1) * SSD_STATE]
        bg = b_ref[:, g * SSD_STATE:(g + 1) * SSD_STATE]
        cb = lax.dot_general(cg, bg, (((1,), (1,)), ((), ())), preferred_element_type=F32)
        xg = x[:, g * gw:(g + 1) * gw]
        yg = jnp.zeros((q, gw), F32)
        dec = []
        for hl in range(r):
            h = g * r + hl
            col = a_cs[:, h:h + 1]
            row = a_cs_t[h:h + 1, :]
            lm = jnp.where(tri, jnp.exp(col - row), 0.0)
            m = (cb * lm * dtp_t[h:h + 1, :]).astype(BF16)
            in_head = jnp.logical_and(lane_g >= hl * SSD_HEAD_DIM, lane_g < (hl + 1) * SSD_HEAD_DIM)
            xm = jnp.where(in_head, xg, jnp.zeros_like(xg))
            yg = yg + _dot(m, xm)
            dec.append(jnp.exp(a_end[:, h:h + 1]))
        s_old = state_ref[g]
        gs = slice(g * gw, (g + 1) * gw)
        yg = yg + e_all[:, gs] * _dot(cg, s_old.astype(BF16))
        xw = (xg.astype(F32) * w_all[:, gs]).astype(BF16)
        upd = lax.dot_general(bg, xw, (((0,), (0,)), ((), ())), preferred_element_type=F32)
        state_ref[g] = _head_cols(dec, SSD_HEAD_DIM) * s_old + upd
        sl = slice(g * gw, (g + 1) * gw)
        if final:
            ytot = yprev_ref[:, sl] + yg + dskip_ref[:, sl] * xg.astype(F32)
            tmp_ref[:, sl] = ytot * _silu(z_ref[:, sl].astype(F32))
        else:
            o_ref[:, sl] = yg
    if final:
        gated = tmp_ref[...]
        ms = jnp.mean(gated * gated, axis=-1, keepdims=True)
        o_ref[...] = (gated * lax.rsqrt(ms + EPS) * ng_ref[...]).astype(o_ref.dtype)


def _ssd_scan(xbc, p_main, p_dt, dtb_all, alog_all, l, lat_len, ctx_len, n_batch, reverse, final_args=None):
    nt = xbc.shape[0]
    q = SSD_CHUNK
    ncl, ncc = lat_len // q, ctx_len // q
    ctx0 = n_batch * ncl
    dd = 1 if reverse else 0

    def blk(b, j):
        if reverse:
            return jnp.where(j < ncc, ctx0 + b * ncc + (ncc - 1 - j), b * ncl + (ncl - 1 - (j - ncc)))
        return jnp.where(j < ncc, ctx0 + b * ncc + j, b * ncl + (j - ncc))

    head_vec = pl.BlockSpec((None, 1, LANES), lambda b, j: (2 * l + dd, 0, 0))
    final = final_args is not None
    in_specs = [pl.BlockSpec((q, SSD_INNER), lambda b, j: (blk(b, j), 0)),
                pl.BlockSpec((q, SSD_GROUPS * SSD_STATE), lambda b, j: (blk(b, j), 2)),
                pl.BlockSpec((q, SSD_GROUPS * SSD_STATE), lambda b, j: (blk(b, j), 3)),
                pl.BlockSpec((q, LANES), lambda b, j: (blk(b, j), 0)),
                head_vec, head_vec,
                pl.BlockSpec((LANES, SSD_INNER), lambda b, j: (0, 0))]
    head_spread = (jnp.arange(SSD_INNER, dtype=I32)[None, :] // SSD_HEAD_DIM
                   == jnp.arange(LANES, dtype=I32)[:, None]).astype(BF16)
    args = [xbc, xbc, xbc, p_dt, dtb_all, alog_all, head_spread]
    scratch = [pltpu.VMEM((SSD_GROUPS, SSD_STATE, (SSD_HEADS // SSD_GROUPS) * SSD_HEAD_DIM), F32)]
    if final:
        y_prev, d_skip_all, norm_g_all = final_args
        ds3, ds_spec = _layer_rows(d_skip_all, l)
        ng3, ng_spec = _layer_rows(norm_g_all, l)
        in_specs += [pl.BlockSpec((q, SSD_INNER), lambda b, j: (blk(b, j), 0)),
                     pl.BlockSpec((q, SSD_INNER), lambda b, j: (blk(b, j), COL_Z // SSD_INNER)),
                     ds_spec, ng_spec]
        args += [y_prev, p_main, ds3, ng3]
        scratch.append(pltpu.VMEM((q, SSD_INNER), F32))
    return pl.pallas_call(
        functools.partial(_ssd_kernel, reverse=reverse, final=final),
        out_shape=jax.ShapeDtypeStruct((nt, SSD_INNER), BF16 if final else F32),
        grid=(n_batch, ncc + ncl),
        in_specs=in_specs,
        out_specs=pl.BlockSpec((q, SSD_INNER), lambda b, j: (blk(b, j), 0)),
        scratch_shapes=scratch,
        compiler_params=_cparams(("arbitrary", "arbitrary")),
        name="ssd_scan_rev" if reverse else "ssd_scan_fwd",
    )(*args)


def _glu(ref):
    v = ref[:, 0:CONF_DIM].astype(F32)
    gt = ref[:, CONF_DIM:2 * CONF_DIM].astype(F32)
    return v * jax.nn.sigmoid(gt)


def _conf_kernel(cur_ref, prev_ref, next_ref, dww_ref, dwb_ref, lng_ref, lnb_ref, pww_ref, pwb_ref,
                 og_ref, o_ref, ext_ref, acc_ref, sh_ref, *, tiles_per_lat_seq, n_lat_tiles):
    i = pl.program_id(0)
    first, last = _seq_edges(i, tiles_per_lat_seq, n_lat_tiles)
    h, tm = CONF_HALO, ROW_TILE
    ext_ref[0:h, :] = jnp.where(first, 0.0, _glu(prev_ref))
    ext_ref[h:h + tm, :] = _glu(cur_ref)
    ext_ref[h + tm:h + tm + h, :] = jnp.where(last, 0.0, _glu(next_ref))
    pad = (CONF_KERNEL - 1) // 2
    span = (CONF_KERNEL - 1) // SUBLANES * SUBLANES + tm
    for b in range(SUBLANES):
        sh_ref[b] = ext_ref[h - pad + b:h - pad + b + span, :]
    cw = 256
    for c in range(0, CONF_DIM, cw):
        acc = jnp.broadcast_to(dwb_ref[:, c:c + cw], (tm, cw))
        for j in range(CONF_KERNEL):
            a8 = j // SUBLANES * SUBLANES
            acc = acc + dww_ref[j:j + 1, c:c + cw] * sh_ref[j % SUBLANES, a8:a8 + tm, c:c + cw]
        acc_ref[:, c:c + cw] = acc
    u = acc_ref[...]
    mu = jnp.mean(u, axis=-1, keepdims=True)
    var = jnp.mean(jnp.square(u - mu), axis=-1, keepdims=True)
    y = (u - mu) * lax.rsqrt(var + EPS) * lng_ref[...] + lnb_ref[...]
    y = _silu(y)
    v = _dot(y.astype(BF16), pww_ref[...]) + pwb_ref[...]
    ms = jnp.mean(v * v, axis=-1, keepdims=True)
    o_ref[...] = (v * lax.rsqrt(ms + EPS) * og_ref[...]).astype(o_ref.dtype)


def _conformer(p_main, dw_w, dw_b, ln_g, ln_b, pw_w_bf, pw_b, out_g, l, lat_len, ctx_len, n_batch):
    nt = p_main.shape[0]
    tm, h = ROW_TILE, CONF_HALO
    n_lat_tiles = n_batch * lat_len // tm
    cblk = COL_CONF // (2 * CONF_DIM)
    nhb = nt // h
    kern = functools.partial(_conf_kernel, tiles_per_lat_seq=lat_len // tm, n_lat_tiles=n_lat_tiles)
    rows = [_layer_rows(a, l) for a in (dw_b, ln_g, ln_b)]
    rows2 = [_layer_rows(a, l) for a in (pw_b, out_g)]
    dww, dww_spec = _layer_mat(dw_w, l)
    pww, pww_spec = _layer_mat(pw_w_bf, l)
    return pl.pallas_call(
        kern,
        out_shape=jax.ShapeDtypeStruct((nt, CONF_DIM), BF16),
        grid=(nt // tm,),
        in_specs=[pl.BlockSpec((tm, 2 * CONF_DIM), lambda i: (i, cblk)),
                  pl.BlockSpec((h, 2 * CONF_DIM), lambda i: (jnp.maximum(i * (tm // h) - 1, 0), cblk)),
                  pl.BlockSpec((h, 2 * CONF_DIM), lambda i: (jnp.minimum((i + 1) * (tm // h), nhb - 1), cblk)),
                  dww_spec] + [s for _, s in rows] + [pww_spec] + [s for _, s in rows2],
        out_specs=pl.BlockSpec((tm, CONF_DIM), lambda i: (i, 0)),
        scratch_shapes=[pltpu.VMEM((tm + 2 * h, CONF_DIM), F32), pltpu.VMEM((tm, CONF_DIM), F32),
                        pltpu.VMEM((SUBLANES, (CONF_KERNEL - 1) // SUBLANES * SUBLANES + tm, CONF_DIM), F32)],
        compiler_params=_cparams(("arbitrary",)),
        name="conformer",
    )(p_main, p_main, p_main, dww, *[a for a, _ in rows], pww, *[a for a, _ in rows2])


def _gelu_tanh(v):
    return 0.5 * v * (1.0 + jnp.tanh(math.sqrt(2.0 / math.pi) * (v + 0.044715 * (v * v * v))))


def _s5_kernel(*refs, reverse, final, colmajor, seg):
    if final:
        (u_ref, yprev_ref, bblk_ref, cblk_ref, lam_ref, sin_ref, dskip_ref, gw_ref, gb_ref, og_ref,
         o_ref, sout_ref, h_ref, fin_ref, init_ref, carry_ref, *perm_refs) = refs
    else:
        (u_ref, bblk_ref, cblk_ref, lam_ref, sin_ref,
         o_ref, sout_ref, h_ref, fin_ref, init_ref, carry_ref, *perm_refs) = refs
    ns = S5_NSTATE
    nsub = SUBLANES
    win = seg * nsub
    jw = pl.program_id(1)

    @pl.when(jw == 0)
    def _():
        carry_ref[...] = sin_ref[...]

    n_lb = S5_DIM // LANES

    per_row = GRID_W // seg

    def sub_seg(s):
        return s // per_row, slice((s % per_row) * seg, (s % per_row + 1) * seg)

    def permuted(tile_ref, buf_ref):
        for s in range(nsub):
            g, rows = sub_seg(s)
            for k in range(n_lb):
                buf_ref[k, pl.ds(s, seg, stride=nsub), :] = tile_ref[g, rows, k * LANES:(k + 1) * LANES]
        return jnp.concatenate([buf_ref[k] for k in range(n_lb)], axis=1)

    if colmajor:
        u_win = u_ref[...].reshape(win, S5_DIM)
        prev_win = yprev_ref[...].reshape(win, S5_DIM) if final else None
    else:
        u_win = permuted(u_ref, perm_refs[0])
        prev_win = permuted(yprev_ref, perm_refs[1]) if final else None

    u_bf = u_win.astype(BF16)
    ch_per_tile = 2 * LANES // S5_STATE * S5_GROUP
    for j in range(2 * ns // (2 * LANES)):
        c0 = (j * ch_per_tile) % S5_DIM // LANES * LANES
        h_ref[:, j * 2 * LANES:(j + 1) * 2 * LANES] = _dot(
            u_bf[:, c0:c0 + LANES], bblk_ref[c0:c0 + LANES, j * 2 * LANES:(j + 1) * 2 * LANES])

    cw = 512
    n_chunks = ns // cw

    def lam_chunk(row, c):
        return (jnp.broadcast_to(lam_ref[row:row + 1, c * cw:(c + 1) * cw], (nsub, cw)),
                jnp.broadcast_to(lam_ref[row + 1:row + 2, c * cw:(c + 1) * cw], (nsub, cw)))

    def row0(i):
        step = (seg - 1 - i) if reverse else i
        return pl.multiple_of(step * nsub, nsub)

    for c in range(n_chunks):
        lre, lim = lam_chunk(0, c)
        cre = slice(c * cw, (c + 1) * cw)
        cim = slice(ns + c * cw, ns + (c + 1) * cw)

        def step1(i, hc, cre=cre, cim=cim, lre=lre, lim=lim):
            hre, him = hc
            r0 = row0(i)
            nre = lre * hre - lim * him + h_ref[pl.ds(r0, nsub), cre]
            nim = lre * him + lim * hre + h_ref[pl.ds(r0, nsub), cim]
            h_ref[pl.ds(r0, nsub), cre] = nre
            h_ref[pl.ds(r0, nsub), cim] = nim
            return nre, nim

        z0 = jnp.zeros((nsub, cw), F32)
        fre, fim = lax.fori_loop(0, seg, step1, (z0, z0), unroll=4)
        fin_ref[:, cre] = fre
        fin_ref[:, cim] = fim

    seg_row = {32: 2, 64: 4}[seg]
    gre, gim = lam_ref[seg_row:seg_row + 1, :], lam_ref[seg_row + 1:seg_row + 2, :]
    cur_re, cur_im = carry_ref[:, 0:ns], carry_ref[:, ns:2 * ns]
    order = range(nsub - 1, -1, -1) if reverse else range(nsub)
    for s in order:
        init_ref[s:s + 1, 0:ns] = cur_re
        init_ref[s:s + 1, ns:2 * ns] = cur_im
        f_re, f_im = fin_ref[s:s + 1, 0:ns], fin_ref[s:s + 1, ns:2 * ns]
        cur_re, cur_im = gre * cur_re - gim * cur_im + f_re, gre * cur_im + gim * cur_re + f_im
    carry_ref[:, 0:ns] = cur_re
    carry_ref[:, ns:2 * ns] = cur_im

    for c in range(n_chunks):
        lre, lim = lam_chunk(0, c)
        cre = slice(c * cw, (c + 1) * cw)
        cim = slice(ns + c * cw, ns + (c + 1) * cw)

        def step2(i, gc, cre=cre, cim=cim, lre=lre, lim=lim):
            g_re, g_im = gc
            n_re = lre * g_re - lim * g_im
            n_im = lre * g_im + lim * g_re
            r0 = row0(i)
            h_ref[pl.ds(r0, nsub), cre] = h_ref[pl.ds(r0, nsub), cre] + n_re
            h_ref[pl.ds(r0, nsub), cim] = h_ref[pl.ds(r0, nsub), cim] + n_im
            return n_re, n_im

        lax.fori_loop(0, seg, step2, (init_ref[:, cre], init_ref[:, cim]), unroll=4)

    halves = []
    st_per_tile = 2 * LANES // S5_GROUP * S5_STATE
    for n in range(S5_DIM // (2 * LANES)):
        oc = slice(n * 2 * LANES, (n + 1) * 2 * LANES)
        s_re = slice(n * st_per_tile, (n + 1) * st_per_tile)
        s_im = slice(ns + n * st_per_tile, ns + (n + 1) * st_per_tile)
        halves.append(_dot(h_ref[:, s_re].astype(BF16), cblk_ref[s_re, oc])
                      + _dot(h_ref[:, s_im].astype(BF16), cblk_ref[s_im, oc]))
    y_win = jnp.concatenate(halves, axis=1)

    if final:
        tot = prev_win + y_win + dskip_ref[...] * u_win
        gl = _gelu_tanh(tot)
        gate = jax.nn.sigmoid(_dot(gl.astype(BF16), gw_ref[...]) + gb_ref[...])
        v = gl * gate
        ms = jnp.mean(v * v, axis=-1, keepdims=True)
        y_win = v * lax.rsqrt(ms + EPS) * og_ref[...]

    if colmajor:
        o_ref[...] = y_win.reshape(o_ref.shape).astype(o_ref.dtype)
    else:
        y_ref = perm_refs[0]
        for k in range(n_lb):
            y_ref[k] = y_win[:, k * LANES:(k + 1) * LANES]
        for s in range(nsub):
            g, rows = sub_seg(s)
            for k in range(n_lb):
                o_ref[g, rows, k * LANES:(k + 1) * LANES] = (
                    y_ref[k, pl.ds(s, seg, stride=nsub), :].astype(o_ref.dtype))

    @pl.when(jw == pl.num_programs(1) - 1)
    def _():
        sout_ref[...] = carry_ref[...]


def _s5_scan(u3d, ops, ld, state_in, n_batch, lat_len, reverse, colmajor, final_args=None):
    ns2 = 2 * S5_NSTATE
    bblk, cblk, lam = ops
    if colmajor:
        seg = GRID_W
        tile = (seg, SUBLANES, S5_DIM)
        n_win = GRID_W // SUBLANES
        out_shape = (n_batch * seg, GRID_W, S5_DIM)
        imap = (lambda b, j: (b, n_win - 1 - j, 0)) if reverse else (lambda b, j: (b, j, 0))
        in_map = imap
    else:
        seg = S5_SEG
        tile = (S5_WIN // GRID_W, GRID_W, S5_DIM)
        n_win = 1
        out_shape = (n_batch * S5_WIN // GRID_W, GRID_W, S5_DIM)
        ctx_blk0 = n_batch * lat_len // S5_WIN
        imap = lambda b, j: (b, 0, 0)
        in_map = lambda b, j: (ctx_blk0 + b, 0, 0)
    win = seg * SUBLANES
    st = pl.BlockSpec((None, 1, ns2), lambda b, j: (b, 0, 0))
    stacked = lambda a: pl.BlockSpec((None,) + a.shape[1:], lambda b, j: (ld, 0, 0))
    final = final_args is not None
    in_specs = [pl.BlockSpec(tile, in_map)]
    args = [u3d]
    if final:
        in_specs.append(pl.BlockSpec(tile, imap))
        args.append(final_args[0])
    in_specs += [stacked(bblk), stacked(cblk), stacked(lam), st]
    args += [bblk, cblk, lam, state_in]
    scratch = [pltpu.VMEM((win, ns2), F32),
               pltpu.VMEM((SUBLANES, ns2), F32), pltpu.VMEM((SUBLANES, ns2), F32),
               pltpu.VMEM((1, ns2), F32)]
    if not colmajor:
        perm = pltpu.VMEM((S5_DIM // LANES, win, LANES), F32)
        scratch += [perm, perm] if final else [perm]
    if final:
        _, l, d_skip, glu_w_bf, glu_b, out_g = final_args
        for a in (d_skip,):
            a3, sp = _layer_rows(a, l)
            in_specs.append(sp)
            args.append(a3)
        gw, gw_spec = _layer_mat(glu_w_bf, l)
        in_specs.append(gw_spec)
        args.append(gw)
        for a in (glu_b, out_g):
            a3, sp = _layer_rows(a, l)
            in_specs.append(sp)
            args.append(a3)
    out, s_out = pl.pallas_call(
        functools.partial(_s5_kernel, reverse=reverse, final=final, colmajor=colmajor, seg=seg),
        out_shape=(jax.ShapeDtypeStruct(out_shape, F32),
                   jax.ShapeDtypeStruct((n_batch, 1, ns2), F32)),
        grid=(n_batch, n_win),
        in_specs=in_specs,
        out_specs=(pl.BlockSpec(tile, imap), st),
        scratch_shapes=scratch,
        compiler_params=_cparams(("arbitrary", "arbitrary")),
        name="s5_" + ("rev" if reverse else "fwd") + ("_lat" if colmajor else "_ctx"),
    )(*args)
    return out, s_out


def _s5_operands(lam_re, lam_im, log_step, b_re, b_im, c_re, c_im):
    g, p, k = S5_GROUPS, S5_STATE, S5_GROUP
    lam = lax.complex(jnp.minimum(lam_re.astype(F32), -1e-4), lam_im.astype(F32))
    step = jnp.exp(log_step.astype(F32))[:, None]
    lam_bar = jnp.exp(lam * step)
    lam_seg = jnp.exp(lam * (step * S5_SEG))
    lam_col = jnp.exp(lam * (step * GRID_W))
    b_bar = ((lam_bar - 1.0) / lam)[..., None] * lax.complex(b_re.astype(F32), b_im.astype(F32))
    def block_diag(m):
        a, b = m.shape[1], m.shape[2]
        tiled = jnp.tile(m.reshape(g * a, b), (1, g))
        own = (lax.broadcasted_iota(I32, (g * a, g * b), 0) // a
               == lax.broadcasted_iota(I32, (g * a, g * b), 1) // b)
        return jnp.where(own, tiled, 0.0)

    bd_in = lambda m: block_diag(jnp.transpose(m, (0, 2, 1)))
    bblk = jnp.concatenate([bd_in(jnp.real(b_bar)), bd_in(jnp.imag(b_bar))], axis=1)
    bd_out = lambda m: block_diag(jnp.transpose(m, (0, 2, 1)))
    cblk = jnp.concatenate([bd_out(c_re.astype(F32)), -bd_out(c_im.astype(F32))], axis=0)
    zeros = jnp.zeros((g * p,), F32)
    lam_rows = jnp.stack([jnp.real(lam_bar).reshape(-1), jnp.imag(lam_bar).reshape(-1),
                          jnp.real(lam_seg).reshape(-1), jnp.imag(lam_seg).reshape(-1),
                          jnp.real(lam_col).reshape(-1), jnp.imag(lam_col).reshape(-1), zeros, zeros])
    return bblk.astype(BF16), cblk.astype(BF16), lam_rows


def _pack_bf16_pairs(v):
    n = v.shape[1] // 2
    bits = pltpu.bitcast(v.astype(BF16).astype(F32), U32)
    return (bits[:, :n] >> 16) | (bits[:, n:] & jnp.uint32(0xFFFF0000))


def _unpack_bf16_pairs(w):
    lo = pltpu.bitcast(w << 16, F32)
    hi = pltpu.bitcast(w & jnp.uint32(0xFFFF0000), F32)
    return lo, hi


def _store_token_tiles(ref, v):
    rows = v.shape[0]
    for c in range(SUBLANES):
        ref[pl.ds(c, rows, stride=SUBLANES), :] = v[:, c * LANES:(c + 1) * LANES]


def _load_token_tiles(ref, rows):
    return [ref[pl.ds(c, rows, stride=SUBLANES), :] for c in range(SUBLANES)]


def _mixout_kernel(r_ref, a_ref, b_ref, cl_ref, cc_ref, wa_ref, wb_ref, wc_ref, g1_ref, ng_ref, sh_ref,
                   sc_ref, rw_ref, rb_ref, r1_ref, hp_ref, idx_ref, gate_ref, rank_ref, cnt_ref,
                   carry_ref, *, n_lat_tiles):
    i = pl.program_id(0)

    @pl.when(i == 0)
    def _():
        carry_ref[...] = jnp.zeros_like(carry_ref)

    mix_c = jnp.where(i < n_lat_tiles, cl_ref[...], cc_ref[...]).reshape(a_ref.shape[0], S5_DIM).astype(BF16)
    acc = _dot(a_ref[...], wa_ref[...]) + _dot(b_ref[...], wb_ref[...]) + _dot(mix_c, wc_ref[...])
    x = r_ref[...] + g1_ref[...] * acc
    r1_ref[...] = x
    ms = jnp.mean(x * x, axis=-1, keepdims=True)
    h = x * lax.rsqrt(ms + EPS) * ng_ref[...]
    h = h * (1.0 + sc_ref[...]) + sh_ref[...]
    _store_token_tiles(hp_ref, _pack_bf16_pairs(h))

    h_hi = h.astype(BF16)
    h_lo = (h - h_hi.astype(F32)).astype(BF16)
    rw = rw_ref[...]
    w_hi = rw.astype(BF16)
    w_lo = (rw - w_hi.astype(F32)).astype(BF16)
    logits = _dot(h_hi, w_hi) + _dot(h_lo, w_hi) + _dot(h_hi, w_lo) + rb_ref[...]

    tm = logits.shape[0]
    lane = lax.broadcasted_iota(I32, (tm, LANES), 1)
    lane_f = lane.astype(F32)
    work = logits
    tops, picks = [], []
    for _ in range(TOP_K):
        m = jnp.max(work, axis=-1, keepdims=True)
        pick = jnp.min(jnp.where(work == m, lane_f, float(LANES)), axis=-1, keepdims=True)
        work = jnp.where(lane_f == pick, -jnp.inf, work)
        tops.append(m)
        picks.append(pick)
    exps = [jnp.exp(t - tops[0]) for t in tops]
    denom = exps[0] + exps[1] + exps[2] + exps[3]

    onehot = jnp.zeros((tm, LANES), F32)
    for k in range(TOP_K):
        onehot = onehot + jnp.where(lane_f == picks[k], 1.0, 0.0)
    ri = lax.broadcasted_iota(I32, (tm, tm), 0)
    ci = lax.broadcasted_iota(I32, (tm, tm), 1)
    before = jnp.where(ci < ri, 1.0, 0.0).astype(BF16)
    base = carry_ref[0:1, :] + _dot(before, onehot.astype(BF16))
    carry_ref[...] = carry_ref[...] + jnp.sum(onehot, axis=0, keepdims=True)
    cnt_ref[...] = carry_ref[...]

    idx_out = jnp.zeros((tm, LANES), F32)
    gate_out = jnp.zeros((tm, LANES), F32)
    rank_out = jnp.zeros((tm, LANES), F32)
    for k in range(TOP_K):
        rank_k = jnp.sum(jnp.where(lane_f == picks[k], base, 0.0), axis=-1, keepdims=True)
        idx_out = jnp.where(lane == k, picks[k], idx_out)
        gate_out = jnp.where(lane == k, exps[k] / denom, gate_out)
        rank_out = jnp.where(lane == k, rank_k, rank_out)
    gate_ref[...] = gate_out
    idx_t = idx_out.T.astype(I32)
    rank_t = rank_out.T.astype(I32)
    for k in range(TOP_K):
        for hh in range(tm // LANES):
            row = k * (tm // LANES) + hh
            idx_ref[row:row + 1, :] = idx_t[k:k + 1, hh * LANES:(hh + 1) * LANES]
            rank_ref[row:row + 1, :] = rank_t[k:k + 1, hh * LANES:(hh + 1) * LANES]


def _mix_out(r, mix_a, mix_b, mix_c_lat, mix_c_ctx, w_out_bf, mods4, norm2_g, router_w_pad, router_b_pad,
             l, lat_len, n_batch):
    nt, d = r.shape
    tm = ROW_TILE
    n_lat_tiles = n_batch * lat_len // tm

    def grp(i):
        return jnp.minimum((i * tm) // lat_len, n_batch)

    ng3, ng_spec = _layer_rows(norm2_g, l)
    rb3, rb_spec = _layer_rows(router_b_pad, l)
    tile = lambda w: pl.BlockSpec((tm, w), lambda i: (i, 0))
    dense_rows = tm * TOP_K // LANES
    dense = pl.BlockSpec((dense_rows, LANES), lambda i: (i, 0))
    n_dense = nt * TOP_K // LANES
    return pl.pallas_call(
        functools.partial(_mixout_kernel, n_lat_tiles=n_lat_tiles),
        out_shape=(jax.ShapeDtypeStruct((nt, d), F32), jax.ShapeDtypeStruct((nt * SUBLANES, LANES), U32),
                   jax.ShapeDtypeStruct((n_dense, LANES), I32), jax.ShapeDtypeStruct((nt, LANES), F32),
                   jax.ShapeDtypeStruct((n_dense, LANES), I32), jax.ShapeDtypeStruct((SUBLANES, LANES), F32)),
        grid=(nt // tm,),
        in_specs=[tile(d), tile(SSD_INNER), tile(CONF_DIM),
                  pl.BlockSpec((tm // GRID_W, GRID_W, S5_DIM), lambda i: (jnp.minimum(i, n_lat_tiles - 1), 0, 0)),
                  pl.BlockSpec((tm // GRID_W, GRID_W, S5_DIM), lambda i: (jnp.maximum(i - n_lat_tiles, 0), 0, 0)),
                  pl.BlockSpec((None, SSD_INNER, d), lambda i: (l, 0, 0)),
                  pl.BlockSpec((None, CONF_DIM, d), lambda i: (l, SSD_INNER // CONF_DIM, 0)),
                  pl.BlockSpec((None, S5_DIM, d), lambda i: (l, (SSD_INNER + CONF_DIM) // S5_DIM, 0)),
                  _mod_spec(l, grp, 2, d), ng_spec, _mod_spec(l, grp, 3, d), _mod_spec(l, grp, 4, d),
                  pl.BlockSpec((None, d, LANES), lambda i: (l, 0, 0)), rb_spec],
        out_specs=(tile(d), pl.BlockSpec((tm * SUBLANES, LANES), lambda i: (i, 0)), dense, tile(LANES), dense,
                   pl.BlockSpec((SUBLANES, LANES), lambda i: (0, 0))),
        scratch_shapes=[pltpu.VMEM((SUBLANES, LANES), F32)],
        compiler_params=_cparams(("arbitrary",)),
        name="mix_out_router",
    )(r, mix_a, mix_b, mix_c_lat, mix_c_ctx, w_out_bf, w_out_bf, w_out_bf, mods4, ng3, mods4, mods4,
      router_w_pad, rb3)


def _expert_kernel(be_ref, nv_ref, first_ref, nxt_ref, g_cur_ref, g_nxt_ref, s_cur_ref, s_prv_ref, h_hbm,
                   wgu_hbm, bgu_ref, wd_hbm, bd_ref, ya_hbm, xbuf, ybuf, wgu_st, wd_st, wgu_bf, wd_bf, zbuf,
                   gu_ref, gsem, ssem, wsem, zsem, *, layer):
    i = pl.program_id(0)
    n_steps = pl.num_programs(0)
    slot = i % 2
    nv = nv_ref[i]
    blk = MOE_BLOCK
    tile_rows = blk * SUBLANES

    def weight_copies(e):
        return (pltpu.make_async_copy(wgu_hbm.at[layer, e], wgu_st, wsem.at[0]),
                pltpu.make_async_copy(wd_hbm.at[layer, e], wd_st, wsem.at[1]))

    def token_tile(ref, row0):
        return ref.at[pl.ds(pl.multiple_of(row0, SUBLANES), SUBLANES)]

    def used(j):
        return jnp.logical_and(jnp.logical_and(j >= 0, j < n_steps),
                               nv_ref[jnp.clip(j, 0, n_steps - 1)] > 0)

    row_priority, scatter_priority, weight_priority = 0, 1, 1

    def gather_rows(idx_ref, dst_slot, lo_row, n):
        def body(rr, carry):
            pltpu.make_async_copy(token_tile(h_hbm, idx_ref[0, rr]),
                                  token_tile(xbuf.at[dst_slot], rr * SUBLANES),
                                  gsem.at[dst_slot]).start(priority=row_priority)
            return carry
        lax.fori_loop(lo_row, lo_row + n, body, 0, unroll=16)

    def scatter_rows(idx_ref, src_slot, lo_row, n):
        def body(rr, carry):
            pltpu.make_async_copy(token_tile(ybuf.at[src_slot], rr * SUBLANES),
                                  token_tile(ya_hbm, idx_ref[0, rr]),
                                  ssem.at[src_slot]).start(priority=scatter_priority)
            return carry
        lax.fori_loop(lo_row, lo_row + n, body, 0, unroll=16)

    def scatter_wait(src_slot):
        pltpu.make_async_copy(ybuf.at[src_slot], ya_hbm.at[pl.ds(0, tile_rows)], ssem.at[src_slot]).wait()

    def gather_wait(dst_slot):
        pltpu.make_async_copy(h_hbm.at[pl.ds(0, tile_rows)], xbuf.at[dst_slot], gsem.at[dst_slot]).wait()

    @pl.when(i == 0)
    def _():
        zbuf[...] = jnp.zeros_like(zbuf)
        for cp in weight_copies(be_ref[0]):
            cp.start(priority=weight_priority)
        gather_rows(g_cur_ref, 0, 0, blk)

    def compute_block(with_scatter):
        @pl.when(first_ref[i] == 1)
        def _():
            for cp in weight_copies(be_ref[i]):
                cp.wait()
            wgu_bf[...] = wgu_st[...].astype(BF16)
            wd_bf[...] = wd_st[...].astype(BF16)

            @pl.when(nxt_ref[i] >= 0)
            def _():
                for cp in weight_copies(nxt_ref[i]):
                    cp.start(priority=weight_priority)

        gather_wait(slot)
        half = D_MODEL // 2
        xw = jnp.concatenate(_load_token_tiles(xbuf.at[slot], blk), axis=1)
        lo, hi = _unpack_bf16_pairs(xw)
        lo, hi = lo.astype(BF16), hi.astype(BF16)
        n_col = 2 * D_EXPERT // (2 * LANES)
        bounds = [blk * c // n_col for c in range(n_col + 1)]
        for c in range(n_col):
            cols = slice(c * 2 * LANES, (c + 1) * 2 * LANES)
            gu_ref[:, cols] = (_dot(lo, wgu_bf[0:half, cols]) + _dot(hi, wgu_bf[half:D_MODEL, cols])
                               + bgu_ref[:, cols])
            for rr in range(bounds[c], bounds[c + 1]):
                pltpu.make_async_copy(token_tile(h_hbm, g_nxt_ref[0, rr]),
                                      xbuf.at[1 - slot, pl.ds(rr * SUBLANES, SUBLANES)],
                                      gsem.at[1 - slot]).start(priority=row_priority)
                if with_scatter:
                    pltpu.make_async_copy(ybuf.at[1 - slot, pl.ds(rr * SUBLANES, SUBLANES)],
                                          token_tile(ya_hbm, s_prv_ref[0, rr]),
                                          ssem.at[1 - slot]).start(priority=scatter_priority)
        gu = gu_ref[...]
        gate = jnp.minimum(gu[:, :D_EXPERT], SWIGLU_LIMIT)
        lin = jnp.clip(gu[:, D_EXPERT:], -SWIGLU_LIMIT, SWIGLU_LIMIT)
        act = (gate * jax.nn.sigmoid(SWIGLU_ALPHA * gate) * (lin + 1.0)).astype(BF16)
        y = _dot(act, wd_bf[...]) + bd_ref[...]

        @pl.when(used(i - 2))
        def _():
            scatter_wait(slot)

        _store_token_tiles(ybuf.at[slot], _pack_bf16_pairs(y))

    @pl.when(jnp.logical_and(nv > 0, i == 0))
    def _():
        compute_block(False)

    @pl.when(jnp.logical_and(nv > 0, i > 0))
    def _():
        compute_block(True)

    @pl.when(nv == 0)
    def _():
        gather_wait(slot)
        gather_rows(g_nxt_ref, 1 - slot, 0, blk)

        @pl.when(used(i - 2))
        def _():
            scatter_wait(slot)

        @pl.when(used(i - 1))
        def _():
            scatter_rows(s_prv_ref, 1 - slot, 0, blk)

        own = ya_hbm.at[pl.ds(pl.multiple_of(i * tile_rows, tile_rows), tile_rows)]
        zero_copy = pltpu.make_async_copy(zbuf, own, zsem)
        zero_copy.start()
        zero_copy.wait()

    @pl.when(i == n_steps - 1)
    def _():
        gather_wait(1 - slot)

        @pl.when(nv > 0)
        def _():
            scatter_rows(s_cur_ref, slot, 0, blk)
            scatter_wait(slot)

        @pl.when(used(i - 1))
        def _():
            scatter_wait(1 - slot)


def _experts(h_tiles, gather_row, scatter_row, block_e, n_valid, first, nxt, w_gu, b_gu, w_down, b_down,
             layer):
    n_rows = gather_row.shape[0]
    n_blocks = n_rows // MOE_BLOCK
    depth, ne = w_gu.shape[:2]
    g3 = gather_row.reshape(n_blocks, 1, MOE_BLOCK)
    s3 = scatter_row.reshape(n_blocks, 1, MOE_BLOCK)
    width = D_MODEL // 2
    tile_rows = MOE_BLOCK * SUBLANES
    idx_block = lambda imap: pl.BlockSpec((None, 1, MOE_BLOCK), imap, memory_space=pltpu.SMEM)
    grid_spec = pltpu.PrefetchScalarGridSpec(
        num_scalar_prefetch=4,
        grid=(n_blocks,),
        in_specs=[idx_block(lambda i, *_: (i, 0, 0)),
                  idx_block(lambda i, *_: (jnp.minimum(i + 1, n_blocks - 1), 0, 0)),
                  idx_block(lambda i, *_: (i, 0, 0)),
                  idx_block(lambda i, *_: (jnp.maximum(i - 1, 0), 0, 0)),
                  pl.BlockSpec(memory_space=pl.ANY),
                  pl.BlockSpec(memory_space=pl.ANY),
                  pl.BlockSpec((None, None, 1, 2 * D_EXPERT), lambda i, be, *_: (layer, be[i], 0, 0)),
                  pl.BlockSpec(memory_space=pl.ANY),
                  pl.BlockSpec((None, None, 1, D_MODEL), lambda i, be, *_: (layer, be[i], 0, 0))],
        out_specs=pl.BlockSpec(memory_space=pl.ANY),
        scratch_shapes=[pltpu.VMEM((2, tile_rows, LANES), U32), pltpu.VMEM((2, tile_rows, LANES), U32),
                        pltpu.VMEM((D_MODEL, 2 * D_EXPERT), F32), pltpu.VMEM((D_EXPERT, D_MODEL), F32),
                        pltpu.VMEM((D_MODEL, 2 * D_EXPERT), BF16), pltpu.VMEM((D_EXPERT, D_MODEL), BF16),
                        pltpu.VMEM((tile_rows, LANES), U32), pltpu.VMEM((MOE_BLOCK, 2 * D_EXPERT), F32),
                        pltpu.SemaphoreType.DMA((2,)), pltpu.SemaphoreType.DMA((2,)),
                        pltpu.SemaphoreType.DMA((2,)), pltpu.SemaphoreType.DMA],
    )
    return pl.pallas_call(
        functools.partial(_expert_kernel, layer=layer),
        out_shape=jax.ShapeDtypeStruct((n_rows * SUBLANES, LANES), U32),
        grid_spec=grid_spec,
        compiler_params=_cparams(("arbitrary",)),
        name="moe_experts",
    )(block_e, n_valid, first, nxt, g3, g3, s3, s3, h_tiles, w_gu, b_gu.reshape(depth, ne, 1, -1), w_down,
      b_down.reshape(depth, ne, 1, -1))


def _combine_kernel(r_ref, gate_ref, y0_ref, y1_ref, y2_ref, y3_ref, g2_ref, fg_ref, o_ref, *, last_layer):
    tm = r_ref.shape[0]
    half = D_MODEL // 2
    gates = [gate_ref[:, k:k + 1] for k in range(TOP_K)]
    y_refs = (y0_ref, y1_ref, y2_ref, y3_ref)
    for c in range(SUBLANES):
        acc_lo = jnp.zeros((tm, LANES), F32)
        acc_hi = jnp.zeros((tm, LANES), F32)
        for k in range(TOP_K):
            lo, hi = _unpack_bf16_pairs(y_refs[k][pl.ds(c, tm, stride=SUBLANES), :])
            acc_lo = acc_lo + gates[k] * lo
            acc_hi = acc_hi + gates[k] * hi
        lo_cols = slice(c * LANES, (c + 1) * LANES)
        hi_cols = slice(half + c * LANES, half + (c + 1) * LANES)
        o_ref[:, lo_cols] = r_ref[:, lo_cols] + g2_ref[:, lo_cols] * acc_lo
        o_ref[:, hi_cols] = r_ref[:, hi_cols] + g2_ref[:, hi_cols] * acc_hi
    if last_layer:
        x = o_ref[...]
        ms = jnp.mean(x * x, axis=-1, keepdims=True)
        o_ref[...] = x * lax.rsqrt(ms + EPS) * fg_ref[...]


def _combine(r1, gates, ya, mods4, final_g, l, lat_len, n_batch, last_layer):
    nt, d = r1.shape
    tm = ROW_TILE
    n_out = n_batch * lat_len if last_layer else nt
    tiles_per_k = nt // tm

    def grp(i):
        return jnp.minimum((i * tm) // lat_len, n_batch)

    def choice(k):
        return pl.BlockSpec((tm * SUBLANES, LANES), lambda i: (k * tiles_per_k + i, 0))

    return pl.pallas_call(
        functools.partial(_combine_kernel, last_layer=last_layer),
        out_shape=jax.ShapeDtypeStruct((n_out, d), F32),
        grid=(n_out // tm,),
        in_specs=[pl.BlockSpec((tm, d), lambda i: (i, 0)),
                  pl.BlockSpec((tm, LANES), lambda i: (i, 0)),
                  choice(0), choice(1), choice(2), choice(3),
                  _mod_spec(l, grp, 5, d),
                  pl.BlockSpec((1, d), lambda i: (0, 0))],
        out_specs=pl.BlockSpec((tm, d), lambda i: (i, 0)),
        compiler_params=_cparams(("arbitrary",)),
        name="moe_combine",
    )(r1, gates, ya, ya, ya, ya, mods4, final_g.reshape(1, d))


def _routing_plan(top_idx, rank, counts, n_blocks):
    n_assign = top_idx.size
    n_tok = n_assign // TOP_K
    n_rows = n_blocks * MOE_BLOCK
    counts = counts.astype(I32)
    padded = (counts + MOE_BLOCK - 1) // MOE_BLOCK * MOE_BLOCK
    pad_end = jnp.cumsum(padded)
    pad_start = pad_end - padded
    count_end = jnp.cumsum(counts)
    per_tile = ROW_TILE * TOP_K // LANES
    q = lax.broadcasted_iota(I32, top_idx.shape, 0)
    lane = lax.broadcasted_iota(I32, top_idx.shape, 1)
    halves = ROW_TILE // LANES
    token = (q // per_tile) * ROW_TILE + (q % halves) * LANES + lane
    choice = (q % per_tile) // halves
    out_tile = choice * n_tok + token
    dest = pad_start[top_idx] + rank
    blk_start = jnp.arange(n_blocks, dtype=I32) * MOE_BLOCK

    def expert_at(pos):
        return jnp.minimum(jnp.sum((pad_end[None, :] <= pos[:, None]).astype(I32), axis=1), N_EXPERTS - 1)

    block_e = expert_at(blk_start)
    used = blk_start < pad_end[-1]
    n_valid = jnp.where(used, jnp.clip(pad_start[block_e] + counts[block_e] - blk_start, 0, MOE_BLOCK), 0)
    first = jnp.logical_and(used, blk_start == pad_start[block_e]).astype(I32)
    nxt_start = pad_end[block_e]
    nxt = jnp.where(nxt_start < pad_end[-1], expert_at(nxt_start), -1).astype(I32)

    row = jnp.arange(n_rows, dtype=I32)
    dump = n_assign + row - jnp.repeat(count_end[block_e], MOE_BLOCK)
    row_tile = dump.at[dest].set(out_tile, unique_indices=True, mode='drop')
    gather_row = (row_tile % n_tok) * SUBLANES
    scatter_row = row_tile * SUBLANES
    return gather_row.astype(I32), scatter_row.astype(I32), block_e.astype(I32), n_valid.astype(I32), first, nxt


def _forward(x, c, ctx, c_ctx, ada_w, ada_b, norm1_g, w_in, ssd_conv_w, ssd_conv_b, ssd_a_log,
             ssd_dt_bias, ssd_d, ssd_norm_g, conf_dw_w, conf_dw_b, conf_ln_g, conf_ln_b, conf_pw_w,
             conf_pw_b, conf_out_g, s5_lam_re, s5_lam_im, s5_log_step, s5_b_re, s5_b_im, s5_c_re,
             s5_c_im, s5_d, s5_glu_w, s5_glu_b, s5_out_g, w_out, norm2_g, router_w, router_b,
             w_gate_up, b_gate_up, w_down, b_down, final_norm_g):
    n_batch, lat_len, d = x.shape
    ctx_len = ctx.shape[1]
    depth = ada_w.shape[0]
    n_lat = n_batch * lat_len
    nt = n_lat + n_batch * ctx_len
    assert d == D_MODEL and ctx_len == ROW_TILE and lat_len % MM_TILE_M == 0
    assert lat_len // GRID_W == GRID_W

    r = jnp.concatenate([x.reshape(n_lat, d), ctx.reshape(n_batch * ctx_len, d)], axis=0).astype(F32)
    cond = jnp.zeros((SUBLANES, d), F32).at[:n_batch].set(c).at[n_batch].set(c_ctx)
    mods4 = _modulation(cond, ada_w, ada_b).reshape(depth, SUBLANES, 1, 6 * d)

    c_dt = SSD_INNER + SSD_CONV_DIM
    c_conf = c_dt + SSD_HEADS
    c_s5 = c_conf + 2 * CONF_DIM
    w_proj = jnp.concatenate(
        [w_in[:, :, :c_dt], w_in[:, :, c_conf:c_s5], w_in[:, :, c_s5:], w_in[:, :, c_dt:c_conf],
         jnp.zeros((depth, d, MM_TILE_N - S5_DIM - SSD_HEADS), w_in.dtype)], axis=2).astype(BF16)
    w_out_bf = w_out.astype(BF16)
    conf_pw_bf = conf_pw_w.astype(BF16)
    s5_glu_bf = s5_glu_w.astype(BF16)
    head_pad = lambda a: jnp.pad(a.astype(F32), ((0, 0), (0, 0), (0, LANES - SSD_HEADS))).reshape(
        depth * 2, 1, LANES)
    dtb_all, alog_all = head_pad(ssd_dt_bias), head_pad(ssd_a_log)
    d_skip_all = jnp.repeat(ssd_d.astype(F32), SSD_HEAD_DIM, axis=1)
    router_w_pad = jnp.pad(router_w.astype(F32), ((0, 0), (0, 0), (0, LANES - N_EXPERTS)))
    router_b_pad = jnp.pad(router_b.astype(F32), ((0, 0), (0, LANES - N_EXPERTS)), constant_values=-1e30)
    flat2 = lambda a: a.reshape((depth * 2,) + a.shape[2:])
    s5_ops = jax.vmap(_s5_operands)(*[flat2(a) for a in (s5_lam_re, s5_lam_im, s5_log_step, s5_b_re,
                                                         s5_b_im, s5_c_re, s5_c_im)])

    n_blocks = nt * TOP_K // MOE_BLOCK + N_EXPERTS
    s5_zero = jnp.zeros((n_batch, 1, 2 * S5_NSTATE), F32)

    for l in range(depth):
        p_main, p_s5, p_dt = _in_proj(r, norm1_g, mods4, l, w_proj, lat_len, n_batch)

        xbc = _ssd_conv(p_main, ssd_conv_w, ssd_conv_b, l, lat_len, ctx_len, n_batch)
        y_fwd = _ssd_scan(xbc, p_main, p_dt, dtb_all, alog_all, l, lat_len, ctx_len, n_batch, False)
        mix_a = _ssd_scan(xbc, p_main, p_dt, dtb_all, alog_all, l, lat_len, ctx_len, n_batch, True,
                          (y_fwd, d_skip_all, ssd_norm_g))

        mix_b = _conformer(p_main, conf_dw_w, conf_dw_b, conf_ln_g, conf_ln_b, conf_pw_bf, conf_pw_b,
                           conf_out_g, l, lat_len, ctx_len, n_batch)

        yc_f, st_f = _s5_scan(p_s5, s5_ops, 2 * l, s5_zero, n_batch, lat_len, False, False)
        yl_f, _ = _s5_scan(p_s5, s5_ops, 2 * l, st_f, n_batch, lat_len, False, True)
        fin = (l, s5_d, s5_glu_bf, s5_glu_b, s5_out_g)
        mc_c, st_r = _s5_scan(p_s5, s5_ops, 2 * l + 1, s5_zero, n_batch, lat_len, True, False, (yc_f,) + fin)
        mc_l, _ = _s5_scan(p_s5, s5_ops, 2 * l + 1, st_r, n_batch, lat_len, True, True, (yl_f,) + fin)

        r1, h_tiles, top_idx, gates, rank, counts = _mix_out(
            r, mix_a, mix_b, mc_l, mc_c, w_out_bf, mods4, norm2_g, router_w_pad,
            router_b_pad, l, lat_len, n_batch)

        gather_row, scatter_row, block_e, n_valid, first, nxt = _routing_plan(
            top_idx, rank, counts[0, :N_EXPERTS], n_blocks)
        ya = _experts(h_tiles, gather_row, scatter_row, block_e, n_valid, first, nxt, w_gate_up, b_gate_up,
                      w_down, b_down, l)
        r = _combine(r1, gates, ya, mods4, final_norm_g, l, lat_len, n_batch, l == depth - 1)

    return r.reshape(n_batch, lat_len, d).astype(x.dtype)


def kernel(x, c, ctx, c_ctx, ada_w, ada_b, norm1_g, w_in, ssd_conv_w, ssd_conv_b, ssd_a_log, ssd_dt_bias,
           ssd_d, ssd_norm_g, conf_dw_w, conf_dw_b, conf_ln_g, conf_ln_b, conf_pw_w, conf_pw_b, conf_out_g,
           s5_lam_re, s5_lam_im, s5_log_step, s5_b_re, s5_b_im, s5_c_re, s5_c_im, s5_d, s5_glu_w, s5_glu_b,
           s5_out_g, w_out, norm2_g, router_w, router_b, w_gate_up, b_gate_up, w_down, b_down, final_norm_g):
    return _forward(x, c, ctx, c_ctx, ada_w, ada_b, norm1_g, w_in, ssd_conv_w, ssd_conv_b, ssd_a_log,
                    ssd_dt_bias, ssd_d, ssd_norm_g, conf_dw_w, conf_dw_b, conf_ln_g, conf_ln_b, conf_pw_w,
                    conf_pw_b, conf_out_g, s5_lam_re, s5_lam_im, s5_log_step, s5_b_re, s5_b_im, s5_c_re,
                    s5_c_im, s5_d, s5_glu_w, s5_glu_b, s5_out_g, w_out, norm2_g, router_w, router_b,
                    w_gate_up, b_gate_up, w_down, b_down, final_norm_g)
```

```python
import functools
import math

import jax
import jax.numpy as jnp
from jax import lax
from jax.experimental import pallas as pl
from jax.experimental.pallas import tpu as pltpu

F32 = jnp.float32
BF16 = jnp.bfloat16
I32 = jnp.int32
U32 = jnp.uint32

D_MODEL = 2048
GRID_W = 64
SSD_INNER = 1024
SSD_HEAD_DIM = 64
SSD_HEADS = 16
SSD_GROUPS = 4
SSD_STATE = 128
SSD_CONV = 5
SSD_CHUNK = 128
SSD_CONV_DIM = SSD_INNER + 2 * SSD_GROUPS * SSD_STATE
CONF_DIM = 512
CONF_KERNEL = 31
S5_DIM = 512
S5_GROUP = 16
S5_GROUPS = 32
S5_STATE = 64
S5_NSTATE = S5_GROUPS * S5_STATE
N_EXPERTS = 32
TOP_K = 4
D_EXPERT = 768
SWIGLU_LIMIT = 7.0
SWIGLU_ALPHA = 1.702
MOE_BLOCK = 256
EPS = 1e-6

LANES = 128
SUBLANES = 8
ROW_TILE = 256
MM_TILE_M = 512
MM_TILE_N = 1024
MAIN_COLS = 4096
COL_Z, COL_X, COL_CONF = 0, 1024, 3072
SIDE_COLS = S5_DIM + LANES
CONV_HALO = 16
CONF_HALO = 16
S5_SEG = 32
S5_WIN = SUBLANES * S5_SEG
VMEM_LIMIT = 56 * 1024 * 1024


def _cparams(sem, vmem=VMEM_LIMIT):
    return pltpu.CompilerParams(dimension_semantics=sem, vmem_limit_bytes=vmem)


def _silu(v):
    return v * jax.nn.sigmoid(v)


def _split3(v):
    hi = v.astype(BF16)
    r1 = v - hi.astype(F32)
    mid = r1.astype(BF16)
    lo = (r1 - mid.astype(F32)).astype(BF16)
    return hi, mid, lo


def _dot(a, b):
    return jnp.dot(a, b, preferred_element_type=F32)


def _layer_rows(arr, l):
    depth = arr.shape[0]
    a3 = arr.reshape(depth, 1, -1)
    return a3, pl.BlockSpec((None, 1, a3.shape[2]), lambda *_: (l, 0, 0))


def _layer_mat(arr, l):
    return arr, pl.BlockSpec((None,) + arr.shape[1:], lambda *_: (l, 0, 0))


def _mod_kernel(c_ref, w_ref, b_ref, o_ref):
    s = _silu(c_ref[...])
    w = w_ref[...]
    s_hi = s.astype(BF16)
    s_lo = (s - s_hi.astype(F32)).astype(BF16)
    w_hi = w.astype(BF16)
    w_lo = (w - w_hi.astype(F32)).astype(BF16)
    acc = _dot(s_hi, w_hi) + _dot(s_lo, w_hi) + _dot(s_hi, w_lo)
    o_ref[...] = acc + b_ref[...]


def _modulation(cond, ada_w, ada_b):
    depth, d, n = ada_w.shape
    tn = 1024
    return pl.pallas_call(
        _mod_kernel,
        out_shape=jax.ShapeDtypeStruct((depth, SUBLANES, n), F32),
        grid=(depth, n // tn),
        in_specs=[pl.BlockSpec((SUBLANES, d), lambda l, j: (0, 0)),
                  pl.BlockSpec((None, d, tn), lambda l, j: (l, 0, j)),
                  pl.BlockSpec((None, 1, tn), lambda l, j: (l, 0, j))],
        out_specs=pl.BlockSpec((None, SUBLANES, tn), lambda l, j: (l, 0, j)),
        compiler_params=_cparams(("arbitrary", "arbitrary")),
        name="adaln_mod",
    )(cond, ada_w, ada_b.reshape(depth, 1, n))


def _mod_spec(l, grp, k, d):
    return pl.BlockSpec((None, None, 1, d), lambda i, *_: (l, grp(i), 0, k))


def _inproj_kernel(x_ref, g_ref, sh_ref, sc_ref, w_ref, main_ref, s5_ref, dt_ref, hn_ref, *, n_tiles,
                   n_main):
    j = pl.program_id(1)
    s = pl.program_id(2)
    live = 2 * pl.program_id(0) + s < n_tiles

    @pl.when(jnp.logical_and(j == 0, live))
    def _():
        x = x_ref[...]
        ms = jnp.mean(x * x, axis=-1, keepdims=True)
        y = x * lax.rsqrt(ms + EPS) * g_ref[...]
        hn_ref[s] = (y * (1.0 + sc_ref[...]) + sh_ref[...]).astype(BF16)

    @pl.when(jnp.logical_and(live, j < n_main))
    def _():
        main_ref[...] = _dot(hn_ref[s], w_ref[...]).astype(main_ref.dtype)

    @pl.when(jnp.logical_and(live, j == n_main))
    def _():
        res = _dot(hn_ref[s], w_ref[...])
        s5_ref[...] = res[:, 0:S5_DIM].reshape(s5_ref.shape)
        dt_ref[...] = res[:, S5_DIM:S5_DIM + LANES]


def _in_proj(r, gains, mods4, l, w_all, lat_len, n_batch):
    nt, d = r.shape
    tm = MM_TILE_M if nt % MM_TILE_M == 0 else ROW_TILE
    tn = MM_TILE_N
    n_main = MAIN_COLS // tn
    assert w_all.shape[2] == (n_main + 1) * tn and SIDE_COLS <= tn
    n_tiles = nt // tm
    last = n_tiles - 1

    def tile(p, s):
        return jnp.minimum(2 * p + s, last)

    def pair_end(p):
        return jnp.minimum(2 * p + 1, last)

    def x_tile(p, j, s):
        return jnp.where(j == 0, tile(p, s), pair_end(p))

    def main_tile(p, j, s):
        return jnp.where(j < n_main, tile(p, s), pair_end(p))

    def side_tile(p, j, s):
        return jnp.where(j == n_main, tile(p, s), jnp.maximum(2 * p - 1, 0))

    def grp(t):
        return jnp.minimum((t * tm) // lat_len, n_batch)

    def mod_spec(k):
        return pl.BlockSpec((None, None, 1, d), lambda p, j, s: (l, grp(tile(p, s)), 0, k))

    g3, g_spec = _layer_rows(gains, l)
    return pl.pallas_call(
        functools.partial(_inproj_kernel, n_tiles=n_tiles, n_main=n_main),
        out_shape=(jax.ShapeDtypeStruct((nt, MAIN_COLS), BF16),
                   jax.ShapeDtypeStruct((nt // GRID_W, GRID_W, S5_DIM), F32),
                   jax.ShapeDtypeStruct((nt, LANES), F32)),
        grid=(pl.cdiv(n_tiles, 2), n_main + 1, 2),
        in_specs=[pl.BlockSpec((tm, d), lambda p, j, s: (x_tile(p, j, s), 0)),
                  g_spec, mod_spec(0), mod_spec(1),
                  pl.BlockSpec((None, d, tn), lambda p, j, s: (l, 0, j))],
        out_specs=(pl.BlockSpec((tm, tn), lambda p, j, s: (main_tile(p, j, s), jnp.minimum(j, n_main - 1))),
                   pl.BlockSpec((tm // GRID_W, GRID_W, S5_DIM), lambda p, j, s: (side_tile(p, j, s), 0, 0)),
                   pl.BlockSpec((tm, LANES), lambda p, j, s: (side_tile(p, j, s), 0))),
        scratch_shapes=[pltpu.VMEM((2, tm, d), BF16)],
        compiler_params=_cparams(("arbitrary", "arbitrary", "arbitrary")),
        name="in_proj",
    )(r, g3, mods4, mods4, w_all)


def _seq_edges(i, tiles_per_lat_seq, n_lat_tiles):
    is_lat = i < n_lat_tiles
    first = jnp.logical_or(jnp.logical_not(is_lat), (i % tiles_per_lat_seq) == 0)
    last = jnp.logical_or(jnp.logical_not(is_lat), (i % tiles_per_lat_seq) == tiles_per_lat_seq - 1)
    return first, last


def _conv5_kernel(cur_ref, prev_ref, next_ref, w_ref, b_ref, o_ref, ext_ref, *, tiles_per_lat_seq,
                  n_lat_tiles):
    i = pl.program_id(0)
    first, last = _seq_edges(i, tiles_per_lat_seq, n_lat_tiles)
    h, tm = CONV_HALO, ROW_TILE
    ext_ref[0:h, :] = jnp.where(first, 0.0, prev_ref[...].astype(F32))
    ext_ref[h:h + tm, :] = cur_ref[...].astype(F32)
    ext_ref[h + tm:h + tm + h, :] = jnp.where(last, 0.0, next_ref[...].astype(F32))
    pad = (SSD_CONV - 1) // 2
    cw = 512
    for c in range(0, ext_ref.shape[1], cw):
        acc = jnp.broadcast_to(b_ref[:, c:c + cw], (tm, cw))
        for j in range(SSD_CONV):
            acc = acc + w_ref[j:j + 1, c:c + cw] * ext_ref[h - pad + j:h - pad + j + tm, c:c + cw]
        o_ref[:, c:c + cw] = _silu(acc).astype(BF16)


def _ssd_conv(p_main, conv_w, conv_b, l, lat_len, ctx_len, n_batch):
    nt = p_main.shape[0]
    tm, h = ROW_TILE, CONV_HALO
    assert ctx_len == tm and lat_len % tm == 0
    n_lat_tiles = n_batch * lat_len // tm
    cw = SSD_CONV_DIM // 2
    xblk = COL_X // cw
    nhb = nt // h
    depth = conv_w.shape[0]
    kern = functools.partial(_conv5_kernel, tiles_per_lat_seq=lat_len // tm, n_lat_tiles=n_lat_tiles)
    return pl.pallas_call(
        kern,
        out_shape=jax.ShapeDtypeStruct((nt, SSD_CONV_DIM), BF16),
        grid=(nt // tm, 2),
        in_specs=[pl.BlockSpec((tm, cw), lambda i, j: (i, xblk + j)),
                  pl.BlockSpec((h, cw), lambda i, j: (jnp.maximum(i * (tm // h) - 1, 0), xblk + j)),
                  pl.BlockSpec((h, cw), lambda i, j: (jnp.minimum((i + 1) * (tm // h), nhb - 1), xblk + j)),
                  pl.BlockSpec((None, SSD_CONV, cw), lambda i, j: (l, 0, j)),
                  pl.BlockSpec((None, 1, cw), lambda i, j: (l, 0, j))],
        out_specs=pl.BlockSpec((tm, cw), lambda i, j: (i, j)),
        scratch_shapes=[pltpu.VMEM((tm + 2 * h, cw), F32)],
        compiler_params=_cparams(("arbitrary", "arbitrary")),
        name="ssd_conv",
    )(p_main, p_main, p_main, conv_w, conv_b.reshape(depth, 1, -1))


def _head_cols(vals, width):
    lane = lax.broadcasted_iota(I32, (1, LANES), 1)
    halves = []
    per_half = LANES // width
    for hh in range(len(vals) // per_half):
        sel = vals[hh * per_half + per_half - 1]
        for k in range(per_half - 2, -1, -1):
            sel = jnp.where(lane < (k + 1) * width, vals[hh * per_half + k], sel)
        halves.append(sel)
    return jnp.concatenate(halves, axis=1)


def _ssd_kernel(*refs, reverse, final):
    if final:
        (x_ref, b_ref, c_ref, dt_ref, dtb_ref, alog_ref, hx_ref, yprev_ref, z_ref, dskip_ref, ng_ref,
         o_ref, state_ref, tmp_ref) = refs
    else:
        x_ref, b_ref, c_ref, dt_ref, dtb_ref, alog_ref, hx_ref, o_ref, state_ref = refs
    q = SSD_CHUNK
    r = SSD_HEADS // SSD_GROUPS
    gw = r * SSD_HEAD_DIM

    @pl.when(pl.program_id(1) == 0)
    def _():
        state_ref[...] = jnp.zeros_like(state_ref)

    lane = lax.broadcasted_iota(I32, (1, LANES), 1)
    dtp = jax.nn.softplus(dt_ref[...] + dtb_ref[...])
    a_head = -jnp.exp(alog_ref[...])
    a = jnp.where(lane < SSD_HEADS, dtp * a_head, 0.0)
    ri = lax.broadcasted_iota(I32, (q, q), 0)
    ci = lax.broadcasted_iota(I32, (q, q), 1)
    tri = (ci >= ri) if reverse else (ci <= ri)
    tri_b = jnp.where(tri, 1.0, 0.0).astype(BF16)
    a_hi, a_mid, a_lo = _split3(a)
    a_cs = _dot(tri_b, a_hi) + _dot(tri_b, a_mid) + _dot(tri_b, a_lo)
    a_cs_t = a_cs.T
    dtp_t = dtp.T
    a_end = a_cs[0:1, :] if reverse else a_cs[q - 1:q, :]
    lane_g = lax.broadcasted_iota(I32, (1, gw), 1)

    pieces = []
    for fac in (jnp.exp(a_cs), dtp * jnp.exp(a_end - a_cs)):
        hi = fac.astype(BF16)
        pieces += [hi, (fac - hi.astype(F32)).astype(BF16)]
    spread = _dot(jnp.concatenate(pieces, axis=0), hx_ref[...])
    e_all = spread[0:q] + spread[q:2 * q]
    w_all = spread[2 * q:3 * q] + spread[3 * q:4 * q]

    x = x_ref[...]
    for g in range(SSD_GROUPS):
        cg = c_ref[:, g * SSD_STATE:(g + 1) * SSD_STATE]
        bg = b_ref[:, g * SSD_STATE:(g + 1) * SSD_STATE]
        cb = lax.dot_general(cg, bg, (((1,), (1,)), ((), ())), preferred_element_type=F32)
        xg = x[:, g * gw:(g + 1) * gw]
        yg = jnp.zeros((q, gw), F32)
        dec = []
        for hl in range(r):
            h = g * r + hl
            col = a_cs[:, h:h + 1]
            row = a_cs_t[h:h + 1, :]
            lm = jnp.where(tri, jnp.exp(col - row), 0.0)
            m = (cb * lm * dtp_t[h:h + 1, :]).astype(BF16)
            in_head = jnp.logical_and(lane_g >= hl * SSD_HEAD_DIM, lane_g < (hl + 1) * SSD_HEAD_DIM)
            xm = jnp.where(in_head, xg, jnp.zeros_like(xg))
            yg = yg + _dot(m, xm)
            dec.append(jnp.exp(a_end[:, h:h + 1]))
        s_old = state_ref[g]
        gs = slice(g * gw, (g + 1) * gw)
        yg = yg + e_all[:, gs] * _dot(cg, s_old.astype(BF16))
        xw = (xg.astype(F32) * w_all[:, gs]).astype(BF16)
        upd = lax.dot_general(bg, xw, (((0,), (0,)), ((), ())), preferred_element_type=F32)
        state_ref[g] = _head_cols(dec, SSD_HEAD_DIM) * s_old + upd
        sl = slice(g * gw, (g + 1) * gw)
        if final:
            ytot = yprev_ref[:, sl] + yg + dskip_ref[:, sl] * xg.astype(F32)
            tmp_ref[:, sl] = ytot * _silu(z_ref[:, sl].astype(F32))
        else:
            o_ref[:, sl] = yg
    if final:
        gated = tmp_ref[...]
        ms = jnp.mean(gated * gated, axis=-1, keepdims=True)
        o_ref[...] = (gated * lax.rsqrt(ms + EPS) * ng_ref[...]).astype(o_ref.dtype)


def _ssd_scan(xbc, p_main, p_dt, dtb_all, alog_all, l, lat_len, ctx_len, n_batch, reverse, final_args=None):
    nt = xbc.shape[0]
    q = SSD_CHUNK
    ncl, ncc = lat_len // q, ctx_len // q
    ctx0 = n_batch * ncl
    dd = 1 if reverse else 0

    def blk(b, j):
        if reverse:
            return jnp.where(j < ncc, ctx0 + b * ncc + (ncc - 1 - j), b * ncl + (ncl - 1 - (j - ncc)))
        return jnp.where(j < ncc, ctx0 + b * ncc + j, b * ncl + (j - ncc))

    head_vec = pl.BlockSpec((None, 1, LANES), lambda b, j: (2 * l + dd, 0, 0))
    final = final_args is not None
    in_specs = [pl.BlockSpec((q, SSD_INNER), lambda b, j: (blk(b, j), 0)),
                pl.BlockSpec((q, SSD_GROUPS * SSD_STATE), lambda b, j: (blk(b, j), 2)),
                pl.BlockSpec((q, SSD_GROUPS * SSD_STATE), lambda b, j: (blk(b, j), 3)),
                pl.BlockSpec((q, LANES), lambda b, j: (blk(b, j), 0)),
                head_vec, head_vec,
                pl.BlockSpec((LANES, SSD_INNER), lambda b, j: (0, 0))]
    head_spread = (jnp.arange(SSD_INNER, dtype=I32)[None, :] // SSD_HEAD_DIM
                   == jnp.arange(LANES, dtype=I32)[:, None]).astype(BF16)
    args = [xbc, xbc, xbc, p_dt, dtb_all, alog_all, head_spread]
    scratch = [pltpu.VMEM((SSD_GROUPS, SSD_STATE, (SSD_HEADS // SSD_GROUPS) * SSD_HEAD_DIM), F32)]
    if final:
        y_prev, d_skip_all, norm_g_all = final_args
        ds3, ds_spec = _layer_rows(d_skip_all, l)
        ng3, ng_spec = _layer_rows(norm_g_all, l)
        in_specs += [pl.BlockSpec((q, SSD_INNER), lambda b, j: (blk(b, j), 0)),
                     pl.BlockSpec((q, SSD_INNER), lambda b, j: (blk(b, j), COL_Z // SSD_INNER)),
                     ds_spec, ng_spec]
        args += [y_prev, p_main, ds3, ng3]
        scratch.append(pltpu.VMEM((q, SSD_INNER), F32))
    return pl.pallas_call(
        functools.partial(_ssd_kernel, reverse=reverse, final=final),
        out_shape=jax.ShapeDtypeStruct((nt, SSD_INNER), BF16 if final else F32),
        grid=(n_batch, ncc + ncl),
        in_specs=in_specs,
        out_specs=pl.BlockSpec((q, SSD_INNER), lambda b, j: (blk(b, j), 0)),
        scratch_shapes=scratch,
        compiler_params=_cparams(("arbitrary", "arbitrary")),
        name="ssd_scan_rev" if reverse else "ssd_scan_fwd",
    )(*args)


def _glu(ref):
    v = ref[:, 0:CONF_DIM].astype(F32)
    gt = ref[:, CONF_DIM:2 * CONF_DIM].astype(F32)
    return v * jax.nn.sigmoid(gt)


def _conf_kernel(cur_ref, prev_ref, next_ref, dww_ref, dwb_ref, lng_ref, lnb_ref, pww_ref, pwb_ref,
                 og_ref, o_ref, ext_ref, acc_ref, sh_ref, *, tiles_per_lat_seq, n_lat_tiles):
    i = pl.program_id(0)
    first, last = _seq_edges(i, tiles_per_lat_seq, n_lat_tiles)
    h, tm = CONF_HALO, ROW_TILE
    ext_ref[0:h, :] = jnp.where(first, 0.0, _glu(prev_ref))
    ext_ref[h:h + tm, :] = _glu(cur_ref)
    ext_ref[h + tm:h + tm + h, :] = jnp.where(last, 0.0, _glu(next_ref))
    pad = (CONF_KERNEL - 1) // 2
    span = (CONF_KERNEL - 1) // SUBLANES * SUBLANES + tm
    for b in range(SUBLANES):
        sh_ref[b] = ext_ref[h - pad + b:h - pad + b + span, :]
    cw = 256
    for c in range(0, CONF_DIM, cw):
        acc = jnp.broadcast_to(dwb_ref[:, c:c + cw], (tm, cw))
        for j in range(CONF_KERNEL):
            a8 = j // SUBLANES * SUBLANES
            acc = acc + dww_ref[j:j + 1, c:c + cw] * sh_ref[j % SUBLANES, a8:a8 + tm, c:c + cw]
        acc_ref[:, c:c + cw] = acc
    u = acc_ref[...]
    mu = jnp.mean(u, axis=-1, keepdims=True)
    var = jnp.mean(jnp.square(u - mu), axis=-1, keepdims=True)
    y = (u - mu) * lax.rsqrt(var + EPS) * lng_ref[...] + lnb_ref[...]
    y = _silu(y)
    v = _dot(y.astype(BF16), pww_ref[...]) + pwb_ref[...]
    ms = jnp.mean(v * v, axis=-1, keepdims=True)
    o_ref[...] = (v * lax.rsqrt(ms + EPS) * og_ref[...]).astype(o_ref.dtype)


def _conformer(p_main, dw_w, dw_b, ln_g, ln_b, pw_w_bf, pw_b, out_g, l, lat_len, ctx_len, n_batch):
    nt = p_main.shape[0]
    tm, h = ROW_TILE, CONF_HALO
    n_lat_tiles = n_batch * lat_len // tm
    cblk = COL_CONF // (2 * CONF_DIM)
    nhb = nt // h
    kern = functools.partial(_conf_kernel, tiles_per_lat_seq=lat_len // tm, n_lat_tiles=n_lat_tiles)
    rows = [_layer_rows(a, l) for a in (dw_b, ln_g, ln_b)]
    rows2 = [_layer_rows(a, l) for a in (pw_b, out_g)]
    dww, dww_spec = _layer_mat(dw_w, l)
    pww, pww_spec = _layer_mat(pw_w_bf, l)
    return pl.pallas_call(
        kern,
        out_shape=jax.ShapeDtypeStruct((nt, CONF_DIM), BF16),
        grid=(nt // tm,),
        in_specs=[pl.BlockSpec((tm, 2 * CONF_DIM), lambda i: (i, cblk)),
                  pl.BlockSpec((h, 2 * CONF_DIM), lambda i: (jnp.maximum(i * (tm // h) - 1, 0), cblk)),
                  pl.BlockSpec((h, 2 * CONF_DIM), lambda i: (jnp.minimum((i + 1) * (tm // h), nhb - 1), cblk)),
                  dww_spec] + [s for _, s in rows] + [pww_spec] + [s for _, s in rows2],
        out_specs=pl.BlockSpec((tm, CONF_DIM), lambda i: (i, 0)),
        scratch_shapes=[pltpu.VMEM((tm + 2 * h, CONF_DIM), F32), pltpu.VMEM((tm, CONF_DIM), F32),
                        pltpu.VMEM((SUBLANES, (CONF_KERNEL - 1) // SUBLANES * SUBLANES + tm, CONF_DIM), F32)],
        compiler_params=_cparams(("arbitrary",)),
        name="conformer",
    )(p_main, p_main, p_main, dww, *[a for a, _ in rows], pww, *[a for a, _ in rows2])


def _gelu_tanh(v):
    return 0.5 * v * (1.0 + jnp.tanh(math.sqrt(2.0 / math.pi) * (v + 0.044715 * (v * v * v))))


def _s5_kernel(*refs, reverse, final, colmajor, seg):
    if final:
        (u_ref, yprev_ref, bblk_ref, cblk_ref, lam_ref, sin_ref, dskip_ref, gw_ref, gb_ref, og_ref,
         o_ref, sout_ref, h_ref, fin_ref, init_ref, carry_ref, *perm_refs) = refs
    else:
        (u_ref, bblk_ref, cblk_ref, lam_ref, sin_ref,
         o_ref, sout_ref, h_ref, fin_ref, init_ref, carry_ref, *perm_refs) = refs
    ns = S5_NSTATE
    nsub = SUBLANES
    win = seg * nsub
    jw = pl.program_id(1)

    @pl.when(jw == 0)
    def _():
        carry_ref[...] = sin_ref[...]

    n_lb = S5_DIM // LANES

    per_row = GRID_W // seg

    def sub_seg(s):
        return s // per_row, slice((s % per_row) * seg, (s % per_row + 1) * seg)

    def permuted(tile_ref, buf_ref):
        for s in range(nsub):
            g, rows = sub_seg(s)
            for k in range(n_lb):
                buf_ref[k, pl.ds(s, seg, stride=nsub), :] = tile_ref[g, rows, k * LANES:(k + 1) * LANES]
        return jnp.concatenate([buf_ref[k] for k in range(n_lb)], axis=1)

    if colmajor:
        u_win = u_ref[...].reshape(win, S5_DIM)
        prev_win = yprev_ref[...].reshape(win, S5_DIM) if final else None
    else:
        u_win = permuted(u_ref, perm_refs[0])
        prev_win = permuted(yprev_ref, perm_refs[1]) if final else None

    u_bf = u_win.astype(BF16)
    ch_per_tile = 2 * LANES // S5_STATE * S5_GROUP
    for j in range(2 * ns // (2 * LANES)):
        c0 = (j * ch_per_tile) % S5_DIM // LANES * LANES
        h_ref[:, j * 2 * LANES:(j + 1) * 2 * LANES] = _dot(
            u_bf[:, c0:c0 + LANES], bblk_ref[c0:c0 + LANES, j * 2 * LANES:(j + 1) * 2 * LANES])

    cw = 512
    n_chunks = ns // cw

    def lam_chunk(row, c):
        return (jnp.broadcast_to(lam_ref[row:row + 1, c * cw:(c + 1) * cw], (nsub, cw)),
                jnp.broadcast_to(lam_ref[row + 1:row + 2, c * cw:(c + 1) * cw], (nsub, cw)))

    def row0(i):
        step = (seg - 1 - i) if reverse else i
        return pl.multiple_of(step * nsub, nsub)

    for c in range(n_chunks):
        lre, lim = lam_chunk(0, c)
        cre = slice(c * cw, (c + 1) * cw)
        cim = slice(ns + c * cw, ns + (c + 1) * cw)

        def step1(i, hc, cre=cre, cim=cim, lre=lre, lim=lim):
            hre, him = hc
            r0 = row0(i)
            nre = lre * hre - lim * him + h_ref[pl.ds(r0, nsub), cre]
            nim = lre * him + lim * hre + h_ref[pl.ds(r0, nsub), cim]
            h_ref[pl.ds(r0, nsub), cre] = nre
            h_ref[pl.ds(r0, nsub), cim] = nim
            return nre, nim

        z0 = jnp.zeros((nsub, cw), F32)
        fre, fim = lax.fori_loop(0, seg, step1, (z0, z0), unroll=4)
        fin_ref[:, cre] = fre
        fin_ref[:, cim] = fim

    seg_row = {32: 2, 64: 4}[seg]
    gre, gim = lam_ref[seg_row:seg_row + 1, :], lam_ref[seg_row + 1:seg_row + 2, :]
    cur_re, cur_im = carry_ref[:, 0:ns], carry_ref[:, ns:2 * ns]
    order = range(nsub - 1, -1, -1) if reverse else range(nsub)
    for s in order:
        init_ref[s:s + 1, 0:ns] = cur_re
        init_ref[s:s + 1, ns:2 * ns] = cur_im
        f_re, f_im = fin_ref[s:s + 1, 0:ns], fin_ref[s:s + 1, ns:2 * ns]
        cur_re, cur_im = gre * cur_re - gim * cur_im + f_re, gre * cur_im + gim * cur_re + f_im
    carry_ref[:, 0:ns] = cur_re
    carry_ref[:, ns:2 * ns] = cur_im

    for c in range(n_chunks):
        lre, lim = lam_chunk(0, c)
        cre = slice(c * cw, (c + 1) * cw)
        cim = slice(ns + c * cw, ns + (c + 1) * cw)

        def step2(i, gc, cre=cre, cim=cim, lre=lre, lim=lim):
            g_re, g_im = gc
            n_re = lre * g_re - lim * g_im
            n_im = lre * g_im + lim * g_re
            r0 = row0(i)
            h_ref[pl.ds(r0, nsub), cre] = h_ref[pl.ds(r0, nsub), cre] + n_re
            h_ref[pl.ds(r0, nsub), cim] = h_ref[pl.ds(r0, nsub), cim] + n_im
            return n_re, n_im

        lax.fori_loop(0, seg, step2, (init_ref[:, cre], init_ref[:, cim]), unroll=4)

    halves = []
    st_per_tile = 2 * LANES // S5_GROUP * S5_STATE
    for n in range(S5_DIM // (2 * LANES)):
        oc = slice(n * 2 * LANES, (n + 1) * 2 * LANES)
        s_re = slice(n * st_per_tile, (n + 1) * st_per_tile)
        s_im = slice(ns + n * st_per_tile, ns + (n + 1) * st_per_tile)
        halves.append(_dot(h_ref[:, s_re].astype(BF16), cblk_ref[s_re, oc])
                      + _dot(h_ref[:, s_im].astype(BF16), cblk_ref[s_im, oc]))
    y_win = jnp.concatenate(halves, axis=1)

    if final:
        tot = prev_win + y_win + dskip_ref[...] * u_win
        gl = _gelu_tanh(tot)
        gate = jax.nn.sigmoid(_dot(gl.astype(BF16), gw_ref[...]) + gb_ref[...])
        v = gl * gate
        ms = jnp.mean(v * v, axis=-1, keepdims=True)
        y_win = v * lax.rsqrt(ms + EPS) * og_ref[...]

    if colmajor:
        o_ref[...] = y_win.reshape(o_ref.shape).astype(o_ref.dtype)
    else:
        y_ref = perm_refs[0]
        for k in range(n_lb):
            y_ref[k] = y_win[:, k * LANES:(k + 1) * LANES]
        for s in range(nsub):
            g, rows = sub_seg(s)
            for k in range(n_lb):
                o_ref[g, rows, k * LANES:(k + 1) * LANES] = (
                    y_ref[k, pl.ds(s, seg, stride=nsub), :].astype(o_ref.dtype))

    @pl.when(jw == pl.num_programs(1) - 1)
    def _():
        sout_ref[...] = carry_ref[...]


def _s5_scan(u3d, ops, ld, state_in, n_batch, lat_len, reverse, colmajor, final_args=None):
    ns2 = 2 * S5_NSTATE
    bblk, cblk, lam = ops
    if colmajor:
        seg = GRID_W
        tile = (seg, SUBLANES, S5_DIM)
        n_win = GRID_W // SUBLANES
        out_shape = (n_batch * seg, GRID_W, S5_DIM)
        imap = (lambda b, j: (b, n_win - 1 - j, 0)) if reverse else (lambda b, j: (b, j, 0))
        in_map = imap
    else:
        seg = S5_SEG
        tile = (S5_WIN // GRID_W, GRID_W, S5_DIM)
        n_win = 1
        out_shape = (n_batch * S5_WIN // GRID_W, GRID_W, S5_DIM)
        ctx_blk0 = n_batch * lat_len // S5_WIN
        imap = lambda b, j: (b, 0, 0)
        in_map = lambda b, j: (ctx_blk0 + b, 0, 0)
    win = seg * SUBLANES
    st = pl.BlockSpec((None, 1, ns2), lambda b, j: (b, 0, 0))
    stacked = lambda a: pl.BlockSpec((None,) + a.shape[1:], lambda b, j: (ld, 0, 0))
    final = final_args is not None
    in_specs = [pl.BlockSpec(tile, in_map)]
    args = [u3d]
    if final:
        in_specs.append(pl.BlockSpec(tile, imap))
        args.append(final_args[0])
    in_specs += [stacked(bblk), stacked(cblk), stacked(lam), st]
    args += [bblk, cblk, lam, state_in]
    scratch = [pltpu.VMEM((win, ns2), F32),
               pltpu.VMEM((SUBLANES, ns2), F32), pltpu.VMEM((SUBLANES, ns2), F32),
               pltpu.VMEM((1, ns2), F32)]
    if not colmajor:
        perm = pltpu.VMEM((S5_DIM // LANES, win, LANES), F32)
        scratch += [perm, perm] if final else [perm]
    if final:
        _, l, d_skip, glu_w_bf, glu_b, out_g = final_args
        for a in (d_skip,):
            a3, sp = _layer_rows(a, l)
            in_specs.append(sp)
            args.append(a3)
        gw, gw_spec = _layer_mat(glu_w_bf, l)
        in_specs.append(gw_spec)
        args.append(gw)
        for a in (glu_b, out_g):
            a3, sp = _layer_rows(a, l)
            in_specs.append(sp)
            args.append(a3)
    out, s_out = pl.pallas_call(
        functools.partial(_s5_kernel, reverse=reverse, final=final, colmajor=colmajor, seg=seg),
        out_shape=(jax.ShapeDtypeStruct(out_shape, F32),
                   jax.ShapeDtypeStruct((n_batch, 1, ns2), F32)),
        grid=(n_batch, n_win),
        in_specs=in_specs,
        out_specs=(pl.BlockSpec(tile, imap), st),
        scratch_shapes=scratch,
        compiler_params=_cparams(("arbitrary", "arbitrary")),
        name="s5_" + ("rev" if reverse else "fwd") + ("_lat" if colmajor else "_ctx"),
    )(*args)
    return out, s_out


def _s5_operands(lam_re, lam_im, log_step, b_re, b_im, c_re, c_im):
    g, p, k = S5_GROUPS, S5_STATE, S5_GROUP
    lam = lax.complex(jnp.minimum(lam_re.astype(F32), -1e-4), lam_im.astype(F32))
    step = jnp.exp(log_step.astype(F32))[:, None]
    lam_bar = jnp.exp(lam * step)
    lam_seg = jnp.exp(lam * (step * S5_SEG))
    lam_col = jnp.exp(lam * (step * GRID_W))
    b_bar = ((lam_bar - 1.0) / lam)[..., None] * lax.complex(b_re.astype(F32), b_im.astype(F32))
    def block_diag(m):
        a, b = m.shape[1], m.shape[2]
        tiled = jnp.tile(m.reshape(g * a, b), (1, g))
        own = (lax.broadcasted_iota(I32, (g * a, g * b), 0) // a
               == lax.broadcasted_iota(I32, (g * a, g * b), 1) // b)
        return jnp.where(own, tiled, 0.0)

    bd_in = lambda m: block_diag(jnp.transpose(m, (0, 2, 1)))
    bblk = jnp.concatenate([bd_in(jnp.real(b_bar)), bd_in(jnp.imag(b_bar))], axis=1)
    bd_out = lambda m: block_diag(jnp.transpose(m, (0, 2, 1)))
    cblk = jnp.concatenate([bd_out(c_re.astype(F32)), -bd_out(c_im.astype(F32))], axis=0)
    zeros = jnp.zeros((g * p,), F32)
    lam_rows = jnp.stack([jnp.real(lam_bar).reshape(-1), jnp.imag(lam_bar).reshape(-1),
                          jnp.real(lam_seg).reshape(-1), jnp.imag(lam_seg).reshape(-1),
                          jnp.real(lam_col).reshape(-1), jnp.imag(lam_col).reshape(-1), zeros, zeros])
    return bblk.astype(BF16), cblk.astype(BF16), lam_rows


def _pack_bf16_pairs(v):
    n = v.shape[1] // 2
    bits = pltpu.bitcast(v.astype(BF16).astype(F32), U32)
    return (bits[:, :n] >> 16) | (bits[:, n:] & jnp.uint32(0xFFFF0000))


def _unpack_bf16_pairs(w):
    lo = pltpu.bitcast(w << 16, F32)
    hi = pltpu.bitcast(w & jnp.uint32(0xFFFF0000), F32)
    return lo, hi


def _store_token_tiles(ref, v):
    rows = v.shape[0]
    for c in range(SUBLANES):
        ref[pl.ds(c, rows, stride=SUBLANES), :] = v[:, c * LANES:(c + 1) * LANES]


def _load_token_tiles(ref, rows):
    return [ref[pl.ds(c, rows, stride=SUBLANES), :] for c in range(SUBLANES)]


def _mixout_kernel(r_ref, a_ref, b_ref, cl_ref, cc_ref, w_ref, g1_ref, ng_ref, sh_ref,
                   sc_ref, rw_ref, rb_ref, r1_ref, hp_ref, idx_ref, gate_ref, rank_ref, cnt_ref,
                   carry_ref, *, n_lat_tiles):
    i = pl.program_id(0)

    @pl.when(i == 0)
    def _():
        carry_ref[...] = jnp.zeros_like(carry_ref)

    mix_c = jnp.where(i < n_lat_tiles, cl_ref[...], cc_ref[...]).reshape(a_ref.shape[0], S5_DIM).astype(BF16)
    acc = _dot(jnp.concatenate([a_ref[...], b_ref[...], mix_c], axis=1), w_ref[...])
    x = r_ref[...] + g1_ref[...] * acc
    r1_ref[...] = x
    ms = jnp.mean(x * x, axis=-1, keepdims=True)
    h = x * lax.rsqrt(ms + EPS) * ng_ref[...]
    h = h * (1.0 + sc_ref[...]) + sh_ref[...]
    _store_token_tiles(hp_ref, _pack_bf16_pairs(h))

    h_hi = h.astype(BF16)
    h_lo = (h - h_hi.astype(F32)).astype(BF16)
    rw = rw_ref[...]
    w_hi = rw.astype(BF16)
    w_lo = (rw - w_hi.astype(F32)).astype(BF16)
    prod = _dot(jnp.concatenate([h_hi, h_lo], axis=0), jnp.concatenate([w_hi, w_lo], axis=1))
    tm_h = h.shape[0]
    logits = (prod[0:tm_h, 0:LANES] + prod[0:tm_h, LANES:2 * LANES] + prod[tm_h:2 * tm_h, 0:LANES]
              + rb_ref[...])

    tm = logits.shape[0]
    lane = lax.broadcasted_iota(I32, (tm, LANES), 1)
    lane_f = lane.astype(F32)
    work = logits
    tops, picks = [], []
    for _ in range(TOP_K):
        m = jnp.max(work, axis=-1, keepdims=True)
        pick = jnp.min(jnp.where(work == m, lane_f, float(LANES)), axis=-1, keepdims=True)
        work = jnp.where(lane_f == pick, -jnp.inf, work)
        tops.append(m)
        picks.append(pick)
    exps = [jnp.exp(t - tops[0]) for t in tops]
    denom = exps[0] + exps[1] + exps[2] + exps[3]

    onehot = jnp.zeros((tm, LANES), F32)
    for k in range(TOP_K):
        onehot = onehot + jnp.where(lane_f == picks[k], 1.0, 0.0)
    ri = lax.broadcasted_iota(I32, (tm, tm), 0)
    ci = lax.broadcasted_iota(I32, (tm, tm), 1)
    before = jnp.where(ci < ri, 1.0, 0.0).astype(BF16)
    base = carry_ref[0:1, :] + _dot(before, onehot.astype(BF16))
    carry_ref[...] = carry_ref[...] + jnp.sum(onehot, axis=0, keepdims=True)
    cnt_ref[...] = carry_ref[...]

    idx_out = jnp.zeros((tm, LANES), F32)
    gate_out = jnp.zeros((tm, LANES), F32)
    rank_out = jnp.zeros((tm, LANES), F32)
    for k in range(TOP_K):
        rank_k = jnp.sum(jnp.where(lane_f == picks[k], base, 0.0), axis=-1, keepdims=True)
        idx_out = jnp.where(lane == k, picks[k], idx_out)
        gate_out = jnp.where(lane == k, exps[k] / denom, gate_out)
        rank_out = jnp.where(lane == k, rank_k, rank_out)
    gate_ref[...] = gate_out
    idx_t = idx_out.T.astype(I32)
    rank_t = rank_out.T.astype(I32)
    for k in range(TOP_K):
        for hh in range(tm // LANES):
            row = k * (tm // LANES) + hh
            idx_ref[row:row + 1, :] = idx_t[k:k + 1, hh * LANES:(hh + 1) * LANES]
            rank_ref[row:row + 1, :] = rank_t[k:k + 1, hh * LANES:(hh + 1) * LANES]


def _mix_out(r, mix_a, mix_b, mix_c_lat, mix_c_ctx, w_out_bf, mods4, norm2_g, router_w_pad, router_b_pad,
             l, lat_len, n_batch):
    nt, d = r.shape
    tm = ROW_TILE
    n_lat_tiles = n_batch * lat_len // tm

    def grp(i):
        return jnp.minimum((i * tm) // lat_len, n_batch)

    ng3, ng_spec = _layer_rows(norm2_g, l)
    rb3, rb_spec = _layer_rows(router_b_pad, l)
    tile = lambda w: pl.BlockSpec((tm, w), lambda i: (i, 0))
    dense_rows = tm * TOP_K // LANES
    dense = pl.BlockSpec((dense_rows, LANES), lambda i: (i, 0))
    n_dense = nt * TOP_K // LANES
    return pl.pallas_call(
        functools.partial(_mixout_kernel, n_lat_tiles=n_lat_tiles),
        out_shape=(jax.ShapeDtypeStruct((nt, d), F32), jax.ShapeDtypeStruct((nt * SUBLANES, LANES), U32),
                   jax.ShapeDtypeStruct((n_dense, LANES), I32), jax.ShapeDtypeStruct((nt, LANES), F32),
                   jax.ShapeDtypeStruct((n_dense, LANES), I32), jax.ShapeDtypeStruct((SUBLANES, LANES), F32)),
        grid=(nt // tm,),
        in_specs=[tile(d), tile(SSD_INNER), tile(CONF_DIM),
                  pl.BlockSpec((tm // GRID_W, GRID_W, S5_DIM), lambda i: (jnp.minimum(i, n_lat_tiles - 1), 0, 0)),
                  pl.BlockSpec((tm // GRID_W, GRID_W, S5_DIM), lambda i: (jnp.maximum(i - n_lat_tiles, 0), 0, 0)),
                  pl.BlockSpec((None, d, d), lambda i: (l, 0, 0)),
                  _mod_spec(l, grp, 2, d), ng_spec, _mod_spec(l, grp, 3, d), _mod_spec(l, grp, 4, d),
                  pl.BlockSpec((None, d, LANES), lambda i: (l, 0, 0)), rb_spec],
        out_specs=(tile(d), pl.BlockSpec((tm * SUBLANES, LANES), lambda i: (i, 0)), dense, tile(LANES), dense,
                   pl.BlockSpec((SUBLANES, LANES), lambda i: (0, 0))),
        scratch_shapes=[pltpu.VMEM((SUBLANES, LANES), F32)],
        compiler_params=_cparams(("arbitrary",)),
        name="mix_out_router",
    )(r, mix_a, mix_b, mix_c_lat, mix_c_ctx, w_out_bf, mods4, ng3, mods4, mods4,
      router_w_pad, rb3)


def _expert_kernel(be_ref, nv_ref, first_ref, nxt_ref, g_cur_ref, g_nxt_ref, s_cur_ref, s_prv_ref, h_hbm,
                   wgu_hbm, bgu_ref, wd_hbm, bd_ref, ya_hbm, xbuf, ybuf, wgu_st, wd_st, wgu_bf, wd_bf, zbuf,
                   gu_ref, gsem, ssem, wsem, zsem, *, layer):
    i = pl.program_id(0)
    n_steps = pl.num_programs(0)
    slot = i % 2
    nv = nv_ref[i]
    blk = MOE_BLOCK
    tile_rows = blk * SUBLANES

    def weight_copies(e):
        return (pltpu.make_async_copy(wgu_hbm.at[layer, e], wgu_st, wsem.at[0]),
                pltpu.make_async_copy(wd_hbm.at[layer, e], wd_st, wsem.at[1]))

    def token_tile(ref, row0):
        return ref.at[pl.ds(pl.multiple_of(row0, SUBLANES), SUBLANES)]

    def used(j):
        return jnp.logical_and(jnp.logical_and(j >= 0, j < n_steps),
                               nv_ref[jnp.clip(j, 0, n_steps - 1)] > 0)

    row_priority, scatter_priority, weight_priority = 0, 1, 1

    def gather_rows(idx_ref, dst_slot, lo_row, n):
        def body(rr, carry):
            pltpu.make_async_copy(token_tile(h_hbm, idx_ref[0, rr]),
                                  token_tile(xbuf.at[dst_slot], rr * SUBLANES),
                                  gsem.at[dst_slot]).start(priority=row_priority)
            return carry
        lax.fori_loop(lo_row, lo_row + n, body, 0, unroll=16)

    def scatter_rows(idx_ref, src_slot, lo_row, n):
        def body(rr, carry):
            pltpu.make_async_copy(token_tile(ybuf.at[src_slot], rr * SUBLANES),
                                  token_tile(ya_hbm, idx_ref[0, rr]),
                                  ssem.at[src_slot]).start(priority=scatter_priority)
            return carry
        lax.fori_loop(lo_row, lo_row + n, body, 0, unroll=16)

    def scatter_wait(src_slot):
        pltpu.make_async_copy(ybuf.at[src_slot], ya_hbm.at[pl.ds(0, tile_rows)], ssem.at[src_slot]).wait()

    def gather_wait(dst_slot):
        pltpu.make_async_copy(h_hbm.at[pl.ds(0, tile_rows)], xbuf.at[dst_slot], gsem.at[dst_slot]).wait()

    @pl.when(i == 0)
    def _():
        zbuf[...] = jnp.zeros_like(zbuf)
        for cp in weight_copies(be_ref[0]):
            cp.start(priority=weight_priority)
        gather_rows(g_cur_ref, 0, 0, blk)

    def compute_block(with_scatter):
        @pl.when(first_ref[i] == 1)
        def _():
            for cp in weight_copies(be_ref[i]):
                cp.wait()
            wgu_bf[...] = wgu_st[...].astype(BF16)
            wd_bf[...] = wd_st[...].astype(BF16)

            @pl.when(nxt_ref[i] >= 0)
            def _():
                for cp in weight_copies(nxt_ref[i]):
                    cp.start(priority=weight_priority)

        gather_wait(slot)
        half = D_MODEL // 2
        xw = jnp.concatenate(_load_token_tiles(xbuf.at[slot], blk), axis=1)
        lo, hi = _unpack_bf16_pairs(xw)
        lo, hi = lo.astype(BF16), hi.astype(BF16)
        n_col = 2 * D_EXPERT // (2 * LANES)
        bounds = [blk * c // n_col for c in range(n_col + 1)]
        for c in range(n_col):
            cols = slice(c * 2 * LANES, (c + 1) * 2 * LANES)
            gu_ref[:, cols] = (_dot(lo, wgu_bf[0:half, cols]) + _dot(hi, wgu_bf[half:D_MODEL, cols])
                               + bgu_ref[:, cols])
            for rr in range(bounds[c], bounds[c + 1]):
                pltpu.make_async_copy(token_tile(h_hbm, g_nxt_ref[0, rr]),
                                      xbuf.at[1 - slot, pl.ds(rr * SUBLANES, SUBLANES)],
                                      gsem.at[1 - slot]).start(priority=row_priority)
                if with_scatter:
                    pltpu.make_async_copy(ybuf.at[1 - slot, pl.ds(rr * SUBLANES, SUBLANES)],
                                          token_tile(ya_hbm, s_prv_ref[0, rr]),
                                          ssem.at[1 - slot]).start(priority=scatter_priority)
        gu = gu_ref[...]
        gate = jnp.minimum(gu[:, :D_EXPERT], SWIGLU_LIMIT)
        lin = jnp.clip(gu[:, D_EXPERT:], -SWIGLU_LIMIT, SWIGLU_LIMIT)
        act = (gate * jax.nn.sigmoid(SWIGLU_ALPHA * gate) * (lin + 1.0)).astype(BF16)
        y = _dot(act, wd_bf[...]) + bd_ref[...]

        @pl.when(used(i - 2))
        def _():
            scatter_wait(slot)

        _store_token_tiles(ybuf.at[slot], _pack_bf16_pairs(y))

    @pl.when(jnp.logical_and(nv > 0, i == 0))
    def _():
        compute_block(False)

    @pl.when(jnp.logical_and(nv > 0, i > 0))
    def _():
        compute_block(True)

    @pl.when(nv == 0)
    def _():
        gather_wait(slot)
        gather_rows(g_nxt_ref, 1 - slot, 0, blk)

        @pl.when(used(i - 2))
        def _():
            scatter_wait(slot)

        @pl.when(used(i - 1))
        def _():
            scatter_rows(s_prv_ref, 1 - slot, 0, blk)

        own = ya_hbm.at[pl.ds(pl.multiple_of(i * tile_rows, tile_rows), tile_rows)]
        zero_copy = pltpu.make_async_copy(zbuf, own, zsem)
        zero_copy.start()
        zero_copy.wait()

    @pl.when(i == n_steps - 1)
    def _():
        gather_wait(1 - slot)

        @pl.when(nv > 0)
        def _():
            scatter_rows(s_cur_ref, slot, 0, blk)
            scatter_wait(slot)

        @pl.when(used(i - 1))
        def _():
            scatter_wait(1 - slot)


def _experts(h_tiles, gather_row, scatter_row, block_e, n_valid, first, nxt, w_gu, b_gu, w_down, b_down,
             layer):
    n_rows = gather_row.shape[0]
    n_blocks = n_rows // MOE_BLOCK
    depth, ne = w_gu.shape[:2]
    g3 = gather_row.reshape(n_blocks, 1, MOE_BLOCK)
    s3 = scatter_row.reshape(n_blocks, 1, MOE_BLOCK)
    width = D_MODEL // 2
    tile_rows = MOE_BLOCK * SUBLANES
    idx_block = lambda imap: pl.BlockSpec((None, 1, MOE_BLOCK), imap, memory_space=pltpu.SMEM)
    grid_spec = pltpu.PrefetchScalarGridSpec(
        num_scalar_prefetch=4,
        grid=(n_blocks,),
        in_specs=[idx_block(lambda i, *_: (i, 0, 0)),
                  idx_block(lambda i, *_: (jnp.minimum(i + 1, n_blocks - 1), 0, 0)),
                  idx_block(lambda i, *_: (i, 0, 0)),
                  idx_block(lambda i, *_: (jnp.maximum(i - 1, 0), 0, 0)),
                  pl.BlockSpec(memory_space=pl.ANY),
                  pl.BlockSpec(memory_space=pl.ANY),
                  pl.BlockSpec((None, None, 1, 2 * D_EXPERT), lambda i, be, *_: (layer, be[i], 0, 0)),
                  pl.BlockSpec(memory_space=pl.ANY),
                  pl.BlockSpec((None, None, 1, D_MODEL), lambda i, be, *_: (layer, be[i], 0, 0))],
        out_specs=pl.BlockSpec(memory_space=pl.ANY),
        scratch_shapes=[pltpu.VMEM((2, tile_rows, LANES), U32), pltpu.VMEM((2, tile_rows, LANES), U32),
                        pltpu.VMEM((D_MODEL, 2 * D_EXPERT), F32), pltpu.VMEM((D_EXPERT, D_MODEL), F32),
                        pltpu.VMEM((D_MODEL, 2 * D_EXPERT), BF16), pltpu.VMEM((D_EXPERT, D_MODEL), BF16),
                        pltpu.VMEM((tile_rows, LANES), U32), pltpu.VMEM((MOE_BLOCK, 2 * D_EXPERT), F32),
                        pltpu.SemaphoreType.DMA((2,)), pltpu.SemaphoreType.DMA((2,)),
                        pltpu.SemaphoreType.DMA((2,)), pltpu.SemaphoreType.DMA],
    )
    return pl.pallas_call(
        functools.partial(_expert_kernel, layer=layer),
        out_shape=jax.ShapeDtypeStruct((n_rows * SUBLANES, LANES), U32),
        grid_spec=grid_spec,
        compiler_params=_cparams(("arbitrary",)),
        name="moe_experts",
    )(block_e, n_valid, first, nxt, g3, g3, s3, s3, h_tiles, w_gu, b_gu.reshape(depth, ne, 1, -1), w_down,
      b_down.reshape(depth, ne, 1, -1))


def _combine_kernel(r_ref, gate_ref, y0_ref, y1_ref, y2_ref, y3_ref, g2_ref, fg_ref, o_ref, *, last_layer):
    tm = r_ref.shape[0]
    half = D_MODEL // 2
    gates = [gate_ref[:, k:k + 1] for k in range(TOP_K)]
    y_refs = (y0_ref, y1_ref, y2_ref, y3_ref)
    for c in range(SUBLANES):
        acc_lo = jnp.zeros((tm, LANES), F32)
        acc_hi = jnp.zeros((tm, LANES), F32)
        for k in range(TOP_K):
            lo, hi = _unpack_bf16_pairs(y_refs[k][pl.ds(c, tm, stride=SUBLANES), :])
            acc_lo = acc_lo + gates[k] * lo
            acc_hi = acc_hi + gates[k] * hi
        lo_cols = slice(c * LANES, (c + 1) * LANES)
        hi_cols = slice(half + c * LANES, half + (c + 1) * LANES)
        o_ref[:, lo_cols] = r_ref[:, lo_cols] + g2_ref[:, lo_cols] * acc_lo
        o_ref[:, hi_cols] = r_ref[:, hi_cols] + g2_ref[:, hi_cols] * acc_hi
    if last_layer:
        x = o_ref[...]
        ms = jnp.mean(x * x, axis=-1, keepdims=True)
        o_ref[...] = x * lax.rsqrt(ms + EPS) * fg_ref[...]


def _combine(r1, gates, ya, mods4, final_g, l, lat_len, n_batch, last_layer):
    nt, d = r1.shape
    tm = ROW_TILE
    n_out = n_batch * lat_len if last_layer else nt
    tiles_per_k = nt // tm

    def grp(i):
        return jnp.minimum((i * tm) // lat_len, n_batch)

    def choice(k):
        return pl.BlockSpec((tm * SUBLANES, LANES), lambda i: (k * tiles_per_k + i, 0))

    return pl.pallas_call(
        functools.partial(_combine_kernel, last_layer=last_layer),
        out_shape=jax.ShapeDtypeStruct((n_out, d), F32),
        grid=(n_out // tm,),
        in_specs=[pl.BlockSpec((tm, d), lambda i: (i, 0)),
                  pl.BlockSpec((tm, LANES), lambda i: (i, 0)),
                  choice(0), choice(1), choice(2), choice(3),
                  _mod_spec(l, grp, 5, d),
                  pl.BlockSpec((1, d), lambda i: (0, 0))],
        out_specs=pl.BlockSpec((tm, d), lambda i: (i, 0)),
        compiler_params=_cparams(("arbitrary",)),
        name="moe_combine",
    )(r1, gates, ya, ya, ya, ya, mods4, final_g.reshape(1, d))


def _routing_plan(top_idx, rank, counts, n_blocks):
    n_assign = top_idx.size
    n_tok = n_assign // TOP_K
    n_rows = n_blocks * MOE_BLOCK
    counts = counts.astype(I32)
    padded = (counts + MOE_BLOCK - 1) // MOE_BLOCK * MOE_BLOCK
    pad_end = jnp.cumsum(padded)
    pad_start = pad_end - padded
    count_end = jnp.cumsum(counts)
    per_tile = ROW_TILE * TOP_K // LANES
    q = lax.broadcasted_iota(I32, top_idx.shape, 0)
    lane = lax.broadcasted_iota(I32, top_idx.shape, 1)
    halves = ROW_TILE // LANES
    token = (q // per_tile) * ROW_TILE + (q % halves) * LANES + lane
    choice = (q % per_tile) // halves
    out_tile = choice * n_tok + token
    dest = pad_start[top_idx] + rank
    blk_start = jnp.arange(n_blocks, dtype=I32) * MOE_BLOCK

    def expert_at(pos):
        return jnp.minimum(jnp.sum((pad_end[None, :] <= pos[:, None]).astype(I32), axis=1), N_EXPERTS - 1)

    block_e = expert_at(blk_start)
    used = blk_start < pad_end[-1]
    n_valid = jnp.where(used, jnp.clip(pad_start[block_e] + counts[block_e] - blk_start, 0, MOE_BLOCK), 0)
    first = jnp.logical_and(used, blk_start == pad_start[block_e]).astype(I32)
    nxt_start = pad_end[block_e]
    nxt = jnp.where(nxt_start < pad_end[-1], expert_at(nxt_start), -1).astype(I32)

    row = jnp.arange(n_rows, dtype=I32)
    dump = n_assign + row - jnp.repeat(count_end[block_e], MOE_BLOCK)
    row_tile = dump.at[dest].set(out_tile, unique_indices=True, mode='drop')
    gather_row = (row_tile % n_tok) * SUBLANES
    scatter_row = row_tile * SUBLANES
    return gather_row.astype(I32), scatter_row.astype(I32), block_e.astype(I32), n_valid.astype(I32), first, nxt


def _forward(x, c, ctx, c_ctx, ada_w, ada_b, norm1_g, w_in, ssd_conv_w, ssd_conv_b, ssd_a_log,
             ssd_dt_bias, ssd_d, ssd_norm_g, conf_dw_w, conf_dw_b, conf_ln_g, conf_ln_b, conf_pw_w,
             conf_pw_b, conf_out_g, s5_lam_re, s5_lam_im, s5_log_step, s5_b_re, s5_b_im, s5_c_re,
             s5_c_im, s5_d, s5_glu_w, s5_glu_b, s5_out_g, w_out, norm2_g, router_w, router_b,
             w_gate_up, b_gate_up, w_down, b_down, final_norm_g):
    n_batch, lat_len, d = x.shape
    ctx_len = ctx.shape[1]
    depth = ada_w.shape[0]
    n_lat = n_batch * lat_len
    nt = n_lat + n_batch * ctx_len
    assert d == D_MODEL and ctx_len == ROW_TILE and lat_len % MM_TILE_M == 0
    assert lat_len // GRID_W == GRID_W

    r = jnp.concatenate([x.reshape(n_lat, d), ctx.reshape(n_batch * ctx_len, d)], axis=0).astype(F32)
    cond = jnp.zeros((SUBLANES, d), F32).at[:n_batch].set(c).at[n_batch].set(c_ctx)
    mods4 = _modulation(cond, ada_w, ada_b).reshape(depth, SUBLANES, 1, 6 * d)

    c_dt = SSD_INNER + SSD_CONV_DIM
    c_conf = c_dt + SSD_HEADS
    c_s5 = c_conf + 2 * CONF_DIM
    w_proj = jnp.concatenate(
        [w_in[:, :, :c_dt], w_in[:, :, c_conf:c_s5], w_in[:, :, c_s5:], w_in[:, :, c_dt:c_conf],
         jnp.zeros((depth, d, MM_TILE_N - S5_DIM - SSD_HEADS), w_in.dtype)], axis=2).astype(BF16)
    w_out_bf = w_out.astype(BF16)
    conf_pw_bf = conf_pw_w.astype(BF16)
    s5_glu_bf = s5_glu_w.astype(BF16)
    head_pad = lambda a: jnp.pad(a.astype(F32), ((0, 0), (0, 0), (0, LANES - SSD_HEADS))).reshape(
        depth * 2, 1, LANES)
    dtb_all, alog_all = head_pad(ssd_dt_bias), head_pad(ssd_a_log)
    d_skip_all = jnp.repeat(ssd_d.astype(F32), SSD_HEAD_DIM, axis=1)
    router_w_pad = jnp.pad(router_w.astype(F32), ((0, 0), (0, 0), (0, LANES - N_EXPERTS)))
    router_b_pad = jnp.pad(router_b.astype(F32), ((0, 0), (0, LANES - N_EXPERTS)), constant_values=-1e30)
    flat2 = lambda a: a.reshape((depth * 2,) + a.shape[2:])
    s5_ops = jax.vmap(_s5_operands)(*[flat2(a) for a in (s5_lam_re, s5_lam_im, s5_log_step, s5_b_re,
                                                         s5_b_im, s5_c_re, s5_c_im)])

    n_blocks = nt * TOP_K // MOE_BLOCK + N_EXPERTS
    s5_zero = jnp.zeros((n_batch, 1, 2 * S5_NSTATE), F32)

    for l in range(depth):
        p_main, p_s5, p_dt = _in_proj(r, norm1_g, mods4, l, w_proj, lat_len, n_batch)

        xbc = _ssd_conv(p_main, ssd_conv_w, ssd_conv_b, l, lat_len, ctx_len, n_batch)
        y_fwd = _ssd_scan(xbc, p_main, p_dt, dtb_all, alog_all, l, lat_len, ctx_len, n_batch, False)
        mix_a = _ssd_scan(xbc, p_main, p_dt, dtb_all, alog_all, l, lat_len, ctx_len, n_batch, True,
                          (y_fwd, d_skip_all, ssd_norm_g))

        mix_b = _conformer(p_main, conf_dw_w, conf_dw_b, conf_ln_g, conf_ln_b, conf_pw_bf, conf_pw_b,
                           conf_out_g, l, lat_len, ctx_len, n_batch)

        yc_f, st_f = _s5_scan(p_s5, s5_ops, 2 * l, s5_zero, n_batch, lat_len, False, False)
        yl_f, _ = _s5_scan(p_s5, s5_ops, 2 * l, st_f, n_batch, lat_len, False, True)
        fin = (l, s5_d, s5_glu_bf, s5_glu_b, s5_out_g)
        mc_c, st_r = _s5_scan(p_s5, s5_ops, 2 * l + 1, s5_zero, n_batch, lat_len, True, False, (yc_f,) + fin)
        mc_l, _ = _s5_scan(p_s5, s5_ops, 2 * l + 1, st_r, n_batch, lat_len, True, True, (yl_f,) + fin)

        r1, h_tiles, top_idx, gates, rank, counts = _mix_out(
            r, mix_a, mix_b, mc_l, mc_c, w_out_bf, mods4, norm2_g, router_w_pad,
            router_b_pad, l, lat_len, n_batch)

        gather_row, scatter_row, block_e, n_valid, first, nxt = _routing_plan(
            top_idx, rank, counts[0, :N_EXPERTS], n_blocks)
        ya = _experts(h_tiles, gather_row, scatter_row, block_e, n_valid, first, nxt, w_gate_up, b_gate_up,
                      w_down, b_down, l)
        r = _combine(r1, gates, ya, mods4, final_norm_g, l, lat_len, n_batch, l == depth - 1)

    return r.reshape(n_batch, lat_len, d).astype(x.dtype)


def kernel(x, c, ctx, c_ctx, ada_w, ada_b, norm1_g, w_in, ssd_conv_w, ssd_conv_b, ssd_a_log, ssd_dt_bias,
           ssd_d, ssd_norm_g, conf_dw_w, conf_dw_b, conf_ln_g, conf_ln_b, conf_pw_w, conf_pw_b, conf_out_g,
           s5_lam_re, s5_lam_im, s5_log_step, s5_b_re, s5_b_im, s5_c_re, s5_c_im, s5_d, s5_glu_w, s5_glu_b,
           s5_out_g, w_out, norm2_g, router_w, router_b, w_gate_up, b_gate_up, w_down, b_down, final_norm_g):
    return _forward(x, c, ctx, c_ctx, ada_w, ada_b, norm1_g, w_in, ssd_conv_w, ssd_conv_b, ssd_a_log,
                    ssd_dt_bias, ssd_d, ssd_norm_g, conf_dw_w, conf_dw_b, conf_ln_g, conf_ln_b, conf_pw_w,
                    conf_pw_b, conf_out_g, s5_lam_re, s5_lam_im, s5_log_step, s5_b_re, s5_b_im, s5_c_re,
                    s5_c_im, s5_d, s5_glu_w, s5_glu_b, s5_out_g, w_out, norm2_g, router_w, router_b,
                    w_gate_up, b_gate_up, w_down, b_down, final_norm_g)
```
